```python
import math
import jax
import jax.numpy as jnp
from jax import lax
import numpy as np

D_MODEL = 2048
BATCH = 4
SEQ = 4096
DEPTH = 4

N_A_LAYERS = DEPTH // 2
N_B_LAYERS = DEPTH - N_A_LAYERS
RMS_EPS = 1e-6

GDN_K_DIM = 128
GDN_V_DIM = 128
GDN_K_HEADS = D_MODEL // GDN_K_DIM
GDN_V_HEADS = 2 * GDN_K_HEADS
GDN_QK_WIDTH = GDN_K_HEADS * GDN_K_DIM
GDN_V_WIDTH = GDN_V_HEADS * GDN_V_DIM
GDN_CONV_CH = 2 * GDN_QK_WIDTH + GDN_V_WIDTH
GDN_IN_WIDTH = GDN_CONV_CH + GDN_V_WIDTH + 2 * GDN_V_HEADS
CONV_WIDTH = 4
CHUNK = 64

DIL_HEAD_DIM = 128
DIL_HEADS = D_MODEL // DIL_HEAD_DIM
DILATION_GROUPS = ((128, 1), (512, 4), (2048, 16))
N_GROUPS = len(DILATION_GROUPS)
DIL_WIDTH = DIL_HEADS * DIL_HEAD_DIM
DIL_Q_WIDTH = N_GROUPS * DIL_WIDTH
DIL_IN_WIDTH = DIL_Q_WIDTH + DIL_WIDTH
N_BUCKETS = 32
MAX_DISTANCE = 2048

kernel_name = "yoco_deltanet_dilated_swa_trunk"


def rmsnorm(x, gain):
    x32 = x.astype(jnp.float32)
    y = x32 * lax.rsqrt(jnp.mean(x32 * x32, axis=-1, keepdims=True) + RMS_EPS)
    return y * gain.astype(jnp.float32)


def modulated_norm(x, gain, shift, scale):
    y = rmsnorm(x, gain) * (1.0 + scale[:, None, :].astype(jnp.float32)) + shift[:, None, :].astype(jnp.float32)
    return y.astype(x.dtype)


def l2norm(x, eps=1e-6):
    return x * lax.rsqrt(jnp.sum(x * x, axis=-1, keepdims=True) + eps)


def causal_depthwise_conv(x, w):
    k_width, ch = w.shape
    return lax.conv_general_dilated(
        x, w[:, None, :].astype(x.dtype), window_strides=(1,),
        padding=((k_width - 1, 0),), dimension_numbers=("NWC", "WIO", "NWC"),
        feature_group_count=ch)


def chunk_gated_delta_rule(q, k, v, g, beta):
    bsz, seq, heads, dk = q.shape
    dv = v.shape[-1]
    n_chunks = seq // CHUNK

    def blocks(t):
        t = jnp.moveaxis(t.astype(jnp.float32), 1, 2)
        return t.reshape(bsz, heads, n_chunks, CHUNK, *t.shape[3:])

    q, k, v, g, beta = blocks(q), blocks(k), blocks(v), blocks(g), blocks(beta)
    g = jnp.cumsum(g, axis=-1)
    causal = jnp.tril(jnp.ones((CHUNK, CHUNK), dtype=bool))
    strict = jnp.tril(jnp.ones((CHUNK, CHUNK), dtype=bool), -1)
    decay = jnp.exp(jnp.where(causal, g[..., :, None] - g[..., None, :], -jnp.inf))
    kb = k * beta[..., None]
    lmat = jnp.where(strict, jnp.einsum("bhnie,bhnje->bhnij", kb, k) * decay, 0.0)
    tmat = lmat + jnp.eye(CHUNK, dtype=jnp.float32)
    rhs = jnp.concatenate([v * beta[..., None], kb * jnp.exp(g)[..., None]], axis=-1)
    sol = lax.linalg.triangular_solve(tmat, rhs, left_side=True, lower=True,
                                      unit_diagonal=True)
    u, w = sol[..., :dv], sol[..., dv:]
    attn = jnp.einsum("bhnie,bhnje->bhnij", q, k) * decay
    g_last = g[..., -1]
    q_dec = q * jnp.exp(g)[..., None]
    k_dec = k * jnp.exp(g_last[..., None] - g)[..., None]

    xs = tuple(jnp.moveaxis(t, 2, 0) for t in (u, w, q_dec, k_dec, attn, g_last))

    def step(state, inp):
        u_n, w_n, qd_n, kd_n, at_n, gl_n = inp
        v_new = u_n - jnp.einsum("bhce,bhef->bhcf", w_n, state)
        o_n = (jnp.einsum("bhce,bhef->bhcf", qd_n, state)
               + jnp.einsum("bhij,bhjf->bhif", at_n, v_new))
        state = (state * jnp.exp(gl_n)[..., None, None]
                 + jnp.einsum("bhce,bhcf->bhef", kd_n, v_new))
        return state, o_n

    state0 = jnp.zeros((bsz, heads, dk, dv), jnp.float32)
    _, o = lax.scan(step, state0, xs)
    o = jnp.moveaxis(o, 0, 2).reshape(bsz, heads, seq, dv)
    return jnp.moveaxis(o, 1, 2)


def gated_deltanet_mixer(h, w_in, conv_w, a_log, dt_bias, o_gain, w_out):
    bsz, seq, _ = h.shape
    proj = h @ w_in.astype(h.dtype)
    qkv, z, b, a = jnp.split(
        proj, [GDN_CONV_CH, GDN_CONV_CH + GDN_V_WIDTH,
               GDN_CONV_CH + GDN_V_WIDTH + GDN_V_HEADS], axis=-1)
    qkv = jax.nn.silu(causal_depthwise_conv(qkv, conv_w)).astype(jnp.float32)
    q, k, v = jnp.split(qkv, [GDN_QK_WIDTH, 2 * GDN_QK_WIDTH], axis=-1)
    q = l2norm(q.reshape(bsz, seq, GDN_K_HEADS, GDN_K_DIM)) * (GDN_K_DIM ** -0.5)
    k = l2norm(k.reshape(bsz, seq, GDN_K_HEADS, GDN_K_DIM))
    v = v.reshape(bsz, seq, GDN_V_HEADS, GDN_V_DIM)
    rep = GDN_V_HEADS // GDN_K_HEADS
    q = jnp.repeat(q, rep, axis=2)
    k = jnp.repeat(k, rep, axis=2)
    beta = jax.nn.sigmoid(b.astype(jnp.float32))
    g = -jnp.exp(a_log.astype(jnp.float32)) * jax.nn.softplus(
        a.astype(jnp.float32) + dt_bias.astype(jnp.float32))
    o = chunk_gated_delta_rule(q, k, v, g, beta)
    o = o * lax.rsqrt(jnp.mean(o * o, axis=-1, keepdims=True) + RMS_EPS) * o_gain.astype(jnp.float32)
    o = o * jax.nn.silu(z.astype(jnp.float32).reshape(bsz, seq, GDN_V_HEADS, GDN_V_DIM))
    return o.reshape(bsz, seq, GDN_V_WIDTH).astype(h.dtype) @ w_out.astype(h.dtype)


def t5_bucket(dist):
    max_exact = N_BUCKETS // 2
    n = jnp.maximum(dist, 0)
    large = max_exact + (jnp.log(jnp.maximum(n, 1).astype(jnp.float32) / max_exact)
                         / math.log(MAX_DISTANCE / max_exact)
                         * (N_BUCKETS - max_exact)).astype(jnp.int32)
    large = jnp.minimum(large, N_BUCKETS - 1)
    return jnp.where(n < max_exact, n, large)


def dilated_group_attention(q, k, v, window, dilation, bias_table):
    bsz, seq, heads, dh = q.shape
    blk = window // dilation
    sub_len = seq // dilation
    n_blk = -(-sub_len // blk)
    sub_pad = n_blk * blk

    def strided(t):
        t = t.reshape(bsz, sub_len, dilation, heads, dh).transpose(0, 2, 3, 1, 4)
        t = jnp.pad(t, ((0, 0), (0, 0), (0, 0), (0, sub_pad - sub_len), (0, 0)))
        return t.reshape(bsz, dilation, heads, n_blk, blk, dh)

    def with_prev(t):
        prev = jnp.pad(t, ((0, 0), (0, 0), (0, 0), (1, 0), (0, 0), (0, 0)))[:, :, :, :-1]
        return jnp.concatenate([prev, t], axis=-2)

    qs = strided(q)
    kk = with_prev(strided(k))
    vv = with_prev(strided(v))
    qi = jnp.arange(blk)[:, None]
    kj = jnp.arange(2 * blk)[None, :]
    rel = qi + blk - kj
    blk_idx = jnp.arange(n_blk)[:, None, None]
    valid = (rel >= 0) & (rel <= blk) & (blk_idx * blk + kj - blk >= 0)
    bias = jnp.transpose(bias_table.astype(jnp.float32)[t5_bucket(rel * dilation)], (2, 0, 1))
    s = jnp.einsum("bdhnqe,bdhnke->bdhnqk", qs, kk) * (dh ** -0.5) + bias[:, None]
    s = jnp.where(valid, s, -jnp.inf)
    m = jnp.max(s, axis=-1, keepdims=True)
    p = jnp.exp(s - m)
    den = jnp.sum(p, axis=-1)
    o = jnp.einsum("bdhnqk,bdhnke->bdhnqe", p, vv) / den[..., None]
    lse = m[..., 0] + jnp.log(den)
    o = o.reshape(bsz, dilation, heads, sub_pad, dh)[:, :, :, :sub_len]
    o = o.transpose(0, 3, 1, 2, 4).reshape(bsz, seq, heads, dh)
    lse = lse.reshape(bsz, dilation, heads, sub_pad)[..., :sub_len]
    lse = lse.transpose(0, 3, 1, 2).reshape(bsz, seq, heads)
    return o, lse


def dilated_mixer(h, k_sh, v_sh, w_in, w_out, rel_bias):
    bsz, seq, _ = h.shape
    proj = h @ w_in.astype(h.dtype)
    q_all = proj[..., :DIL_Q_WIDTH].astype(jnp.float32)
    z = proj[..., DIL_Q_WIDTH:].astype(jnp.float32)
    outs, lses = [], []
    for gi, (window, dilation) in enumerate(DILATION_GROUPS):
        q = q_all[..., gi * DIL_WIDTH:(gi + 1) * DIL_WIDTH].reshape(bsz, seq, DIL_HEADS, DIL_HEAD_DIM)
        o, lse = dilated_group_attention(
            q, k_sh, v_sh, window, dilation,
            rel_bias[:, gi * DIL_HEADS:(gi + 1) * DIL_HEADS])
        outs.append(o)
        lses.append(lse)
    wts = jax.nn.softmax(jnp.stack(lses, axis=0), axis=0)
    o = jnp.sum(wts[..., None] * jnp.stack(outs, axis=0), axis=0)
    y = o.reshape(bsz, seq, DIL_WIDTH) * jax.nn.silu(z)
    return y.astype(h.dtype) @ w_out.astype(h.dtype)


def setup_inputs(seed: int = 0) -> dict:
    key = jax.random.key(seed)
    ks = jax.random.split(key, 20)

    def normal(k, shape, scale):
        return jax.random.normal(k, shape, jnp.float32) * scale

    dt = jnp.exp(jax.random.uniform(ks[8], (N_A_LAYERS, GDN_V_HEADS), jnp.float32,
                                    minval=math.log(1e-3), maxval=math.log(1e-1)))
    return {
        "x": normal(ks[0], (BATCH, SEQ, D_MODEL), 1.0),
        "c": normal(ks[1], (BATCH, D_MODEL), 1.0),
        "norm_gain": 1.0 + normal(ks[2], (DEPTH, D_MODEL), 0.02),
        "w_mod": normal(ks[3], (DEPTH, D_MODEL, 3 * D_MODEL), 0.5 * D_MODEL ** -0.5),
        "b_mod": normal(ks[4], (DEPTH, 3 * D_MODEL), 0.02),
        "w_in_a": normal(ks[5], (N_A_LAYERS, D_MODEL, GDN_IN_WIDTH), D_MODEL ** -0.5),
        "conv_w_a": normal(ks[6], (N_A_LAYERS, CONV_WIDTH, GDN_CONV_CH), CONV_WIDTH ** -0.5),
        "a_log": jnp.log(jax.random.uniform(ks[7], (N_A_LAYERS, GDN_V_HEADS), jnp.float32,
                                            minval=1.0, maxval=16.0)),
        "dt_bias": dt + jnp.log(-jnp.expm1(-dt)),
        "o_norm_a": 1.0 + normal(ks[9], (N_A_LAYERS, GDN_V_DIM), 0.02),
        "w_out_a": normal(ks[10], (N_A_LAYERS, GDN_V_WIDTH, D_MODEL), GDN_V_WIDTH ** -0.5),
        "kv_gain": 1.0 + normal(ks[11], (D_MODEL,), 0.02),
        "w_kv_mod": normal(ks[12], (D_MODEL, 2 * D_MODEL), 0.5 * D_MODEL ** -0.5),
        "b_kv_mod": normal(ks[13], (2 * D_MODEL,), 0.02),
        "w_kv": normal(ks[14], (D_MODEL, 2 * DIL_WIDTH), D_MODEL ** -0.5),
        "w_in_b": normal(ks[15], (N_B_LAYERS, D_MODEL, DIL_IN_WIDTH), D_MODEL ** -0.5),
        "w_out_b": normal(ks[16], (N_B_LAYERS, DIL_WIDTH, D_MODEL), DIL_WIDTH ** -0.5),
        "rel_bias": normal(ks[17], (N_BUCKETS, N_GROUPS * DIL_HEADS), 0.5),
        "final_gain": 1.0 + normal(ks[18], (D_MODEL,), 0.02),
    }


def reference(x, c, norm_gain, w_mod, b_mod, w_in_a, conv_w_a, a_log, dt_bias,
              o_norm_a, w_out_a, kv_gain, w_kv_mod, b_kv_mod, w_kv, w_in_b,
              w_out_b, rel_bias, final_gain):
    bsz, seq, _ = x.shape
    c_act = jax.nn.silu(c)
    mods = jnp.einsum("bd,lde->lbe", c_act, w_mod) + b_mod[:, None, :]
    k_sh = v_sh = None
    for layer in range(DEPTH):
        shift, scale, gate = jnp.split(mods[layer], 3, axis=-1)
        h = modulated_norm(x, norm_gain[layer], shift, scale)
        if layer < N_A_LAYERS:
            y = gated_deltanet_mixer(h, w_in_a[layer], conv_w_a[layer], a_log[layer],
                                     dt_bias[layer], o_norm_a[layer], w_out_a[layer])
        else:
            if layer == N_A_LAYERS:
                kv_shift, kv_scale = jnp.split(c_act @ w_kv_mod + b_kv_mod, 2, axis=-1)
                hk = modulated_norm(x, kv_gain, kv_shift, kv_scale)
                kv = (hk @ w_kv.astype(hk.dtype)).astype(jnp.float32)
                k_sh = kv[..., :DIL_WIDTH].reshape(bsz, seq, DIL_HEADS, DIL_HEAD_DIM)
                v_sh = kv[..., DIL_WIDTH:].reshape(bsz, seq, DIL_HEADS, DIL_HEAD_DIM)
            j = layer - N_A_LAYERS
            y = dilated_mixer(h, k_sh, v_sh, w_in_b[j], w_out_b[j], rel_bias)
        x = x + (gate[:, None, :] * y).astype(x.dtype)
    return rmsnorm(x, final_gain).astype(x.dtype)
```

```python
import functools
import math

import jax
import jax.numpy as jnp
from jax import lax
from jax.experimental import pallas as pl
from jax.experimental.pallas import tpu as pltpu

F32 = jnp.float32
BF16 = jnp.bfloat16

RMS_EPS = 1e-6
L2_EPS = 1e-6
HEAD = 128
BF16_SUBLANES = 16
CHUNK = 64
CONV_WIDTH = 4
DILATION_GROUPS = ((128, 1), (512, 4), (2048, 16))
BAND = 128
ATT_TILE = 2048
N_BUCKETS = 32
MAX_DISTANCE = 2048
NEG = -1e30
VMEM_LIMIT = 56 * 1024 * 1024


def _cparams(n_axes):
    return pltpu.CompilerParams(
        dimension_semantics=("arbitrary",) * n_axes, vmem_limit_bytes=VMEM_LIMIT)


def _dot(a, b):
    return jnp.dot(a, b, preferred_element_type=F32)


def _dot_nt(a, b):
    return lax.dot_general(a, b, (((1,), (1,)), ((), ())), preferred_element_type=F32)


def _dot_tn(a, b):
    return lax.dot_general(a, b, (((0,), (0,)), ((), ())), preferred_element_type=F32)


def _sigmoid(x):
    return 1.0 / (1.0 + jnp.exp(-x))


def _silu(x):
    return x * _sigmoid(x)


def _softplus(x):
    return jnp.maximum(x, 0.0) + jnp.log(1.0 + jnp.exp(-jnp.abs(x)))


def _split3(x):
    hi = x.astype(BF16)
    r1 = x - hi.astype(F32)
    mid = r1.astype(BF16)
    lo = (r1 - mid.astype(F32)).astype(BF16)
    return hi, mid, lo


def _pick(n, prefs):
    for p in prefs:
        if n % p == 0:
            return p
    return n


def _mod_kernel(c_ref, w_ref, b_ref, o_ref):
    ca = _silu(c_ref[...])
    w = w_ref[...]
    c_hi, c_mid, c_lo = _split3(ca)
    w_hi = w.astype(BF16)
    w_lo = (w - w_hi.astype(F32)).astype(BF16)
    acc = _dot(c_hi, w_hi) + _dot(c_hi, w_lo) + _dot(c_mid, w_hi) + _dot(c_lo, w_hi)
    o_ref[...] = acc + b_ref[...]


def _mod_proj(c, w, b):
    nl, d, n = w.shape
    n_rows = c.shape[0]
    bsz = -(-n_rows // BF16_SUBLANES) * BF16_SUBLANES
    c = jnp.pad(c, ((0, bsz - n_rows), (0, 0)))
    tn = _pick(n, (512, 256, 128))
    out = pl.pallas_call(
        _mod_kernel,
        grid=(nl, n // tn),
        in_specs=[
            pl.BlockSpec((bsz, d), lambda l, j: (0, 0)),
            pl.BlockSpec((None, d, tn), lambda l, j: (l, 0, j)),
            pl.BlockSpec((None, 1, tn), lambda l, j: (l, 0, j)),
        ],
        out_specs=pl.BlockSpec((None, bsz, tn), lambda l, j: (l, 0, j)),
        out_shape=jax.ShapeDtypeStruct((nl, bsz, n), F32),
        compiler_params=_cparams(2),
        name="mod_proj",
    )(c, w, b.reshape(nl, 1, n))
    return out[:, :n_rows]


def _modnorm(x, gain, shift, scale):
    ms = jnp.mean(x * x, axis=-1, keepdims=True)
    return x * lax.rsqrt(ms + RMS_EPS) * (gain * (1.0 + scale)) + shift


def _nmm_plain_kernel(x_ref, g_ref, sh_ref, sc_ref, w_ref, o_ref, h_ref):
    @pl.when(pl.program_id(2) == 0)
    def _():
        h_ref[...] = _modnorm(x_ref[...], g_ref[...], sh_ref[...], sc_ref[...]).astype(BF16)

    o_ref[...] = _dot(h_ref[...], w_ref[...]).astype(o_ref.dtype)


def _norm_matmul(x, gain, shift, scale, w, out_dtype=F32, tm_pref=(1024, 512, 256, 128)):
    bsz, seq, d = x.shape
    n = w.shape[1]
    tm = _pick(seq, tm_pref)
    tn = _pick(n, (512, 256, 128))
    return pl.pallas_call(
        _nmm_plain_kernel,
        grid=(bsz, seq // tm, n // tn),
        in_specs=[
            pl.BlockSpec((None, tm, d), lambda b, i, j: (b, i, 0)),
            pl.BlockSpec((1, d), lambda b, i, j: (0, 0)),
            pl.BlockSpec((None, 1, d), lambda b, i, j: (b, 0, 0)),
            pl.BlockSpec((None, 1, d), lambda b, i, j: (b, 0, 0)),
            pl.BlockSpec((d, tn), lambda b, i, j: (0, j)),
        ],
        out_specs=pl.BlockSpec((None, tm, tn), lambda b, i, j: (b, i, j)),
        out_shape=jax.ShapeDtypeStruct((bsz, seq, n), out_dtype),
        scratch_shapes=[pltpu.VMEM((tm, d), BF16)],
        compiler_params=_cparams(3),
        name="norm_matmul",
    )(x, gain.reshape(1, d), shift.reshape(bsz, 1, d), scale.reshape(bsz, 1, d), w)


def _nmm_conv_kernel(x_ref, g_ref, sh_ref, sc_ref, w_ref, cw_ref, o_ref, h_ref, halo_ref,
                     *, n_q_tiles, n_qk_tiles):
    i = pl.program_id(1)
    j = pl.program_id(2)
    tm, tn = o_ref.shape

    @pl.when(j == 0)
    def _():
        h_ref[...] = _modnorm(x_ref[...], g_ref[...], sh_ref[...], sc_ref[...]).astype(BF16)

    @pl.when(i == 0)
    def _():
        halo_ref[j] = jnp.zeros((8, tn), F32)

    acc = _dot(h_ref[...], w_ref[...])
    cat = jnp.concatenate([halo_ref[j], acc], axis=0)
    halo_ref[j] = acc[tm - 8:, :]
    cw = cw_ref[...]
    y = cw[3:4, :] * acc
    for back in (1, 2, 3):
        y = y + cw[3 - back:4 - back, :] * pltpu.roll(cat, back, 0)[8:, :]
    y = _silu(y)

    @pl.when(j < n_qk_tiles)
    def _():
        qscale = jnp.where(j < n_q_tiles, HEAD ** -0.5, 1.0).astype(F32)
        for hh in range(tn // HEAD):
            yh = y[:, hh * HEAD:(hh + 1) * HEAD]
            ss = jnp.sum(yh * yh, axis=-1, keepdims=True)
            o_ref[:, hh * HEAD:(hh + 1) * HEAD] = yh * (lax.rsqrt(ss + L2_EPS) * qscale)

    @pl.when(j >= n_qk_tiles)
    def _():
        o_ref[...] = y


def _norm_matmul_conv(x, gain, shift, scale, w, conv_w, qk_width):
    bsz, seq, d = x.shape
    n = w.shape[1]
    tm = _pick(seq, (512, 256, 128))
    tn = _pick(qk_width, (512, 256, 128))
    kern = functools.partial(_nmm_conv_kernel, n_q_tiles=qk_width // tn, n_qk_tiles=2 * qk_width // tn)
    return pl.pallas_call(
        kern,
        grid=(bsz, seq // tm, n // tn),
        in_specs=[
            pl.BlockSpec((None, tm, d), lambda b, i, j: (b, i, 0)),
            pl.BlockSpec((1, d), lambda b, i, j: (0, 0)),
            pl.BlockSpec((None, 1, d), lambda b, i, j: (b, 0, 0)),
            pl.BlockSpec((None, 1, d), lambda b, i, j: (b, 0, 0)),
            pl.BlockSpec((d, tn), lambda b, i, j: (0, j)),
            pl.BlockSpec((CONV_WIDTH, tn), lambda b, i, j: (0, j)),
        ],
        out_specs=pl.BlockSpec((None, tm, tn), lambda b, i, j: (b, i, j)),
        out_shape=jax.ShapeDtypeStruct((bsz, seq, n), F32),
        scratch_shapes=[pltpu.VMEM((tm, d), BF16), pltpu.VMEM((n // tn, 8, tn), F32)],
        compiler_params=_cparams(3),
        name="norm_matmul_conv",
    )(x, gain.reshape(1, d), shift.reshape(bsz, 1, d), scale.reshape(bsz, 1, d), w, conv_w)


def _delta_kernel(q_ref, k_ref, v_ref, z_ref, ba_ref, par_ref, og_ref, o_ref, state_ref,
                  *, heads_per_step, n_vheads):
    n = pl.program_id(2)
    hg = pl.program_id(1)
    g_heads = heads_per_step
    c = CHUNK

    @pl.when(n == 0)
    def _():
        state_ref[...] = jnp.zeros(state_ref.shape, F32)

    row = lax.broadcasted_iota(jnp.int32, (c, c), 0)
    col = lax.broadcasted_iota(jnp.int32, (c, c), 1)
    lower = row >= col
    strict = row > col
    eye = row == col
    lane = lax.broadcasted_iota(jnp.int32, (c, HEAD), 1)

    ba = ba_ref[...]
    beta_full = _sigmoid(ba)
    g_full = -jnp.exp(par_ref[0:1, :]) * _softplus(ba + par_ref[1:2, :])
    g_hi, g_mid, g_lo = _split3(g_full)
    tril = jnp.where(lower, 1.0, 0.0).astype(BF16)
    gc_full = _dot(tril, g_hi) + _dot(tril, g_mid) + _dot(tril, g_lo)
    comb = jnp.where(lane < n_vheads, beta_full, gc_full)

    og = og_ref[...]
    kk_cache = {}
    for h in range(g_heads):
        kh_idx = h // 2
        head = hg * g_heads + h
        kh = k_ref[:, kh_idx * HEAD:(kh_idx + 1) * HEAD]
        qh = q_ref[:, kh_idx * HEAD:(kh_idx + 1) * HEAD]
        vh = v_ref[:, h * HEAD:(h + 1) * HEAD]
        zh = z_ref[:, h * HEAD:(h + 1) * HEAD]
        if kh_idx not in kk_cache:
            kb16 = kh.astype(BF16)
            kk_cache[kh_idx] = (_dot_nt(kb16, kb16), _dot_nt(qh.astype(BF16), kb16))
        kk, qk = kk_cache[kh_idx]

        beta_c = jnp.sum(jnp.where(lane == head, comb, 0.0), axis=-1, keepdims=True)
        gc_c = jnp.sum(jnp.where(lane == head + n_vheads, comb, 0.0), axis=-1, keepdims=True)
        gc_r = jnp.sum(jnp.where(eye, gc_c, 0.0), axis=0, keepdims=True)
        gl = gc_c[c - 1:c, :]
        decay = jnp.exp(jnp.where(lower, gc_c - gc_r, NEG))
        eg = jnp.exp(gc_c)

        lmat = jnp.where(strict, kk * beta_c * decay, 0.0)
        attn = qk * decay
        xm = -lmat
        pw = lmat
        for _ in range(5):
            pw16 = pw.astype(BF16)
            pw = _dot(pw16, pw16)
            xm = xm + pw + _dot(xm.astype(BF16), pw.astype(BF16))
        rhs = jnp.concatenate([vh * beta_c, kh * (beta_c * eg)], axis=1)
        uw = rhs + _dot(xm.astype(BF16), rhs.astype(BF16))
        u = uw[:, :HEAD]
        w = uw[:, HEAD:]

        s_old = state_ref[h]
        wq = jnp.concatenate([w, qh * eg], axis=0)
        wqs = _dot(wq.astype(BF16), s_old.astype(BF16))
        v_new = u - wqs[:c, :]
        v16 = v_new.astype(BF16)
        o = wqs[c:, :] + _dot(attn.astype(BF16), v16)
        kd = kh * jnp.exp(gl - gc_c)
        state_ref[h] = s_old * jnp.exp(gl) + _dot_tn(kd.astype(BF16), v16)

        ms = jnp.mean(o * o, axis=-1, keepdims=True)
        o_ref[:, h * HEAD:(h + 1) * HEAD] = (o * lax.rsqrt(ms + RMS_EPS) * og * _silu(zh)).astype(o_ref.dtype)


def _delta_rule(qkv, z, ba, gate_par, o_gain, qk_width, heads_per_step=4):
    bsz, seq, _ = qkv.shape
    v_width = z.shape[-1]
    n_vheads = v_width // HEAD
    g = min(heads_per_step, n_vheads)
    gk = g // 2
    n_kblocks = qk_width // (gk * HEAD)
    kern = functools.partial(_delta_kernel, heads_per_step=g, n_vheads=n_vheads)
    return pl.pallas_call(
        kern,
        grid=(bsz, n_vheads // g, seq // CHUNK),
        in_specs=[
            pl.BlockSpec((None, CHUNK, gk * HEAD), lambda b, hg, n: (b, n, hg)),
            pl.BlockSpec((None, CHUNK, gk * HEAD), lambda b, hg, n: (b, n, n_kblocks + hg)),
            pl.BlockSpec((None, CHUNK, g * HEAD), lambda b, hg, n: (b, n, 2 * qk_width // (g * HEAD) + hg)),
            pl.BlockSpec((None, CHUNK, g * HEAD), lambda b, hg, n: (b, n, hg)),
            pl.BlockSpec((None, CHUNK, HEAD), lambda b, hg, n: (b, n, 0)),
            pl.BlockSpec((2, HEAD), lambda b, hg, n: (0, 0)),
            pl.BlockSpec((1, HEAD), lambda b, hg, n: (0, 0)),
        ],
        out_specs=pl.BlockSpec((None, CHUNK, g * HEAD), lambda b, hg, n: (b, n, hg)),
        out_shape=jax.ShapeDtypeStruct((bsz, seq, v_width), BF16),
        scratch_shapes=[pltpu.VMEM((g, HEAD, HEAD), F32)],
        compiler_params=_cparams(3),
        name="delta_rule",
    )(qkv, qkv, qkv, z, ba, gate_par, o_gain.reshape(1, HEAD))


def _out_proj_kernel(y_ref, w_ref, x_ref, gate_ref, o_ref):
    o_ref[...] = x_ref[...] + gate_ref[...] * _dot(y_ref[...].astype(BF16), w_ref[...])


def _out_proj(y, w, x, gate):
    bsz, seq, k = y.shape
    d = w.shape[1]
    tm = _pick(seq, (1024, 512, 256, 128))
    tn = _pick(d, (512, 256, 128))
    return pl.pallas_call(
        _out_proj_kernel,
        grid=(bsz, seq // tm, d // tn),
        in_specs=[
            pl.BlockSpec((None, tm, k), lambda b, i, j: (b, i, 0)),
            pl.BlockSpec((k, tn), lambda b, i, j: (0, j)),
            pl.BlockSpec((None, tm, tn), lambda b, i, j: (b, i, j)),
            pl.BlockSpec((None, 1, tn), lambda b, i, j: (b, 0, j)),
        ],
        out_specs=pl.BlockSpec((None, tm, tn), lambda b, i, j: (b, i, j)),
        out_shape=jax.ShapeDtypeStruct((bsz, seq, d), F32),
        compiler_params=_cparams(3),
        name="out_proj",
    )(y, w, x, gate.reshape(bsz, 1, d))


def _attn_kernel(q0_ref, q1_ref, q2_ref, kp_ref, ko_ref, vp_ref, vo_ref, z_ref, bias_ref, o_ref,
                 m_ref, l_ref, acc_ref):
    first_tile = pl.program_id(1) == 0
    head_penalty = jnp.where(first_tile, NEG, 0.0).astype(F32)
    scale = HEAD ** -0.5
    q_refs = (q0_ref, q1_ref, q2_ref)
    for gi, (_, dil) in enumerate(DILATION_GROUPS):
        q_ref = q_refs[gi]
        bias_a = bias_ref[gi, :, :BAND]
        bias_b = bias_ref[gi, :, BAND:]
        blocks_per_res = ATT_TILE // (BAND * dil)
        for r in range(dil):
            for nb in range(blocks_per_res):
                start = r + dil * BAND * nb
                rows = pl.ds(start, BAND, stride=dil) if dil > 1 else pl.ds(start, BAND)
                if nb == 0:
                    pstart = ATT_TILE - dil * BAND + r
                    prow = pl.ds(pstart, BAND, stride=dil) if dil > 1 else pl.ds(pstart, BAND)
                    k_a, v_a = kp_ref[prow, :], vp_ref[prow, :]
                    b_a = bias_a + head_penalty
                else:
                    pstart = start - dil * BAND
                    prow = pl.ds(pstart, BAND, stride=dil) if dil > 1 else pl.ds(pstart, BAND)
                    k_a, v_a = ko_ref[prow, :], vo_ref[prow, :]
                    b_a = bias_a
                k_b, v_b = ko_ref[rows, :], vo_ref[rows, :]
                q16 = q_ref[rows, :].astype(BF16)
                s_a = _dot_nt(q16, k_a.astype(BF16)) * scale + b_a
                s_b = _dot_nt(q16, k_b.astype(BF16)) * scale + bias_b
                m_blk = jnp.maximum(jnp.max(s_a, axis=-1, keepdims=True),
                                    jnp.max(s_b, axis=-1, keepdims=True))
                if gi == 0:
                    m_new = m_blk
                else:
                    m_old = m_ref[rows, :]
                    m_new = jnp.maximum(m_old, m_blk)
                p_a = jnp.exp(s_a - m_new)
                p_b = jnp.exp(s_b - m_new)
                l_blk = jnp.sum(p_a, axis=-1, keepdims=True) + jnp.sum(p_b, axis=-1, keepdims=True)
                pv = _dot(p_a.astype(BF16), v_a.astype(BF16)) + _dot(p_b.astype(BF16), v_b.astype(BF16))
                if gi == 0:
                    l_new = jnp.broadcast_to(l_blk, (BAND, HEAD))
                    acc_new = pv
                else:
                    alpha = jnp.exp(m_old - m_new)
                    l_new = alpha * l_ref[rows, :] + l_blk
                    acc_new = alpha * acc_ref[rows, :] + pv
                if gi == len(DILATION_GROUPS) - 1:
                    o_ref[rows, :] = acc_new / l_new * _silu(z_ref[rows, :])
                else:
                    m_ref[rows, :] = jnp.broadcast_to(m_new, (BAND, HEAD))
                    l_ref[rows, :] = l_new
                    acc_ref[rows, :] = acc_new


def _dilated_attention(qz, kv, bias):
    bsz, seq, _ = qz.shape
    width = kv.shape[-1] // 2
    heads = width // HEAD
    t = ATT_TILE
    row_tile = lambda b, i, h: (b, i, h)

    def q_spec(gi):
        return pl.BlockSpec((None, t, HEAD), lambda b, i, h: (b, i, gi * heads + h))

    return pl.pallas_call(
        _attn_kernel,
        grid=(bsz, seq // t, heads),
        in_specs=[
            q_spec(0), q_spec(1), q_spec(2),
            pl.BlockSpec((None, t, HEAD), lambda b, i, h: (b, jnp.maximum(i - 1, 0), h)),
            pl.BlockSpec((None, t, HEAD), row_tile),
            pl.BlockSpec((None, t, HEAD), lambda b, i, h: (b, jnp.maximum(i - 1, 0), heads + h)),
            pl.BlockSpec((None, t, HEAD), lambda b, i, h: (b, i, heads + h)),
            pl.BlockSpec((None, t, HEAD), lambda b, i, h: (b, i, 3 * heads + h)),
            pl.BlockSpec((3, None, BAND, 2 * BAND), lambda b, i, h: (0, h, 0, 0)),
        ],
        out_specs=pl.BlockSpec((None, t, HEAD), row_tile),
        out_shape=jax.ShapeDtypeStruct((bsz, seq, width), F32),
        scratch_shapes=[pltpu.VMEM((t, HEAD), F32)] * 3,
        compiler_params=_cparams(3),
        name="dilated_attention",
    )(qz, qz, qz, kv, kv, kv, kv, qz, bias)


def _t5_bucket(dist):
    max_exact = N_BUCKETS // 2
    n = jnp.maximum(dist, 0)
    large = max_exact + (jnp.log(jnp.maximum(n, 1).astype(F32) / max_exact)
                         / math.log(MAX_DISTANCE / max_exact)
                         * (N_BUCKETS - max_exact)).astype(jnp.int32)
    large = jnp.minimum(large, N_BUCKETS - 1)
    return jnp.where(n < max_exact, n, large)


def _band_bias(rel_bias, heads):
    qi = jnp.arange(BAND)[:, None]
    kj = jnp.arange(2 * BAND)[None, :]
    rel = qi + BAND - kj
    valid = (rel >= 0) & (rel <= BAND)
    out = []
    for gi, (_, dil) in enumerate(DILATION_GROUPS):
        table = rel_bias[:, gi * heads:(gi + 1) * heads].astype(F32)
        b = jnp.transpose(table[_t5_bucket(rel * dil)], (2, 0, 1))
        out.append(jnp.where(valid[None], b, NEG))
    return jnp.stack(out, axis=0)


def _rmsnorm_kernel(x_ref, g_ref, o_ref):
    x = x_ref[...]
    ms = jnp.mean(x * x, axis=-1, keepdims=True)
    o_ref[...] = x * lax.rsqrt(ms + RMS_EPS) * g_ref[...]


def _rmsnorm(x, gain):
    bsz, seq, d = x.shape
    tm = _pick(seq, (1024, 512, 256, 128))
    return pl.pallas_call(
        _rmsnorm_kernel,
        grid=(bsz, seq // tm),
        in_specs=[pl.BlockSpec((None, tm, d), lambda b, i: (b, i, 0)),
                  pl.BlockSpec((1, d), lambda b, i: (0, 0))],
        out_specs=pl.BlockSpec((None, tm, d), lambda b, i: (b, i, 0)),
        out_shape=jax.ShapeDtypeStruct((bsz, seq, d), F32),
        compiler_params=_cparams(2),
        name="final_rmsnorm",
    )(x, gain.reshape(1, d))


def kernel(x, c, norm_gain, w_mod, b_mod, w_in_a, conv_w_a, a_log, dt_bias, o_norm_a, w_out_a,
           kv_gain, w_kv_mod, b_kv_mod, w_kv, w_in_b, w_out_b, rel_bias, final_gain):
    bsz, seq, d = x.shape
    depth = norm_gain.shape[0]
    n_a = w_in_a.shape[0]
    qk_width = d
    v_width = w_out_a.shape[1]
    n_vheads = v_width // HEAD
    conv_ch = 2 * qk_width + v_width
    dil_width = w_out_b.shape[1]
    dil_heads = dil_width // HEAD
    assert seq % ATT_TILE == 0 and seq % CHUNK == 0 and 2 * n_vheads <= HEAD

    mods = _mod_proj(c, w_mod, b_mod)
    kv_mod = _mod_proj(c, w_kv_mod[None], b_kv_mod[None])[0]
    bias = _band_bias(rel_bias, dil_heads)

    kv = None
    for layer in range(depth):
        shift, scale, gate = mods[layer, :, :d], mods[layer, :, d:2 * d], mods[layer, :, 2 * d:]
        if layer < n_a:
            w_in = w_in_a[layer].astype(BF16)
            qkv = _norm_matmul_conv(x, norm_gain[layer], shift, scale, w_in[:, :conv_ch],
                                    conv_w_a[layer], qk_width)
            z = _norm_matmul(x, norm_gain[layer], shift, scale, w_in[:, conv_ch:conv_ch + v_width])
            w_ba = jnp.pad(w_in[:, conv_ch + v_width:], ((0, 0), (0, HEAD - 2 * n_vheads)))
            ba = _norm_matmul(x, norm_gain[layer], shift, scale, w_ba)
            gate_par = jnp.zeros((2, HEAD), F32)
            gate_par = gate_par.at[0, n_vheads:2 * n_vheads].set(a_log[layer])
            gate_par = gate_par.at[1, n_vheads:2 * n_vheads].set(dt_bias[layer])
            y = _delta_rule(qkv, z, ba, gate_par, o_norm_a[layer], qk_width)
            x = _out_proj(y, w_out_a[layer].astype(BF16), x, gate)
        else:
            if kv is None:
                kv = _norm_matmul(x, kv_gain, kv_mod[:, :d], kv_mod[:, d:], w_kv.astype(BF16))
            j = layer - n_a
            qz = _norm_matmul(x, norm_gain[layer], shift, scale, w_in_b[j].astype(BF16))
            y = _dilated_attention(qz, kv, bias)
            x = _out_proj(y, w_out_b[j].astype(BF16), x, gate)
    return _rmsnorm(x, final_gain)
```

```python
import functools
import math

import jax
import jax.numpy as jnp
from jax import lax
from jax.experimental import pallas as pl
from jax.experimental.pallas import tpu as pltpu

F32 = jnp.float32
BF16 = jnp.bfloat16

RMS_EPS = 1e-6
L2_EPS = 1e-6
HEAD = 128
BF16_SUBLANES = 16
CHUNK = 64
CONV_WIDTH = 4
DILATION_GROUPS = ((128, 1), (512, 4), (2048, 16))
BAND = 128
ATT_TILE = 2048
N_BUCKETS = 32
MAX_DISTANCE = 2048
NEG = -1e30
VMEM_LIMIT = 56 * 1024 * 1024


def _cparams(n_axes):
    return pltpu.CompilerParams(
        dimension_semantics=("arbitrary",) * n_axes, vmem_limit_bytes=VMEM_LIMIT)


def _dot(a, b):
    return jnp.dot(a, b, preferred_element_type=F32)


def _dot_nt(a, b):
    return lax.dot_general(a, b, (((1,), (1,)), ((), ())), preferred_element_type=F32)


def _dot_tn(a, b):
    return lax.dot_general(a, b, (((0,), (0,)), ((), ())), preferred_element_type=F32)


def _sigmoid(x):
    return 1.0 / (1.0 + jnp.exp(-x))


def _silu(x):
    return x * _sigmoid(x)


def _softplus(x):
    return jnp.maximum(x, 0.0) + jnp.log(1.0 + jnp.exp(-jnp.abs(x)))


def _split3(x):
    hi = x.astype(BF16)
    r1 = x - hi.astype(F32)
    mid = r1.astype(BF16)
    lo = (r1 - mid.astype(F32)).astype(BF16)
    return hi, mid, lo


def _pick(n, prefs):
    for p in prefs:
        if n % p == 0:
            return p
    return n


def _mod_kernel(c_ref, w_ref, b_ref, o_ref):
    ca = _silu(c_ref[...])
    w = w_ref[...]
    c_hi, c_mid, c_lo = _split3(ca)
    w_hi = w.astype(BF16)
    w_lo = (w - w_hi.astype(F32)).astype(BF16)
    acc = _dot(c_hi, w_hi) + _dot(c_hi, w_lo) + _dot(c_mid, w_hi) + _dot(c_lo, w_hi)
    o_ref[...] = acc + b_ref[...]


def _mod_proj(c, w, b):
    nl, d, n = w.shape
    n_rows = c.shape[0]
    bsz = -(-n_rows // BF16_SUBLANES) * BF16_SUBLANES
    c = jnp.pad(c, ((0, bsz - n_rows), (0, 0)))
    tn = _pick(n, (512, 256, 128))
    out = pl.pallas_call(
        _mod_kernel,
        grid=(nl, n // tn),
        in_specs=[
            pl.BlockSpec((bsz, d), lambda l, j: (0, 0)),
            pl.BlockSpec((None, d, tn), lambda l, j: (l, 0, j)),
            pl.BlockSpec((None, 1, tn), lambda l, j: (l, 0, j)),
        ],
        out_specs=pl.BlockSpec((None, bsz, tn), lambda l, j: (l, 0, j)),
        out_shape=jax.ShapeDtypeStruct((nl, bsz, n), F32),
        compiler_params=_cparams(2),
        name="mod_proj",
    )(c, w, b.reshape(nl, 1, n))
    return out[:, :n_rows]


def _modnorm(x, gain, shift, scale):
    ms = jnp.mean(x * x, axis=-1, keepdims=True)
    return x * lax.rsqrt(ms + RMS_EPS) * (gain * (1.0 + scale)) + shift


def _nmm_plain_kernel(x_ref, g_ref, sh_ref, sc_ref, w_ref, o_ref, h_ref):
    @pl.when(pl.program_id(2) == 0)
    def _():
        h_ref[...] = _modnorm(x_ref[...], g_ref[...], sh_ref[...], sc_ref[...]).astype(BF16)

    o_ref[...] = _dot(h_ref[...], w_ref[...]).astype(o_ref.dtype)


def _norm_matmul(x, gain, shift, scale, w, out_dtype=F32, tm_pref=(1024, 512, 256, 128)):
    bsz, seq, d = x.shape
    n = w.shape[1]
    tm = _pick(seq, tm_pref)
    tn = _pick(n, (512, 256, 128))
    return pl.pallas_call(
        _nmm_plain_kernel,
        grid=(bsz, seq // tm, n // tn),
        in_specs=[
            pl.BlockSpec((None, tm, d), lambda b, i, j: (b, i, 0)),
            pl.BlockSpec((1, d), lambda b, i, j: (0, 0)),
            pl.BlockSpec((None, 1, d), lambda b, i, j: (b, 0, 0)),
            pl.BlockSpec((None, 1, d), lambda b, i, j: (b, 0, 0)),
            pl.BlockSpec((d, tn), lambda b, i, j: (0, j)),
        ],
        out_specs=pl.BlockSpec((None, tm, tn), lambda b, i, j: (b, i, j)),
        out_shape=jax.ShapeDtypeStruct((bsz, seq, n), out_dtype),
        scratch_shapes=[pltpu.VMEM((tm, d), BF16)],
        compiler_params=_cparams(3),
        name="norm_matmul",
    )(x, gain.reshape(1, d), shift.reshape(bsz, 1, d), scale.reshape(bsz, 1, d), w)


def _nmm_conv_kernel(x_ref, g_ref, sh_ref, sc_ref, w_ref, cw_ref, o_ref, h_ref, halo_ref,
                     *, n_q_tiles, n_qk_tiles):
    i = pl.program_id(1)
    j = pl.program_id(2)
    tm, tn = o_ref.shape

    @pl.when(j == 0)
    def _():
        h_ref[...] = _modnorm(x_ref[...], g_ref[...], sh_ref[...], sc_ref[...]).astype(BF16)

    @pl.when(i == 0)
    def _():
        halo_ref[j] = jnp.zeros((8, tn), F32)

    acc = _dot(h_ref[...], w_ref[...])
    cat = jnp.concatenate([halo_ref[j], acc], axis=0)
    halo_ref[j] = acc[tm - 8:, :]
    cw = cw_ref[...]
    y = cw[3:4, :] * acc
    for back in (1, 2, 3):
        y = y + cw[3 - back:4 - back, :] * pltpu.roll(cat, back, 0)[8:, :]
    y = _silu(y)

    @pl.when(j < n_qk_tiles)
    def _():
        qscale = jnp.where(j < n_q_tiles, HEAD ** -0.5, 1.0).astype(F32)
        for hh in range(tn // HEAD):
            yh = y[:, hh * HEAD:(hh + 1) * HEAD]
            ss = jnp.sum(yh * yh, axis=-1, keepdims=True)
            o_ref[:, hh * HEAD:(hh + 1) * HEAD] = yh * (lax.rsqrt(ss + L2_EPS) * qscale)

    @pl.when(j >= n_qk_tiles)
    def _():
        o_ref[...] = y


def _norm_matmul_conv(x, gain, shift, scale, w, conv_w, qk_width):
    bsz, seq, d = x.shape
    n = w.shape[1]
    tm = _pick(seq, (512, 256, 128))
    tn = _pick(qk_width, (512, 256, 128))
    kern = functools.partial(_nmm_conv_kernel, n_q_tiles=qk_width // tn, n_qk_tiles=2 * qk_width // tn)
    return pl.pallas_call(
        kern,
        grid=(bsz, seq // tm, n // tn),
        in_specs=[
            pl.BlockSpec((None, tm, d), lambda b, i, j: (b, i, 0)),
            pl.BlockSpec((1, d), lambda b, i, j: (0, 0)),
            pl.BlockSpec((None, 1, d), lambda b, i, j: (b, 0, 0)),
            pl.BlockSpec((None, 1, d), lambda b, i, j: (b, 0, 0)),
            pl.BlockSpec((d, tn), lambda b, i, j: (0, j)),
            pl.BlockSpec((CONV_WIDTH, tn), lambda b, i, j: (0, j)),
        ],
        out_specs=pl.BlockSpec((None, tm, tn), lambda b, i, j: (b, i, j)),
        out_shape=jax.ShapeDtypeStruct((bsz, seq, n), F32),
        scratch_shapes=[pltpu.VMEM((tm, d), BF16), pltpu.VMEM((n // tn, 8, tn), F32)],
        compiler_params=_cparams(3),
        name="norm_matmul_conv",
    )(x, gain.reshape(1, d), shift.reshape(bsz, 1, d), scale.reshape(bsz, 1, d), w, conv_w)


def _delta_kernel(q_ref, k_ref, v_ref, z_ref, ba_ref, par_ref, og_ref, o_ref,
                  state_ref, u_ref, w_ref, qd_ref, kd_ref, at_ref, a_ref,
                  *, heads_per_step, chunks_per_step, n_vheads):
    s = pl.program_id(2)
    hg = pl.program_id(1)
    g_heads = heads_per_step
    c = CHUNK
    slot = s % 2
    prev = 1 - slot

    @pl.when(s == 0)
    def _():
        state_ref[...] = jnp.zeros(state_ref.shape, F32)
        u_ref[1] = jnp.zeros(u_ref.shape[1:], F32)
        w_ref[1] = jnp.zeros(w_ref.shape[1:], BF16)
        qd_ref[1] = jnp.zeros(qd_ref.shape[1:], BF16)
        kd_ref[1] = jnp.zeros(kd_ref.shape[1:], BF16)
        at_ref[1] = jnp.zeros(at_ref.shape[1:], BF16)
        a_ref[1] = jnp.zeros(a_ref.shape[1:], F32)

    row = lax.broadcasted_iota(jnp.int32, (c, c), 0)
    col = lax.broadcasted_iota(jnp.int32, (c, c), 1)
    lower = row >= col
    strict = row > col
    eye = row == col
    lane = lax.broadcasted_iota(jnp.int32, (c, HEAD), 1)
    tril = jnp.where(lower, 1.0, 0.0).astype(BF16)

    items = [(cb, h) for cb in range(chunks_per_step) for h in range(g_heads)]
    rows_of = lambda cb: slice(cb * c, (cb + 1) * c)
    cols_of = lambda i: slice(i * HEAD, (i + 1) * HEAD)

    comb, kk, qk = {}, {}, {}
    lmat, xm, pw, rhs, fac = {}, {}, {}, {}, {}

    def gates_stage():
        for cb in range(chunks_per_step):
            ba = ba_ref[rows_of(cb), :]
            beta_full = _sigmoid(ba)
            g_full = -jnp.exp(par_ref[0:1, :]) * _softplus(ba + par_ref[1:2, :])
            g_hi, g_mid, g_lo = _split3(g_full)
            gc_full = _dot(tril, g_hi) + _dot(tril, g_mid) + _dot(tril, g_lo)
            comb[cb] = jnp.where(lane < n_vheads, beta_full, gc_full)

    def gram_stage():
        for cb in range(chunks_per_step):
            for kh_idx in range(g_heads // 2):
                k16 = k_ref[rows_of(cb), cols_of(kh_idx)].astype(BF16)
                q16 = q_ref[rows_of(cb), cols_of(kh_idx)].astype(BF16)
                kk[cb, kh_idx] = _dot_nt(k16, k16)
                qk[cb, kh_idx] = _dot_nt(q16, k16)

    def decay_stage():
        for cb, h in items:
            head = hg * g_heads + h
            kh = k_ref[rows_of(cb), cols_of(h // 2)]
            qh = q_ref[rows_of(cb), cols_of(h // 2)]
            vh = v_ref[rows_of(cb), cols_of(h)]
            beta_c = jnp.sum(jnp.where(lane == head, comb[cb], 0.0), axis=-1, keepdims=True)
            gc_c = jnp.sum(jnp.where(lane == head + n_vheads, comb[cb], 0.0), axis=-1, keepdims=True)
            gc_r = jnp.sum(jnp.where(eye, gc_c, 0.0), axis=0, keepdims=True)
            gl = gc_c[c - 1:c, :]
            decay = jnp.exp(jnp.where(lower, gc_c - gc_r, NEG))
            eg = jnp.exp(gc_c)
            lmat[cb, h] = jnp.where(strict, kk[cb, h // 2] * beta_c * decay, 0.0)
            rhs[cb, h] = jnp.concatenate([vh * beta_c, kh * (beta_c * eg)], axis=1)
            qd_ref[slot, cb, h] = (qh * eg).astype(BF16)
            kd_ref[slot, cb, h] = (kh * jnp.exp(gl - gc_c)).astype(BF16)
            at_ref[slot, cb, h] = (qk[cb, h // 2] * decay).astype(BF16)
            a_ref[slot, cb, h] = jnp.broadcast_to(jnp.exp(gl), (1, HEAD))
            xm[cb, h] = -lmat[cb, h]
            pw[cb, h] = lmat[cb, h].astype(BF16)

    def square_stage():
        for it in items:
            fac[it] = _dot(pw[it], pw[it])
            pw[it] = fac[it].astype(BF16)

    def product_stage():
        for it in items:
            xm[it] = xm[it] + fac[it] + _dot(xm[it].astype(BF16), pw[it])

    def solve_stage():
        for cb, h in items:
            uw = rhs[cb, h] + _dot(xm[cb, h].astype(BF16), rhs[cb, h].astype(BF16))
            u_ref[slot, cb, h] = uw[:, :HEAD]
            w_ref[slot, cb, h] = uw[:, HEAD:].astype(BF16)

    factor_stages = [gates_stage, gram_stage, decay_stage]
    for _ in range(5):
        factor_stages += [square_stage, product_stage]
    factor_stages.append(solve_stage)

    og = og_ref[...]
    states = [state_ref[h] for h in range(g_heads)]
    wqs, v16 = {}, {}

    def read_stage(cb):
        def run():
            for h in range(g_heads):
                wq = jnp.concatenate([w_ref[prev, cb, h], qd_ref[prev, cb, h]], axis=0)
                wqs[h] = _dot(wq, states[h].astype(BF16))
        return run

    def update_stage(cb):
        def run():
            for h in range(g_heads):
                v16[h] = (u_ref[prev, cb, h] - wqs[h][:c, :]).astype(BF16)
                o = wqs[h][c:, :] + _dot(at_ref[prev, cb, h], v16[h])
                states[h] = states[h] * a_ref[prev, cb, h] + _dot_tn(kd_ref[prev, cb, h], v16[h])
                ms = jnp.mean(o * o, axis=-1, keepdims=True)
                zh = z_ref[rows_of(cb), cols_of(h)]
                o_ref[rows_of(cb), cols_of(h)] = (o * lax.rsqrt(ms + RMS_EPS) * og * _silu(zh)).astype(o_ref.dtype)
        return run

    state_stages = []
    for cb in range(chunks_per_step):
        state_stages += [read_stage(cb), update_stage(cb)]

    stride = max(1, len(factor_stages) // (len(state_stages) + 1))
    for idx, stage in enumerate(factor_stages):
        stage()
        if idx % stride == stride - 1 and state_stages:
            state_stages.pop(0)()
    for stage in state_stages:
        stage()
    for h in range(g_heads):
        state_ref[h] = states[h]


def _delta_rule(qkv, z, ba, gate_par, o_gain, qk_width, heads_per_step=8, chunks_per_step=2):
    bsz, seq, _ = qkv.shape
    v_width = z.shape[-1]
    n_vheads = v_width // HEAD
    g = min(heads_per_step, n_vheads)
    gk = g // 2
    cbs = chunks_per_step
    rows = cbs * CHUNK
    n_blocks = seq // rows
    n_kblocks = qk_width // (gk * HEAD)
    kern = functools.partial(_delta_kernel, heads_per_step=g, chunks_per_step=cbs, n_vheads=n_vheads)
    cur = lambda s: jnp.minimum(s, n_blocks - 1)
    old = lambda s: jnp.maximum(s - 1, 0)
    return pl.pallas_call(
        kern,
        grid=(bsz, n_vheads // g, n_blocks + 1),
        in_specs=[
            pl.BlockSpec((None, rows, gk * HEAD), lambda b, hg, s: (b, cur(s), hg)),
            pl.BlockSpec((None, rows, gk * HEAD), lambda b, hg, s: (b, cur(s), n_kblocks + hg)),
            pl.BlockSpec((None, rows, g * HEAD), lambda b, hg, s: (b, cur(s), 2 * qk_width // (g * HEAD) + hg)),
            pl.BlockSpec((None, rows, g * HEAD), lambda b, hg, s: (b, old(s), hg)),
            pl.BlockSpec((None, rows, HEAD), lambda b, hg, s: (b, cur(s), 0)),
            pl.BlockSpec((2, HEAD), lambda b, hg, s: (0, 0)),
            pl.BlockSpec((1, HEAD), lambda b, hg, s: (0, 0)),
        ],
        out_specs=pl.BlockSpec((None, rows, g * HEAD), lambda b, hg, s: (b, old(s), hg)),
        out_shape=jax.ShapeDtypeStruct((bsz, seq, v_width), BF16),
        scratch_shapes=[
            pltpu.VMEM((g, HEAD, HEAD), F32),
            pltpu.VMEM((2, cbs, g, CHUNK, HEAD), F32),
            pltpu.VMEM((2, cbs, g, CHUNK, HEAD), BF16),
            pltpu.VMEM((2, cbs, g, CHUNK, HEAD), BF16),
            pltpu.VMEM((2, cbs, g, CHUNK, HEAD), BF16),
            pltpu.VMEM((2, cbs, g, CHUNK, CHUNK), BF16),
            pltpu.VMEM((2, cbs, g, 1, HEAD), F32),
        ],
        compiler_params=_cparams(3),
        name="delta_rule",
    )(qkv, qkv, qkv, z, ba, gate_par, o_gain.reshape(1, HEAD))


def _out_proj_kernel(y_ref, w_ref, x_ref, gate_ref, o_ref):
    o_ref[...] = x_ref[...] + gate_ref[...] * _dot(y_ref[...].astype(BF16), w_ref[...])


def _out_proj(y, w, x, gate):
    bsz, seq, k = y.shape
    d = w.shape[1]
    tm = _pick(seq, (1024, 512, 256, 128))
    tn = _pick(d, (512, 256, 128))
    return pl.pallas_call(
        _out_proj_kernel,
        grid=(bsz, seq // tm, d // tn),
        in_specs=[
            pl.BlockSpec((None, tm, k), lambda b, i, j: (b, i, 0)),
            pl.BlockSpec((k, tn), lambda b, i, j: (0, j)),
            pl.BlockSpec((None, tm, tn), lambda b, i, j: (b, i, j)),
            pl.BlockSpec((None, 1, tn), lambda b, i, j: (b, 0, j)),
        ],
        out_specs=pl.BlockSpec((None, tm, tn), lambda b, i, j: (b, i, j)),
        out_shape=jax.ShapeDtypeStruct((bsz, seq, d), F32),
        compiler_params=_cparams(3),
        name="out_proj",
    )(y, w, x, gate.reshape(bsz, 1, d))


def _attn_kernel(q0_ref, q1_ref, q2_ref, kp_ref, ko_ref, vp_ref, vo_ref, z_ref, bias_ref, o_ref,
                 m_ref, l_ref, acc_ref):
    first_tile = pl.program_id(1) == 0
    head_penalty = jnp.where(first_tile, NEG, 0.0).astype(F32)
    scale = HEAD ** -0.5
    q_refs = (q0_ref, q1_ref, q2_ref)
    for gi, (_, dil) in enumerate(DILATION_GROUPS):
        q_ref = q_refs[gi]
        bias_a = bias_ref[gi, :, :BAND]
        bias_b = bias_ref[gi, :, BAND:]
        blocks_per_res = ATT_TILE // (BAND * dil)
        for r in range(dil):
            for nb in range(blocks_per_res):
                start = r + dil * BAND * nb
                rows = pl.ds(start, BAND, stride=dil) if dil > 1 else pl.ds(start, BAND)
                if nb == 0:
                    pstart = ATT_TILE - dil * BAND + r
                    prow = pl.ds(pstart, BAND, stride=dil) if dil > 1 else pl.ds(pstart, BAND)
                    k_a, v_a = kp_ref[prow, :], vp_ref[prow, :]
                    b_a = bias_a + head_penalty
                else:
                    pstart = start - dil * BAND
                    prow = pl.ds(pstart, BAND, stride=dil) if dil > 1 else pl.ds(pstart, BAND)
                    k_a, v_a = ko_ref[prow, :], vo_ref[prow, :]
                    b_a = bias_a
                k_b, v_b = ko_ref[rows, :], vo_ref[rows, :]
                q16 = q_ref[rows, :].astype(BF16)
                s_a = _dot_nt(q16, k_a.astype(BF16)) * scale + b_a
                s_b = _dot_nt(q16, k_b.astype(BF16)) * scale + bias_b
                m_blk = jnp.maximum(jnp.max(s_a, axis=-1, keepdims=True),
                                    jnp.max(s_b, axis=-1, keepdims=True))
                if gi == 0:
                    m_new = m_blk
                else:
                    m_old = m_ref[rows, :]
                    m_new = jnp.maximum(m_old, m_blk)
                p_a = jnp.exp(s_a - m_new)
                p_b = jnp.exp(s_b - m_new)
                l_blk = jnp.sum(p_a, axis=-1, keepdims=True) + jnp.sum(p_b, axis=-1, keepdims=True)
                pv = _dot(p_a.astype(BF16), v_a.astype(BF16)) + _dot(p_b.astype(BF16), v_b.astype(BF16))
                if gi == 0:
                    l_new = jnp.broadcast_to(l_blk, (BAND, HEAD))
                    acc_new = pv
                else:
                    alpha = jnp.exp(m_old - m_new)
                    l_new = alpha * l_ref[rows, :] + l_blk
                    acc_new = alpha * acc_ref[rows, :] + pv
                if gi == len(DILATION_GROUPS) - 1:
                    o_ref[rows, :] = acc_new / l_new * _silu(z_ref[rows, :])
                else:
                    m_ref[rows, :] = jnp.broadcast_to(m_new, (BAND, HEAD))
                    l_ref[rows, :] = l_new
                    acc_ref[rows, :] = acc_new


def _dilated_attention(qz, kv, bias):
    bsz, seq, _ = qz.shape
    width = kv.shape[-1] // 2
    heads = width // HEAD
    t = ATT_TILE
    row_tile = lambda b, i, h: (b, i, h)

    def q_spec(gi):
        return pl.BlockSpec((None, t, HEAD), lambda b, i, h: (b, i, gi * heads + h))

    return pl.pallas_call(
        _attn_kernel,
        grid=(bsz, seq // t, heads),
        in_specs=[
            q_spec(0), q_spec(1), q_spec(2),
            pl.BlockSpec((None, t, HEAD), lambda b, i, h: (b, jnp.maximum(i - 1, 0), h)),
            pl.BlockSpec((None, t, HEAD), row_tile),
            pl.BlockSpec((None, t, HEAD), lambda b, i, h: (b, jnp.maximum(i - 1, 0), heads + h)),
            pl.BlockSpec((None, t, HEAD), lambda b, i, h: (b, i, heads + h)),
            pl.BlockSpec((None, t, HEAD), lambda b, i, h: (b, i, 3 * heads + h)),
            pl.BlockSpec((3, None, BAND, 2 * BAND), lambda b, i, h: (0, h, 0, 0)),
        ],
        out_specs=pl.BlockSpec((None, t, HEAD), row_tile),
        out_shape=jax.ShapeDtypeStruct((bsz, seq, width), F32),
        scratch_shapes=[pltpu.VMEM((t, HEAD), F32)] * 3,
        compiler_params=_cparams(3),
        name="dilated_attention",
    )(qz, qz, qz, kv, kv, kv, kv, qz, bias)


def _t5_bucket(dist):
    max_exact = N_BUCKETS // 2
    n = jnp.maximum(dist, 0)
    large = max_exact + (jnp.log(jnp.maximum(n, 1).astype(F32) / max_exact)
                         / math.log(MAX_DISTANCE / max_exact)
                         * (N_BUCKETS - max_exact)).astype(jnp.int32)
    large = jnp.minimum(large, N_BUCKETS - 1)
    return jnp.where(n < max_exact, n, large)


def _band_bias(rel_bias, heads):
    qi = jnp.arange(BAND)[:, None]
    kj = jnp.arange(2 * BAND)[None, :]
    rel = qi + BAND - kj
    valid = (rel >= 0) & (rel <= BAND)
    out = []
    for gi, (_, dil) in enumerate(DILATION_GROUPS):
        table = rel_bias[:, gi * heads:(gi + 1) * heads].astype(F32)
        b = jnp.transpose(table[_t5_bucket(rel * dil)], (2, 0, 1))
        out.append(jnp.where(valid[None], b, NEG))
    return jnp.stack(out, axis=0)


def _rmsnorm_kernel(x_ref, g_ref, o_ref):
    x = x_ref[...]
    ms = jnp.mean(x * x, axis=-1, keepdims=True)
    o_ref[...] = x * lax.rsqrt(ms + RMS_EPS) * g_ref[...]


def _rmsnorm(x, gain):
    bsz, seq, d = x.shape
    tm = _pick(seq, (1024, 512, 256, 128))
    return pl.pallas_call(
        _rmsnorm_kernel,
        grid=(bsz, seq // tm),
        in_specs=[pl.BlockSpec((None, tm, d), lambda b, i: (b, i, 0)),
                  pl.BlockSpec((1, d), lambda b, i: (0, 0))],
        out_specs=pl.BlockSpec((None, tm, d), lambda b, i: (b, i, 0)),
        out_shape=jax.ShapeDtypeStruct((bsz, seq, d), F32),
        compiler_params=_cparams(2),
        name="final_rmsnorm",
    )(x, gain.reshape(1, d))


def kernel(x, c, norm_gain, w_mod, b_mod, w_in_a, conv_w_a, a_log, dt_bias, o_norm_a, w_out_a,
           kv_gain, w_kv_mod, b_kv_mod, w_kv, w_in_b, w_out_b, rel_bias, final_gain):
    bsz, seq, d = x.shape
    depth = norm_gain.shape[0]
    n_a = w_in_a.shape[0]
    qk_width = d
    v_width = w_out_a.shape[1]
    n_vheads = v_width // HEAD
    conv_ch = 2 * qk_width + v_width
    dil_width = w_out_b.shape[1]
    dil_heads = dil_width // HEAD
    assert seq % ATT_TILE == 0 and seq % CHUNK == 0 and 2 * n_vheads <= HEAD

    mods = _mod_proj(c, w_mod, b_mod)
    kv_mod = _mod_proj(c, w_kv_mod[None], b_kv_mod[None])[0]
    bias = _band_bias(rel_bias, dil_heads)

    kv = None
    for layer in range(depth):
        shift, scale, gate = mods[layer, :, :d], mods[layer, :, d:2 * d], mods[layer, :, 2 * d:]
        if layer < n_a:
            w_in = w_in_a[layer].astype(BF16)
            qkv = _norm_matmul_conv(x, norm_gain[layer], shift, scale, w_in[:, :conv_ch],
                                    conv_w_a[layer], qk_width)
            z = _norm_matmul(x, norm_gain[layer], shift, scale, w_in[:, conv_ch:conv_ch + v_width])
            w_ba = jnp.pad(w_in[:, conv_ch + v_width:], ((0, 0), (0, HEAD - 2 * n_vheads)))
            ba = _norm_matmul(x, norm_gain[layer], shift, scale, w_ba)
            gate_par = jnp.zeros((2, HEAD), F32)
            gate_par = gate_par.at[0, n_vheads:2 * n_vheads].set(a_log[layer])
            gate_par = gate_par.at[1, n_vheads:2 * n_vheads].set(dt_bias[layer])
            y = _delta_rule(qkv, z, ba, gate_par, o_norm_a[layer], qk_width)
            x = _out_proj(y, w_out_a[layer].astype(BF16), x, gate)
        else:
            if kv is None:
                kv = _norm_matmul(x, kv_gain, kv_mod[:, :d], kv_mod[:, d:], w_kv.astype(BF16))
            j = layer - n_a
            qz = _norm_matmul(x, norm_gain[layer], shift, scale, w_in_b[j].astype(BF16))
            y = _dilated_attention(qz, kv, bias)
            x = _out_proj(y, w_out_b[j].astype(BF16), x, gate)
    return _rmsnorm(x, final_gain)
```

```python
import functools
import math

import jax
import jax.numpy as jnp
from jax import lax
from jax.experimental import pallas as pl
from jax.experimental.pallas import tpu as pltpu

F32 = jnp.float32
BF16 = jnp.bfloat16

RMS_EPS = 1e-6
L2_EPS = 1e-6
HEAD = 128
BF16_SUBLANES = 16
CHUNK = 64
CONV_WIDTH = 4
HALO = 8
DILATION_GROUPS = ((128, 1), (512, 4), (2048, 16))
BAND = 128
ATT_TILE = 2048
PLANES = 16
PLANE_ROWS = ATT_TILE // PLANES
PLANES_PER_STEP = 4
ATT_BATCH = 8
N_BUCKETS = 32
MAX_DISTANCE = 2048
NEG = -1e30
VMEM_LIMIT = 56 * 1024 * 1024


def _cparams(n_axes):
    return pltpu.CompilerParams(
        dimension_semantics=("arbitrary",) * n_axes, vmem_limit_bytes=VMEM_LIMIT)


def _dot(a, b):
    return jnp.dot(a, b, preferred_element_type=F32)


def _dot_nt(a, b):
    return lax.dot_general(a, b, (((1,), (1,)), ((), ())), preferred_element_type=F32)


def _dot_tn(a, b):
    return lax.dot_general(a, b, (((0,), (0,)), ((), ())), preferred_element_type=F32)


def _sigmoid(x):
    return 1.0 / (1.0 + jnp.exp(-x))


def _silu(x):
    return x * _sigmoid(x)


def _softplus(x):
    return jnp.maximum(x, 0.0) + jnp.log(1.0 + jnp.exp(-jnp.abs(x)))


def _split3(x):
    hi = x.astype(BF16)
    r1 = x - hi.astype(F32)
    mid = r1.astype(BF16)
    lo = (r1 - mid.astype(F32)).astype(BF16)
    return hi, mid, lo


def _pick(n, prefs):
    for p in prefs:
        if n % p == 0:
            return p
    return n


def _mod_kernel(c_ref, w_ref, b_ref, o_ref):
    ca = _silu(c_ref[...])
    w = w_ref[...]
    c_hi, c_mid, c_lo = _split3(ca)
    w_hi = w.astype(BF16)
    w_lo = (w - w_hi.astype(F32)).astype(BF16)
    acc = _dot(c_hi, w_hi) + _dot(c_hi, w_lo) + _dot(c_mid, w_hi) + _dot(c_lo, w_hi)
    o_ref[...] = acc + b_ref[...]


def _mod_proj(c, w, b):
    nl, d, n = w.shape
    n_rows = c.shape[0]
    bsz = -(-n_rows // BF16_SUBLANES) * BF16_SUBLANES
    c = jnp.pad(c, ((0, bsz - n_rows), (0, 0)))
    tn = _pick(n, (512, 256, 128))
    out = pl.pallas_call(
        _mod_kernel,
        grid=(nl, n // tn),
        in_specs=[
            pl.BlockSpec((bsz, d), lambda l, j: (0, 0)),
            pl.BlockSpec((None, d, tn), lambda l, j: (l, 0, j)),
            pl.BlockSpec((None, 1, tn), lambda l, j: (l, 0, j)),
        ],
        out_specs=pl.BlockSpec((None, bsz, tn), lambda l, j: (l, 0, j)),
        out_shape=jax.ShapeDtypeStruct((nl, bsz, n), F32),
        compiler_params=_cparams(2),
        name="mod_proj",
    )(c, w, b.reshape(nl, 1, n))
    return out[:, :n_rows]


def _modnorm(x, gain, shift, scale):
    ms = jnp.mean(x * x, axis=-1, keepdims=True)
    return x * lax.rsqrt(ms + RMS_EPS) * (gain * (1.0 + scale)) + shift


def _nmm_plain_kernel(x_ref, g_ref, sh_ref, sc_ref, w_ref, o_ref, h_ref):
    @pl.when(pl.program_id(2) == 0)
    def _():
        h_ref[...] = _modnorm(x_ref[...], g_ref[...], sh_ref[...], sc_ref[...]).astype(BF16)

    o_ref[...] = _dot(h_ref[...], w_ref[...]).astype(o_ref.dtype)


def _norm_matmul(x, gain, shift, scale, w, out_dtype=F32, tm_pref=(1024, 512, 256, 128)):
    bsz, seq, d = x.shape
    n = w.shape[1]
    tm = _pick(seq, tm_pref)
    tn = _pick(n, (512, 256, 128))
    return pl.pallas_call(
        _nmm_plain_kernel,
        grid=(bsz, seq // tm, n // tn),
        in_specs=[
            pl.BlockSpec((None, tm, d), lambda b, i, j: (b, i, 0)),
            pl.BlockSpec((1, d), lambda b, i, j: (0, 0)),
            pl.BlockSpec((None, 1, d), lambda b, i, j: (b, 0, 0)),
            pl.BlockSpec((None, 1, d), lambda b, i, j: (b, 0, 0)),
            pl.BlockSpec((d, tn), lambda b, i, j: (0, j)),
        ],
        out_specs=pl.BlockSpec((None, tm, tn), lambda b, i, j: (b, i, j)),
        out_shape=jax.ShapeDtypeStruct((bsz, seq, n), out_dtype),
        scratch_shapes=[pltpu.VMEM((tm, d), BF16)],
        compiler_params=_cparams(3),
        name="norm_matmul",
    )(x, gain.reshape(1, d), shift.reshape(bsz, 1, d), scale.reshape(bsz, 1, d), w)


def _nmm_conv_kernel(x_ref, g_ref, sh_ref, sc_ref, w_ref, cw_ref, o_ref, h_ref, halo_ref, acc_ref,
                     *, n_q_tiles, n_qk_tiles):
    i = pl.program_id(1)
    j = pl.program_id(2)
    tm, tn = o_ref.shape
    jt = jnp.maximum(j - 1, 0)

    @pl.when(j == 0)
    def _():
        h_ref[...] = _modnorm(x_ref[...], g_ref[...], sh_ref[...], sc_ref[...]).astype(BF16)
        acc_ref[1] = jnp.zeros(acc_ref.shape[1:], F32)

    @pl.when((i == 0) & (j == 0))
    def _():
        halo_ref[...] = jnp.zeros(halo_ref.shape, F32)

    eslot = (j + 1) % 2
    prev_rows = halo_ref[jt]
    acc_ref[eslot, 0:HALO, :] = prev_rows
    acc = acc_ref[eslot, HALO:, :]
    halo_ref[jt] = jnp.where(j > 0, acc[tm - HALO:, :], prev_rows)
    cw = cw_ref[...]
    y = cw[3:4, :] * acc
    for back in (1, 2, 3):
        y = y + cw[3 - back:4 - back, :] * acc_ref[eslot, pl.ds(HALO - back, tm), :]
    y = _silu(y)
    is_qk = jt < n_qk_tiles
    qscale = jnp.where(jt < n_q_tiles, HEAD ** -0.5, 1.0).astype(F32)
    for hh in range(tn // HEAD):
        yh = y[:, hh * HEAD:(hh + 1) * HEAD]
        ss = jnp.sum(yh * yh, axis=-1, keepdims=True)
        o_ref[:, hh * HEAD:(hh + 1) * HEAD] = yh * jnp.where(is_qk, lax.rsqrt(ss + L2_EPS) * qscale, 1.0)

    acc_ref[j % 2, HALO:, :] = _dot(h_ref[...], w_ref[...])


def _norm_matmul_conv(x, gain, shift, scale, w, conv_w, qk_width):
    bsz, seq, d = x.shape
    n = w.shape[1]
    tm = _pick(seq, (1024, 512, 256, 128))
    tn = _pick(qk_width, (512, 256, 128))
    n_tiles = n // tn
    kern = functools.partial(_nmm_conv_kernel, n_q_tiles=qk_width // tn, n_qk_tiles=2 * qk_width // tn)
    cur = lambda j: jnp.minimum(j, n_tiles - 1)
    old = lambda j: jnp.maximum(j - 1, 0)
    return pl.pallas_call(
        kern,
        grid=(bsz, seq // tm, n_tiles + 1),
        in_specs=[
            pl.BlockSpec((None, tm, d), lambda b, i, j: (b, i, 0)),
            pl.BlockSpec((1, d), lambda b, i, j: (0, 0)),
            pl.BlockSpec((None, 1, d), lambda b, i, j: (b, 0, 0)),
            pl.BlockSpec((None, 1, d), lambda b, i, j: (b, 0, 0)),
            pl.BlockSpec((d, tn), lambda b, i, j: (0, cur(j))),
            pl.BlockSpec((CONV_WIDTH, tn), lambda b, i, j: (0, old(j))),
        ],
        out_specs=pl.BlockSpec((None, tm, tn), lambda b, i, j: (b, i, old(j))),
        out_shape=jax.ShapeDtypeStruct((bsz, seq, n), F32),
        scratch_shapes=[pltpu.VMEM((tm, d), BF16), pltpu.VMEM((n_tiles, HALO, tn), F32),
                        pltpu.VMEM((2, HALO + tm, tn), F32)],
        compiler_params=_cparams(3),
        name="norm_matmul_conv",
    )(x, gain.reshape(1, d), shift.reshape(bsz, 1, d), scale.reshape(bsz, 1, d), w, conv_w)


def _delta_kernel(q_ref, k_ref, v_ref, z_ref, ba_ref, par_ref, og_ref, o_ref,
                  state_ref, u_ref, w_ref, qd_ref, kd_ref, at_ref, a_ref,
                  *, heads_per_step, chunks_per_step, n_vheads):
    s = pl.program_id(2)
    hg = pl.program_id(1)
    g_heads = heads_per_step
    c = CHUNK
    slot = s % 2
    prev = 1 - slot

    @pl.when(s == 0)
    def _():
        state_ref[...] = jnp.zeros(state_ref.shape, F32)
        u_ref[1] = jnp.zeros(u_ref.shape[1:], F32)
        w_ref[1] = jnp.zeros(w_ref.shape[1:], BF16)
        qd_ref[1] = jnp.zeros(qd_ref.shape[1:], BF16)
        kd_ref[1] = jnp.zeros(kd_ref.shape[1:], BF16)
        at_ref[1] = jnp.zeros(at_ref.shape[1:], BF16)
        a_ref[1] = jnp.zeros(a_ref.shape[1:], F32)

    row = lax.broadcasted_iota(jnp.int32, (c, c), 0)
    col = lax.broadcasted_iota(jnp.int32, (c, c), 1)
    lower = row >= col
    strict = row > col
    eye = row == col
    lane = lax.broadcasted_iota(jnp.int32, (c, HEAD), 1)
    tril = jnp.where(lower, 1.0, 0.0).astype(BF16)

    items = [(cb, h) for cb in range(chunks_per_step) for h in range(g_heads)]
    rows_of = lambda cb: slice(cb * c, (cb + 1) * c)
    cols_of = lambda i: slice(i * HEAD, (i + 1) * HEAD)

    comb, kk, qk = {}, {}, {}
    lmat, xm, pw, rhs, fac = {}, {}, {}, {}, {}

    def gates_stage():
        for cb in range(chunks_per_step):
            ba = ba_ref[rows_of(cb), :]
            beta_full = _sigmoid(ba)
            g_full = -jnp.exp(par_ref[0:1, :]) * _softplus(ba + par_ref[1:2, :])
            g_hi, g_mid, g_lo = _split3(g_full)
            gc_full = _dot(tril, g_hi) + _dot(tril, g_mid) + _dot(tril, g_lo)
            comb[cb] = jnp.where(lane < n_vheads, beta_full, gc_full)

    def gram_stage():
        for cb in range(chunks_per_step):
            for kh_idx in range(g_heads // 2):
                k16 = k_ref[rows_of(cb), cols_of(kh_idx)].astype(BF16)
                q16 = q_ref[rows_of(cb), cols_of(kh_idx)].astype(BF16)
                kk[cb, kh_idx] = _dot_nt(k16, k16)
                qk[cb, kh_idx] = _dot_nt(q16, k16)

    def decay_stage():
        for cb, h in items:
            head = hg * g_heads + h
            kh = k_ref[rows_of(cb), cols_of(h // 2)]
            qh = q_ref[rows_of(cb), cols_of(h // 2)]
            vh = v_ref[rows_of(cb), cols_of(h)]
            beta_c = jnp.sum(jnp.where(lane == head, comb[cb], 0.0), axis=-1, keepdims=True)
            gc_c = jnp.sum(jnp.where(lane == head + n_vheads, comb[cb], 0.0), axis=-1, keepdims=True)
            gc_r = jnp.sum(jnp.where(eye, gc_c, 0.0), axis=0, keepdims=True)
            gl = gc_c[c - 1:c, :]
            decay = jnp.exp(jnp.where(lower, gc_c - gc_r, NEG))
            eg = jnp.exp(gc_c)
            lmat[cb, h] = jnp.where(strict, kk[cb, h // 2] * beta_c * decay, 0.0)
            rhs[cb, h] = jnp.concatenate([vh * beta_c, kh * (beta_c * eg)], axis=1)
            qd_ref[slot, cb, h] = (qh * eg).astype(BF16)
            kd_ref[slot, cb, h] = (kh * jnp.exp(gl - gc_c)).astype(BF16)
            at_ref[slot, cb, h] = (qk[cb, h // 2] * decay).astype(BF16)
            a_ref[slot, cb, h] = jnp.broadcast_to(jnp.exp(gl), (1, HEAD))
            xm[cb, h] = -lmat[cb, h]
            pw[cb, h] = lmat[cb, h].astype(BF16)

    def square_stage():
        for it in items:
            fac[it] = _dot(pw[it], pw[it])
            pw[it] = fac[it].astype(BF16)

    def product_stage():
        for it in items:
            xm[it] = xm[it] + fac[it] + _dot(xm[it].astype(BF16), pw[it])

    def solve_stage():
        for cb, h in items:
            uw = rhs[cb, h] + _dot(xm[cb, h].astype(BF16), rhs[cb, h].astype(BF16))
            u_ref[slot, cb, h] = uw[:, :HEAD]
            w_ref[slot, cb, h] = uw[:, HEAD:].astype(BF16)

    factor_stages = [gates_stage, gram_stage, decay_stage]
    for _ in range(5):
        factor_stages += [square_stage, product_stage]
    factor_stages.append(solve_stage)

    og = og_ref[...]
    states = [state_ref[h] for h in range(g_heads)]
    wqs, v16 = {}, {}

    def read_stage(cb):
        def run():
            for h in range(g_heads):
                wq = jnp.concatenate([w_ref[prev, cb, h], qd_ref[prev, cb, h]], axis=0)
                wqs[h] = _dot(wq, states[h].astype(BF16))
        return run

    def update_stage(cb):
        def run():
            for h in range(g_heads):
                v16[h] = (u_ref[prev, cb, h] - wqs[h][:c, :]).astype(BF16)
                o = wqs[h][c:, :] + _dot(at_ref[prev, cb, h], v16[h])
                states[h] = states[h] * a_ref[prev, cb, h] + _dot_tn(kd_ref[prev, cb, h], v16[h])
                ms = jnp.mean(o * o, axis=-1, keepdims=True)
                zh = z_ref[rows_of(cb), cols_of(h)]
                o_ref[rows_of(cb), cols_of(h)] = (o * lax.rsqrt(ms + RMS_EPS) * og * _silu(zh)).astype(o_ref.dtype)
        return run

    state_stages = []
    for cb in range(chunks_per_step):
        state_stages += [read_stage(cb), update_stage(cb)]

    stride = max(1, len(factor_stages) // (len(state_stages) + 1))
    for idx, stage in enumerate(factor_stages):
        stage()
        if idx % stride == stride - 1 and state_stages:
            state_stages.pop(0)()
    for stage in state_stages:
        stage()
    for h in range(g_heads):
        state_ref[h] = states[h]


def _delta_rule(qkv, z, ba, gate_par, o_gain, qk_width, heads_per_step=8, chunks_per_step=2):
    bsz, seq, _ = qkv.shape
    v_width = z.shape[-1]
    n_vheads = v_width // HEAD
    g = min(heads_per_step, n_vheads)
    gk = g // 2
    cbs = chunks_per_step
    rows = cbs * CHUNK
    n_blocks = seq // rows
    n_kblocks = qk_width // (gk * HEAD)
    kern = functools.partial(_delta_kernel, heads_per_step=g, chunks_per_step=cbs, n_vheads=n_vheads)
    cur = lambda s: jnp.minimum(s, n_blocks - 1)
    old = lambda s: jnp.maximum(s - 1, 0)
    return pl.pallas_call(
        kern,
        grid=(bsz, n_vheads // g, n_blocks + 1),
        in_specs=[
            pl.BlockSpec((None, rows, gk * HEAD), lambda b, hg, s: (b, cur(s), hg)),
            pl.BlockSpec((None, rows, gk * HEAD), lambda b, hg, s: (b, cur(s), n_kblocks + hg)),
            pl.BlockSpec((None, rows, g * HEAD), lambda b, hg, s: (b, cur(s), 2 * qk_width // (g * HEAD) + hg)),
            pl.BlockSpec((None, rows, g * HEAD), lambda b, hg, s: (b, old(s), hg)),
            pl.BlockSpec((None, rows, HEAD), lambda b, hg, s: (b, cur(s), 0)),
            pl.BlockSpec((2, HEAD), lambda b, hg, s: (0, 0)),
            pl.BlockSpec((1, HEAD), lambda b, hg, s: (0, 0)),
        ],
        out_specs=pl.BlockSpec((None, rows, g * HEAD), lambda b, hg, s: (b, old(s), hg)),
        out_shape=jax.ShapeDtypeStruct((bsz, seq, v_width), BF16),
        scratch_shapes=[
            pltpu.VMEM((g, HEAD, HEAD), F32),
            pltpu.VMEM((2, cbs, g, CHUNK, HEAD), F32),
            pltpu.VMEM((2, cbs, g, CHUNK, HEAD), BF16),
            pltpu.VMEM((2, cbs, g, CHUNK, HEAD), BF16),
            pltpu.VMEM((2, cbs, g, CHUNK, HEAD), BF16),
            pltpu.VMEM((2, cbs, g, CHUNK, CHUNK), BF16),
            pltpu.VMEM((2, cbs, g, 1, HEAD), F32),
        ],
        compiler_params=_cparams(3),
        name="delta_rule",
    )(qkv, qkv, qkv, z, ba, gate_par, o_gain.reshape(1, HEAD))


def _out_proj_kernel(y_ref, w_ref, x_ref, gate_ref, o_ref):
    o_ref[...] = x_ref[...] + gate_ref[...] * _dot(y_ref[...].astype(BF16), w_ref[...])


def _out_proj(y, w, x, gate):
    bsz, seq, k = y.shape
    d = w.shape[1]
    tm = _pick(seq, (1024, 512, 256, 128))
    tn = _pick(d, (512, 256, 128))
    return pl.pallas_call(
        _out_proj_kernel,
        grid=(bsz, seq // tm, d // tn),
        in_specs=[
            pl.BlockSpec((None, tm, k), lambda b, i, j: (b, i, 0)),
            pl.BlockSpec((k, tn), lambda b, i, j: (0, j)),
            pl.BlockSpec((None, tm, tn), lambda b, i, j: (b, i, j)),
            pl.BlockSpec((None, 1, tn), lambda b, i, j: (b, 0, j)),
        ],
        out_specs=pl.BlockSpec((None, tm, tn), lambda b, i, j: (b, i, j)),
        out_shape=jax.ShapeDtypeStruct((bsz, seq, d), F32),
        compiler_params=_cparams(3),
        name="out_proj",
    )(y, w, x, gate.reshape(bsz, 1, d))


def _block_slabs(dil, res, nb):
    blocks_per_res = ATT_TILE // (BAND * dil)
    slab = PLANE_ROWS // blocks_per_res
    return [((res + dil * a) * PLANE_ROWS + slab * nb, slab) for a in range(PLANES // dil)]


def _block_positions(dil):
    n_slabs = PLANES // dil
    slab = BAND // n_slabs
    return [n_slabs * i + a for a in range(n_slabs) for i in range(slab)]


def _attn_blocks():
    blocks = []
    for gi, (_, dil) in enumerate(DILATION_GROUPS):
        blocks_per_res = ATT_TILE // (BAND * dil)
        for res in range(dil):
            for nb in range(blocks_per_res):
                prev_nb = (nb - 1) % blocks_per_res
                blocks.append((gi, _block_slabs(dil, res, nb), _block_slabs(dil, res, prev_nb), nb == 0))
    return blocks


def _gather_rows(ref, slabs):
    parts = [ref[start:start + size, :] for start, size in slabs]
    return parts[0] if len(parts) == 1 else jnp.concatenate(parts, axis=0)


def _scatter_rows(ref, slabs, val):
    off = 0
    for start, size in slabs:
        ref[start:start + size, :] = val[off:off + size, :].astype(ref.dtype)
        off += size


def _attn_kernel(q0_ref, q1_ref, q2_ref, kp_ref, ko_ref, vp_ref, vo_ref, z_ref, bias_ref, o_ref,
                 m_ref, l_ref, acc_ref):
    first_tile = pl.program_id(1) == 0
    col = lax.broadcasted_iota(jnp.int32, (1, 2 * BAND), 1)
    head_penalty = jnp.where((col < BAND) & first_tile, NEG, 0.0)
    scale = HEAD ** -0.5
    q_refs = (q0_ref, q1_ref, q2_ref)
    last_group = len(DILATION_GROUPS) - 1

    def score_stage(blk):
        gi, q_slabs, prev_slabs, prev_tile = blk
        q16 = _gather_rows(q_refs[gi], q_slabs).astype(BF16)
        k_prev = _gather_rows(kp_ref if prev_tile else ko_ref, prev_slabs)
        k16 = jnp.concatenate([k_prev, _gather_rows(ko_ref, q_slabs)], axis=0).astype(BF16)
        return _dot_nt(q16, k16)

    def softmax_stage(blk, s):
        gi, _, _, prev_tile = blk
        bias = bias_ref[gi]
        if prev_tile:
            bias = bias + head_penalty
        s = s * scale + bias
        m = jnp.max(s, axis=-1, keepdims=True)
        p = jnp.exp(s - m)
        return m, jnp.sum(p, axis=-1, keepdims=True), p.astype(BF16)

    def value_stage(blk, p16):
        _, q_slabs, prev_slabs, prev_tile = blk
        v_prev = _gather_rows(vp_ref if prev_tile else vo_ref, prev_slabs)
        v16 = jnp.concatenate([v_prev, _gather_rows(vo_ref, q_slabs)], axis=0).astype(BF16)
        return _dot(p16, v16)

    def merge_stage(blk, m, l, pv):
        gi, q_slabs, _, _ = blk
        if gi == 0:
            m_new = jnp.broadcast_to(m, (BAND, HEAD))
            l_new = jnp.broadcast_to(l, (BAND, HEAD))
            acc_new = pv
        else:
            m_old = _gather_rows(m_ref, q_slabs)
            m_new = jnp.maximum(m_old, m)
            w_old = jnp.exp(m_old - m_new)
            w_blk = jnp.exp(m - m_new)
            l_new = w_old * _gather_rows(l_ref, q_slabs) + w_blk * l
            acc_new = w_old * _gather_rows(acc_ref, q_slabs) + w_blk * pv
        if gi == last_group:
            _scatter_rows(o_ref, q_slabs, acc_new / l_new * _silu(_gather_rows(z_ref, q_slabs)))
        else:
            _scatter_rows(m_ref, q_slabs, m_new)
            _scatter_rows(l_ref, q_slabs, l_new)
            _scatter_rows(acc_ref, q_slabs, acc_new)

    blocks = _attn_blocks()
    batches = [blocks[i:i + ATT_BATCH] for i in range(0, len(blocks), ATT_BATCH)]
    scores_next = [score_stage(blk) for blk in batches[0]]
    for bi, batch in enumerate(batches):
        scores = scores_next
        if bi + 1 < len(batches):
            scores_next = [score_stage(blk) for blk in batches[bi + 1]]
        stats = [softmax_stage(blk, s) for blk, s in zip(batch, scores)]
        pvs = [value_stage(blk, p16) for blk, (_, _, p16) in zip(batch, stats)]
        for blk, (m, l, _), pv in zip(batch, stats, pvs):
            merge_stage(blk, m, l, pv)


def _dilated_attention(qz, kv, bias):
    bsz, seq, _ = qz.shape
    width = kv.shape[-1] // 2
    heads = width // HEAD
    t = ATT_TILE
    row_tile = lambda b, i, h: (b, i, h)

    def q_spec(gi):
        return pl.BlockSpec((None, t, HEAD), lambda b, i, h: (b, i, gi * heads + h))

    return pl.pallas_call(
        _attn_kernel,
        grid=(bsz, seq // t, heads),
        in_specs=[
            q_spec(0), q_spec(1), q_spec(2),
            pl.BlockSpec((None, t, HEAD), lambda b, i, h: (b, jnp.maximum(i - 1, 0), h)),
            pl.BlockSpec((None, t, HEAD), row_tile),
            pl.BlockSpec((None, t, HEAD), lambda b, i, h: (b, jnp.maximum(i - 1, 0), heads + h)),
            pl.BlockSpec((None, t, HEAD), lambda b, i, h: (b, i, heads + h)),
            pl.BlockSpec((None, t, HEAD), lambda b, i, h: (b, i, 3 * heads + h)),
            pl.BlockSpec((3, None, BAND, 2 * BAND), lambda b, i, h: (0, h, 0, 0)),
        ],
        out_specs=pl.BlockSpec((None, t, HEAD), row_tile),
        out_shape=jax.ShapeDtypeStruct((bsz, seq, width), BF16),
        scratch_shapes=[pltpu.VMEM((t, HEAD), F32)] * 3,
        compiler_params=_cparams(3),
        name="dilated_attention",
    )(qz, qz, qz, kv, kv, kv, kv, qz, bias)


def _t5_bucket(dist):
    max_exact = N_BUCKETS // 2
    n = jnp.maximum(dist, 0)
    large = max_exact + (jnp.log(jnp.maximum(n, 1).astype(F32) / max_exact)
                         / math.log(MAX_DISTANCE / max_exact)
                         * (N_BUCKETS - max_exact)).astype(jnp.int32)
    large = jnp.minimum(large, N_BUCKETS - 1)
    return jnp.where(n < max_exact, n, large)


def _band_bias(rel_bias, heads):
    out = []
    for gi, (_, dil) in enumerate(DILATION_GROUPS):
        pos = jnp.asarray(_block_positions(dil), jnp.int32)
        key_pos = jnp.concatenate([pos, pos + BAND])
        rel = pos[:, None] + BAND - key_pos[None, :]
        valid = (rel >= 0) & (rel <= BAND)
        table = rel_bias[:, gi * heads:(gi + 1) * heads].astype(F32)
        b = jnp.transpose(table[_t5_bucket(rel * dil)], (2, 0, 1))
        out.append(jnp.where(valid[None], b, NEG))
    return jnp.stack(out, axis=0)


def _permute_kernel(x_ref, o_ref):
    d = o_ref.shape[-1]
    for k in range(PLANES_PER_STEP):
        o_ref[k * PLANE_ROWS:(k + 1) * PLANE_ROWS, :] = x_ref[:, k * d:(k + 1) * d]


def _permute_rows(x):
    bsz, seq, d = x.shape
    n_tiles = seq // ATT_TILE
    steps = PLANES // PLANES_PER_STEP
    return pl.pallas_call(
        _permute_kernel,
        grid=(bsz, n_tiles, steps),
        in_specs=[pl.BlockSpec((None, PLANE_ROWS, PLANES_PER_STEP * d), lambda b, t, r: (b, t, r))],
        out_specs=pl.BlockSpec((None, PLANES_PER_STEP * PLANE_ROWS, d), lambda b, t, r: (b, t * steps + r, 0)),
        out_shape=jax.ShapeDtypeStruct((bsz, seq, d), x.dtype),
        compiler_params=_cparams(3),
        name="permute_rows",
    )(x.reshape(bsz, seq // PLANES, PLANES * d))


def _rmsnorm_unpermute_kernel(x_ref, g_ref, o_ref):
    d = x_ref.shape[-1]
    for k in range(PLANES_PER_STEP):
        x = x_ref[k * PLANE_ROWS:(k + 1) * PLANE_ROWS, :]
        ms = jnp.mean(x * x, axis=-1, keepdims=True)
        o_ref[:, k * d:(k + 1) * d] = x * lax.rsqrt(ms + RMS_EPS) * g_ref[...]


def _rmsnorm_unpermute(x, gain):
    bsz, seq, d = x.shape
    n_tiles = seq // ATT_TILE
    steps = PLANES // PLANES_PER_STEP
    out = pl.pallas_call(
        _rmsnorm_unpermute_kernel,
        grid=(bsz, n_tiles, steps),
        in_specs=[pl.BlockSpec((None, PLANES_PER_STEP * PLANE_ROWS, d), lambda b, t, r: (b, t * steps + r, 0)),
                  pl.BlockSpec((1, d), lambda b, t, r: (0, 0))],
        out_specs=pl.BlockSpec((None, PLANE_ROWS, PLANES_PER_STEP * d), lambda b, t, r: (b, t, r)),
        out_shape=jax.ShapeDtypeStruct((bsz, seq // PLANES, PLANES * d), F32),
        compiler_params=_cparams(3),
        name="final_rmsnorm",
    )(x, gain.reshape(1, d))
    return out.reshape(bsz, seq, d)


def kernel(x, c, norm_gain, w_mod, b_mod, w_in_a, conv_w_a, a_log, dt_bias, o_norm_a, w_out_a,
           kv_gain, w_kv_mod, b_kv_mod, w_kv, w_in_b, w_out_b, rel_bias, final_gain):
    bsz, seq, d = x.shape
    depth = norm_gain.shape[0]
    n_a = w_in_a.shape[0]
    qk_width = d
    v_width = w_out_a.shape[1]
    n_vheads = v_width // HEAD
    conv_ch = 2 * qk_width + v_width
    dil_width = w_out_b.shape[1]
    dil_heads = dil_width // HEAD
    assert seq % ATT_TILE == 0 and seq % CHUNK == 0 and 2 * n_vheads <= HEAD

    mods = _mod_proj(c, w_mod, b_mod)
    kv_mod = _mod_proj(c, w_kv_mod[None], b_kv_mod[None])[0]
    bias = _band_bias(rel_bias, dil_heads)

    kv = None
    for layer in range(depth):
        shift, scale, gate = mods[layer, :, :d], mods[layer, :, d:2 * d], mods[layer, :, 2 * d:]
        if layer < n_a:
            w_in = w_in_a[layer].astype(BF16)
            qkv = _norm_matmul_conv(x, norm_gain[layer], shift, scale, w_in[:, :conv_ch],
                                    conv_w_a[layer], qk_width)
            z = _norm_matmul(x, norm_gain[layer], shift, scale, w_in[:, conv_ch:conv_ch + v_width])
            w_ba = jnp.pad(w_in[:, conv_ch + v_width:], ((0, 0), (0, HEAD - 2 * n_vheads)))
            ba = _norm_matmul(x, norm_gain[layer], shift, scale, w_ba)
            gate_par = jnp.zeros((2, HEAD), F32)
            gate_par = gate_par.at[0, n_vheads:2 * n_vheads].set(a_log[layer])
            gate_par = gate_par.at[1, n_vheads:2 * n_vheads].set(dt_bias[layer])
            y = _delta_rule(qkv, z, ba, gate_par, o_norm_a[layer], qk_width)
            x = _out_proj(y, w_out_a[layer].astype(BF16), x, gate)
        else:
            if kv is None:
                x = _permute_rows(x)
                kv = _norm_matmul(x, kv_gain, kv_mod[:, :d], kv_mod[:, d:], w_kv.astype(BF16))
            j = layer - n_a
            qz = _norm_matmul(x, norm_gain[layer], shift, scale, w_in_b[j].astype(BF16))
            y = _dilated_attention(qz, kv, bias)
            x = _out_proj(y, w_out_b[j].astype(BF16), x, gate)
    assert kv is not None
    return _rmsnorm_unpermute(x, final_gain)
```

```python
import functools
import math

import jax
import jax.numpy as jnp
from jax import lax
from jax.experimental import pallas as pl
from jax.experimental.pallas import tpu as pltpu

F32 = jnp.float32
BF16 = jnp.bfloat16

RMS_EPS = 1e-6
L2_EPS = 1e-6
HEAD = 128
BF16_SUBLANES = 16
CHUNK = 64
DELTA_PACK = 4
CONV_WIDTH = 4
HALO = 8
DILATION_GROUPS = ((128, 1), (512, 4), (2048, 16))
BAND = 128
ATT_TILE = 2048
PLANES = 16
PLANE_ROWS = ATT_TILE // PLANES
PLANES_PER_STEP = 4
ATT_BATCH = 8
N_BUCKETS = 32
MAX_DISTANCE = 2048
NEG = -1e30
VMEM_LIMIT = 56 * 1024 * 1024


def _cparams(n_axes):
    return pltpu.CompilerParams(
        dimension_semantics=("arbitrary",) * n_axes, vmem_limit_bytes=VMEM_LIMIT)


def _dot(a, b):
    return jnp.dot(a, b, preferred_element_type=F32)


def _dot_nt(a, b):
    return lax.dot_general(a, b, (((1,), (1,)), ((), ())), preferred_element_type=F32)


def _dot_tn(a, b):
    return lax.dot_general(a, b, (((0,), (0,)), ((), ())), preferred_element_type=F32)


def _sigmoid(x):
    return 1.0 / (1.0 + jnp.exp(-x))


def _silu(x):
    return x * _sigmoid(x)


def _softplus(x):
    return jnp.maximum(x, 0.0) + jnp.log(1.0 + jnp.exp(-jnp.abs(x)))


def _split3(x):
    hi = x.astype(BF16)
    r1 = x - hi.astype(F32)
    mid = r1.astype(BF16)
    lo = (r1 - mid.astype(F32)).astype(BF16)
    return hi, mid, lo


def _pick(n, prefs):
    for p in prefs:
        if n % p == 0:
            return p
    return n


def _mod_kernel(c_ref, w_ref, b_ref, o_ref):
    ca = _silu(c_ref[...])
    w = w_ref[...]
    c_hi, c_mid, c_lo = _split3(ca)
    w_hi = w.astype(BF16)
    w_lo = (w - w_hi.astype(F32)).astype(BF16)
    acc = _dot(c_hi, w_hi) + _dot(c_hi, w_lo) + _dot(c_mid, w_hi) + _dot(c_lo, w_hi)
    o_ref[...] = acc + b_ref[...]


def _mod_proj(c, w, b):
    nl, d, n = w.shape
    n_rows = c.shape[0]
    bsz = -(-n_rows // BF16_SUBLANES) * BF16_SUBLANES
    c = jnp.pad(c, ((0, bsz - n_rows), (0, 0)))
    tn = _pick(n, (512, 256, 128))
    out = pl.pallas_call(
        _mod_kernel,
        grid=(nl, n // tn),
        in_specs=[
            pl.BlockSpec((bsz, d), lambda l, j: (0, 0)),
            pl.BlockSpec((None, d, tn), lambda l, j: (l, 0, j)),
            pl.BlockSpec((None, 1, tn), lambda l, j: (l, 0, j)),
        ],
        out_specs=pl.BlockSpec((None, bsz, tn), lambda l, j: (l, 0, j)),
        out_shape=jax.ShapeDtypeStruct((nl, bsz, n), F32),
        compiler_params=_cparams(2),
        name="mod_proj",
    )(c, w, b.reshape(nl, 1, n))
    return out[:, :n_rows]


def _modnorm(x, gain, shift, scale):
    ms = jnp.mean(x * x, axis=-1, keepdims=True)
    return x * lax.rsqrt(ms + RMS_EPS) * (gain * (1.0 + scale)) + shift


def _nmm_plain_kernel(x_ref, g_ref, sh_ref, sc_ref, w_ref, o_ref, h_ref):
    @pl.when(pl.program_id(2) == 0)
    def _():
        h_ref[...] = _modnorm(x_ref[...], g_ref[...], sh_ref[...], sc_ref[...]).astype(BF16)

    o_ref[...] = _dot(h_ref[...], w_ref[...]).astype(o_ref.dtype)


def _norm_matmul(x, gain, shift, scale, w, col_start=0, n_cols=None, out_dtype=F32):
    bsz, seq, d = x.shape
    n = w.shape[1] - col_start if n_cols is None else n_cols
    tm = _pick(seq, (1024, 512, 256, 128))
    tn = _pick(math.gcd(n, col_start) if col_start else n, (512, 256, 128))
    first = col_start // tn
    return pl.pallas_call(
        _nmm_plain_kernel,
        grid=(bsz, seq // tm, n // tn),
        in_specs=[
            pl.BlockSpec((None, tm, d), lambda b, i, j: (b, i, 0)),
            pl.BlockSpec((1, d), lambda b, i, j: (0, 0)),
            pl.BlockSpec((None, 1, d), lambda b, i, j: (b, 0, 0)),
            pl.BlockSpec((None, 1, d), lambda b, i, j: (b, 0, 0)),
            pl.BlockSpec((d, tn), lambda b, i, j: (0, first + j)),
        ],
        out_specs=pl.BlockSpec((None, tm, tn), lambda b, i, j: (b, i, j)),
        out_shape=jax.ShapeDtypeStruct((bsz, seq, n), out_dtype),
        scratch_shapes=[pltpu.VMEM((tm, d), BF16)],
        compiler_params=_cparams(3),
        name="norm_matmul",
    )(x, gain.reshape(1, d), shift.reshape(bsz, 1, d), scale.reshape(bsz, 1, d), w)


def _nmm_conv_kernel(x_ref, g_ref, sh_ref, sc_ref, w_ref, cw_ref, o_ref, h_ref, halo_ref, acc0_ref, acc1_ref,
                     *, n_q_tiles, n_qk_tiles):
    i = pl.program_id(1)
    j = pl.program_id(2)
    tm, tn = o_ref.shape
    jt = jnp.maximum(j - 1, 0)

    @pl.when(j == 0)
    def _():
        h_ref[...] = _modnorm(x_ref[...], g_ref[...], sh_ref[...], sc_ref[...]).astype(BF16)
        acc1_ref[...] = jnp.zeros(acc1_ref.shape, F32)

    @pl.when((i == 0) & (j == 0))
    def _():
        halo_ref[...] = jnp.zeros(halo_ref.shape, F32)

    def step(done_ref, next_ref):
        prev_rows = halo_ref[jt]
        done_ref[0:HALO, :] = prev_rows
        acc = done_ref[HALO:, :]
        halo_ref[jt] = jnp.where(j > 0, acc[tm - HALO:, :], prev_rows)
        cw = cw_ref[...]
        y = cw[3:4, :] * acc
        for back in (1, 2, 3):
            y = y + cw[3 - back:4 - back, :] * done_ref[pl.ds(HALO - back, tm), :]
        y = _silu(y)
        is_qk = jt < n_qk_tiles
        qscale = jnp.where(jt < n_q_tiles, HEAD ** -0.5, 1.0).astype(F32)
        for hh in range(tn // HEAD):
            yh = y[:, hh * HEAD:(hh + 1) * HEAD]
            ss = jnp.sum(yh * yh, axis=-1, keepdims=True)
            o_ref[:, hh * HEAD:(hh + 1) * HEAD] = yh * jnp.where(is_qk, lax.rsqrt(ss + L2_EPS) * qscale, 1.0)
        next_ref[HALO:, :] = _dot(h_ref[...], w_ref[...])

    @pl.when(j % 2 == 0)
    def _():
        step(acc1_ref, acc0_ref)

    @pl.when(j % 2 == 1)
    def _():
        step(acc0_ref, acc1_ref)


def _norm_matmul_conv(x, gain, shift, scale, w, conv_w, qk_width):
    bsz, seq, d = x.shape
    n = conv_w.shape[1]
    tm = _pick(seq, (1024, 512, 256, 128))
    tn = _pick(qk_width, (512, 256, 128))
    n_tiles = n // tn
    kern = functools.partial(_nmm_conv_kernel, n_q_tiles=qk_width // tn, n_qk_tiles=2 * qk_width // tn)
    cur = lambda j: jnp.minimum(j, n_tiles - 1)
    old = lambda j: jnp.maximum(j - 1, 0)
    return pl.pallas_call(
        kern,
        grid=(bsz, seq // tm, n_tiles + 1),
        in_specs=[
            pl.BlockSpec((None, tm, d), lambda b, i, j: (b, i, 0)),
            pl.BlockSpec((1, d), lambda b, i, j: (0, 0)),
            pl.BlockSpec((None, 1, d), lambda b, i, j: (b, 0, 0)),
            pl.BlockSpec((None, 1, d), lambda b, i, j: (b, 0, 0)),
            pl.BlockSpec((d, tn), lambda b, i, j: (0, cur(j))),
            pl.BlockSpec((CONV_WIDTH, tn), lambda b, i, j: (0, old(j))),
        ],
        out_specs=pl.BlockSpec((None, tm, tn), lambda b, i, j: (b, i, old(j))),
        out_shape=jax.ShapeDtypeStruct((bsz, seq, n), F32),
        scratch_shapes=[pltpu.VMEM((tm, d), BF16), pltpu.VMEM((n_tiles, HALO, tn), F32),
                        pltpu.VMEM((HALO + tm, tn), F32), pltpu.VMEM((HALO + tm, tn), F32)],
        compiler_params=_cparams(3),
        name="norm_matmul_conv",
    )(x, gain.reshape(1, d), shift.reshape(bsz, 1, d), scale.reshape(bsz, 1, d), w, conv_w)


def _delta_kernel(q_ref, k_ref, v_ref, z_ref, ba_ref, par_ref, og_ref, o_ref,
                  state_ref, u_ref, w_ref, qd_ref, kd_ref, at_ref, a_ref,
                  *, heads_per_step, chunks_per_step, n_vheads):
    s = pl.program_id(2)
    hg = pl.program_id(1)
    g_heads = heads_per_step
    c = CHUNK
    slot = s % 2
    prev = 1 - slot

    @pl.when(s == 0)
    def _():
        state_ref[...] = jnp.zeros(state_ref.shape, F32)
        u_ref[1] = jnp.zeros(u_ref.shape[1:], F32)
        w_ref[1] = jnp.zeros(w_ref.shape[1:], BF16)
        qd_ref[1] = jnp.zeros(qd_ref.shape[1:], BF16)
        kd_ref[1] = jnp.zeros(kd_ref.shape[1:], BF16)
        at_ref[1] = jnp.zeros(at_ref.shape[1:], BF16)
        a_ref[1] = jnp.zeros(a_ref.shape[1:], F32)

    row = lax.broadcasted_iota(jnp.int32, (c, c), 0)
    col = lax.broadcasted_iota(jnp.int32, (c, c), 1)
    lower = row >= col
    strict = row > col
    eye = row == col
    lane = lax.broadcasted_iota(jnp.int32, (c, HEAD), 1)
    tril = jnp.where(lower, 1.0, 0.0).astype(BF16)

    items = [(cb, h) for cb in range(chunks_per_step) for h in range(g_heads)]
    rows_of = lambda cb: slice(cb * c, (cb + 1) * c)
    cols_of = lambda i: slice(i * HEAD, (i + 1) * HEAD)

    pcols = DELTA_PACK * c
    prow = lax.broadcasted_iota(jnp.int32, (c, pcols), 0)
    plane = lax.broadcasted_iota(jnp.int32, (c, pcols), 1)
    pcol = jnp.bitwise_and(plane, c - 1)
    pblk = jnp.right_shift(plane, c.bit_length() - 1)
    p_lower = prow >= pcol
    p_strict = prow > pcol
    p_eye = prow == pcol
    in_block = [pblk == i for i in range(DELTA_PACK)]
    packs = [(cb, qd) for cb in range(chunks_per_step) for qd in range(g_heads // DELTA_PACK)]
    heads_of = lambda qd: [DELTA_PACK * qd + i for i in range(DELTA_PACK)]

    def pack_columns(cols):
        out = cols[-1]
        for i in reversed(range(DELTA_PACK - 1)):
            out = jnp.where(in_block[i], cols[i], out)
        return out

    def only_block(mat, i):
        return jnp.where(in_block[i], mat, 0.0).astype(BF16)

    def block_diag(mat):
        return jnp.concatenate([only_block(mat, i) for i in range(DELTA_PACK)], axis=0)

    comb, kk, qk = {}, {}, {}
    xm, rhs, fac = {}, {}, {}

    def gates_stage():
        for cb in range(chunks_per_step):
            ba = ba_ref[rows_of(cb), :]
            beta_full = _sigmoid(ba)
            g_full = -jnp.exp(par_ref[0:1, :]) * _softplus(ba + par_ref[1:2, :])
            g_hi, g_mid, g_lo = _split3(g_full)
            gc_full = _dot(tril, g_hi) + _dot(tril, g_mid) + _dot(tril, g_lo)
            comb[cb] = jnp.where(lane < n_vheads, beta_full, gc_full)

    def gram_stage():
        for cb in range(chunks_per_step):
            for kh_idx in range(g_heads // 2):
                k16 = k_ref[rows_of(cb), cols_of(kh_idx)].astype(BF16)
                q16 = q_ref[rows_of(cb), cols_of(kh_idx)].astype(BF16)
                k_twice = jnp.concatenate([k16, k16], axis=0)
                both = _dot_nt(jnp.concatenate([k16, q16], axis=0), k_twice)
                kk[cb, kh_idx] = both[:c, :]
                qk[cb, kh_idx] = both[c:, :]

    def decay_stage():
        for cb, qd in packs:
            betas, gcs = [], []
            for h in heads_of(qd):
                head = hg * g_heads + h
                kh = k_ref[rows_of(cb), cols_of(h // 2)]
                qh = q_ref[rows_of(cb), cols_of(h // 2)]
                vh = v_ref[rows_of(cb), cols_of(h)]
                beta_c = jnp.sum(jnp.where(lane == head, comb[cb], 0.0), axis=-1, keepdims=True)
                gc_c = jnp.sum(jnp.where(lane == head + n_vheads, comb[cb], 0.0), axis=-1, keepdims=True)
                gl = gc_c[c - 1:c, :]
                eg = jnp.exp(gc_c)
                rhs[cb, h] = jnp.concatenate([vh * beta_c, kh * (beta_c * eg)], axis=1)
                qd_ref[slot, cb, h] = (qh * eg).astype(BF16)
                kd_ref[slot, cb, h] = (kh * jnp.exp(gl - gc_c)).astype(BF16)
                a_ref[slot, cb, h] = jnp.broadcast_to(jnp.exp(gl), (1, HEAD))
                betas.append(beta_c)
                gcs.append(gc_c)
            first_kh = DELTA_PACK * qd // 2
            kk_p = jnp.concatenate([kk[cb, first_kh + i] for i in range(DELTA_PACK // 2)], axis=1)
            qk_p = jnp.concatenate([qk[cb, first_kh + i] for i in range(DELTA_PACK // 2)], axis=1)
            gc_p = pack_columns(gcs)
            gc_r = jnp.sum(jnp.where(p_eye, gc_p, 0.0), axis=0, keepdims=True)
            decay = jnp.exp(jnp.where(p_lower, gc_p - gc_r, NEG))
            lmat = jnp.where(p_strict, kk_p * pack_columns(betas) * decay, 0.0)
            at_ref[slot, cb, qd] = block_diag(qk_p * decay)
            xm[cb, qd] = -lmat
            fac[cb, qd] = lmat

    def first_square_stage():
        for pk in packs:
            fac[pk] = _dot(fac[pk].astype(BF16), block_diag(fac[pk]))

    def round_stage():
        for pk in packs:
            both = jnp.concatenate([fac[pk].astype(BF16), xm[pk].astype(BF16)], axis=0)
            out = _dot(both, block_diag(fac[pk]))
            xm[pk] = xm[pk] + fac[pk] + out[c:, :]
            fac[pk] = out[:c, :]

    def last_round_stage():
        for pk in packs:
            xm[pk] = xm[pk] + fac[pk] + _dot(xm[pk].astype(BF16), block_diag(fac[pk]))

    def solve_stage():
        for cb, qd in packs:
            rhs_rows = jnp.concatenate([rhs[cb, h].astype(BF16) for h in heads_of(qd)], axis=0)
            sol = _dot(block_diag(xm[cb, qd]), rhs_rows)
            for i, h in enumerate(heads_of(qd)):
                uw = rhs[cb, h] + sol[i * c:(i + 1) * c, :]
                u_ref[slot, cb, h] = uw[:, :HEAD]
                w_ref[slot, cb, h] = uw[:, HEAD:].astype(BF16)

    factor_stages = [gates_stage, gram_stage, decay_stage, first_square_stage]
    factor_stages += [round_stage] * 4
    factor_stages += [last_round_stage, solve_stage]

    og = og_ref[...]
    states = [state_ref[h] for h in range(g_heads)]
    wqs, v16 = {}, {}

    def read_stage(cb):
        def run():
            for h in range(g_heads):
                wq = jnp.concatenate([w_ref[prev, cb, h], qd_ref[prev, cb, h]], axis=0)
                wqs[h] = _dot(wq, states[h].astype(BF16))
        return run

    def update_stage(cb):
        def run():
            for h in range(g_heads):
                v16[h] = (u_ref[prev, cb, h] - wqs[h][:c, :]).astype(BF16)
            intra = {}
            for qd in range(g_heads // DELTA_PACK):
                v_rows = jnp.concatenate([v16[h] for h in heads_of(qd)], axis=0)
                av = _dot(at_ref[prev, cb, qd], v_rows)
                for i, h in enumerate(heads_of(qd)):
                    intra[h] = av[i * c:(i + 1) * c, :]
            for h in range(g_heads):
                o = wqs[h][c:, :] + intra[h]
                states[h] = states[h] * a_ref[prev, cb, h] + _dot_tn(kd_ref[prev, cb, h], v16[h])
                ms = jnp.mean(o * o, axis=-1, keepdims=True)
                zh = z_ref[rows_of(cb), cols_of(h)]
                o_ref[rows_of(cb), cols_of(h)] = (o * lax.rsqrt(ms + RMS_EPS) * og * _silu(zh)).astype(o_ref.dtype)
        return run

    state_stages = []
    for cb in range(chunks_per_step):
        state_stages += [read_stage(cb), update_stage(cb)]

    stride = max(1, len(factor_stages) // (len(state_stages) + 1))
    for idx, stage in enumerate(factor_stages):
        stage()
        if idx % stride == stride - 1 and state_stages:
            state_stages.pop(0)()
    for stage in state_stages:
        stage()
    for h in range(g_heads):
        state_ref[h] = states[h]


def _delta_rule(qkv, z, ba, gate_par, o_gain, qk_width, heads_per_step=8, chunks_per_step=4):
    bsz, seq, _ = qkv.shape
    v_width = z.shape[-1]
    n_vheads = v_width // HEAD
    g = min(heads_per_step, n_vheads)
    assert g % DELTA_PACK == 0 and n_vheads % g == 0
    gk = g // 2
    cbs = chunks_per_step
    rows = cbs * CHUNK
    n_blocks = seq // rows
    n_kblocks = qk_width // (gk * HEAD)
    kern = functools.partial(_delta_kernel, heads_per_step=g, chunks_per_step=cbs, n_vheads=n_vheads)
    cur = lambda s: jnp.minimum(s, n_blocks - 1)
    old = lambda s: jnp.maximum(s - 1, 0)
    return pl.pallas_call(
        kern,
        grid=(bsz, n_vheads // g, n_blocks + 1),
        in_specs=[
            pl.BlockSpec((None, rows, gk * HEAD), lambda b, hg, s: (b, cur(s), hg)),
            pl.BlockSpec((None, rows, gk * HEAD), lambda b, hg, s: (b, cur(s), n_kblocks + hg)),
            pl.BlockSpec((None, rows, g * HEAD), lambda b, hg, s: (b, cur(s), 2 * qk_width // (g * HEAD) + hg)),
            pl.BlockSpec((None, rows, g * HEAD), lambda b, hg, s: (b, old(s), hg)),
            pl.BlockSpec((None, rows, HEAD), lambda b, hg, s: (b, cur(s), 0)),
            pl.BlockSpec((2, HEAD), lambda b, hg, s: (0, 0)),
            pl.BlockSpec((1, HEAD), lambda b, hg, s: (0, 0)),
        ],
        out_specs=pl.BlockSpec((None, rows, g * HEAD), lambda b, hg, s: (b, old(s), hg)),
        out_shape=jax.ShapeDtypeStruct((bsz, seq, v_width), BF16),
        scratch_shapes=[
            pltpu.VMEM((g, HEAD, HEAD), F32),
            pltpu.VMEM((2, cbs, g, CHUNK, HEAD), F32),
            pltpu.VMEM((2, cbs, g, CHUNK, HEAD), BF16),
            pltpu.VMEM((2, cbs, g, CHUNK, HEAD), BF16),
            pltpu.VMEM((2, cbs, g, CHUNK, HEAD), BF16),
            pltpu.VMEM((2, cbs, g // DELTA_PACK, DELTA_PACK * CHUNK, DELTA_PACK * CHUNK), BF16),
            pltpu.VMEM((2, cbs, g, 1, HEAD), F32),
        ],
        compiler_params=_cparams(3),
        name="delta_rule",
    )(qkv, qkv, qkv, z, ba, gate_par, o_gain.reshape(1, HEAD))


def _out_proj_kernel(y_ref, w_ref, x_ref, gate_ref, o_ref):
    o_ref[...] = x_ref[...] + gate_ref[...] * _dot(y_ref[...].astype(BF16), w_ref[...])


def _out_proj(y, w, x, gate):
    bsz, seq, k = y.shape
    d = w.shape[1]
    tm = _pick(seq, (1024, 512, 256, 128))
    tn = _pick(d, (512, 256, 128))
    return pl.pallas_call(
        _out_proj_kernel,
        grid=(bsz, seq // tm, d // tn),
        in_specs=[
            pl.BlockSpec((None, tm, k), lambda b, i, j: (b, i, 0)),
            pl.BlockSpec((k, tn), lambda b, i, j: (0, j)),
            pl.BlockSpec((None, tm, tn), lambda b, i, j: (b, i, j)),
            pl.BlockSpec((None, 1, tn), lambda b, i, j: (b, 0, j)),
        ],
        out_specs=pl.BlockSpec((None, tm, tn), lambda b, i, j: (b, i, j)),
        out_shape=jax.ShapeDtypeStruct((bsz, seq, d), F32),
        compiler_params=_cparams(3),
        name="out_proj",
    )(y, w, x, gate.reshape(bsz, 1, d))


def _block_slabs(dil, res, nb):
    blocks_per_res = ATT_TILE // (BAND * dil)
    slab = PLANE_ROWS // blocks_per_res
    return [((res + dil * a) * PLANE_ROWS + slab * nb, slab) for a in range(PLANES // dil)]


def _block_positions(dil):
    n_slabs = PLANES // dil
    slab = BAND // n_slabs
    return [n_slabs * i + a for a in range(n_slabs) for i in range(slab)]


def _attn_blocks():
    blocks = []
    for gi, (_, dil) in enumerate(DILATION_GROUPS):
        blocks_per_res = ATT_TILE // (BAND * dil)
        for res in range(dil):
            for nb in range(blocks_per_res):
                prev_nb = (nb - 1) % blocks_per_res
                blocks.append((gi, _block_slabs(dil, res, nb), _block_slabs(dil, res, prev_nb), nb == 0))
    return blocks


def _gather_rows(ref, slabs):
    parts = [ref[start:start + size, :] for start, size in slabs]
    return parts[0] if len(parts) == 1 else jnp.concatenate(parts, axis=0)


def _scatter_rows(ref, slabs, val):
    off = 0
    for start, size in slabs:
        ref[start:start + size, :] = val[off:off + size, :].astype(ref.dtype)
        off += size


def _attn_kernel(q0_ref, q1_ref, q2_ref, kp_ref, ko_ref, vp_ref, vo_ref, z_ref, bias_ref, o_ref,
                 m_ref, l_ref, acc_ref):
    first_tile = pl.program_id(1) == 0
    col = lax.broadcasted_iota(jnp.int32, (1, 2 * BAND), 1)
    head_penalty = jnp.where((col < BAND) & first_tile, NEG, 0.0)
    scale = HEAD ** -0.5
    q_refs = (q0_ref, q1_ref, q2_ref)
    last_group = len(DILATION_GROUPS) - 1

    def score_stage(blk):
        gi, q_slabs, prev_slabs, prev_tile = blk
        q16 = _gather_rows(q_refs[gi], q_slabs).astype(BF16)
        k_prev = _gather_rows(kp_ref if prev_tile else ko_ref, prev_slabs)
        k16 = jnp.concatenate([k_prev, _gather_rows(ko_ref, q_slabs)], axis=0).astype(BF16)
        return _dot_nt(q16, k16)

    def softmax_stage(blk, s):
        gi, _, _, prev_tile = blk
        bias = bias_ref[gi]
        if prev_tile:
            bias = bias + head_penalty
        s = s * scale + bias
        m = jnp.max(s, axis=-1, keepdims=True)
        p = jnp.exp(s - m)
        return m, jnp.sum(p, axis=-1, keepdims=True), p.astype(BF16)

    def value_stage(blk, p16):
        _, q_slabs, prev_slabs, prev_tile = blk
        v_prev = _gather_rows(vp_ref if prev_tile else vo_ref, prev_slabs)
        v16 = jnp.concatenate([v_prev, _gather_rows(vo_ref, q_slabs)], axis=0).astype(BF16)
        return _dot(p16, v16)

    def merge_stage(blk, m, l, pv):
        gi, q_slabs, _, _ = blk
        if gi == 0:
            m_new = jnp.broadcast_to(m, (BAND, HEAD))
            l_new = jnp.broadcast_to(l, (BAND, HEAD))
            acc_new = pv
        else:
            m_old = _gather_rows(m_ref, q_slabs)
            m_new = jnp.maximum(m_old, m)
            w_old = jnp.exp(m_old - m_new)
            w_blk = jnp.exp(m - m_new)
            l_new = w_old * _gather_rows(l_ref, q_slabs) + w_blk * l
            acc_new = w_old * _gather_rows(acc_ref, q_slabs) + w_blk * pv
        if gi == last_group:
            _scatter_rows(o_ref, q_slabs, acc_new / l_new * _silu(_gather_rows(z_ref, q_slabs)))
        else:
            _scatter_rows(m_ref, q_slabs, m_new)
            _scatter_rows(l_ref, q_slabs, l_new)
            _scatter_rows(acc_ref, q_slabs, acc_new)

    blocks = _attn_blocks()
    batches = [blocks[i:i + ATT_BATCH] for i in range(0, len(blocks), ATT_BATCH)]
    scores_next = [score_stage(blk) for blk in batches[0]]
    for bi, batch in enumerate(batches):
        scores = scores_next
        if bi + 1 < len(batches):
            scores_next = [score_stage(blk) for blk in batches[bi + 1]]
        stats = [softmax_stage(blk, s) for blk, s in zip(batch, scores)]
        pvs = [value_stage(blk, p16) for blk, (_, _, p16) in zip(batch, stats)]
        for blk, (m, l, _), pv in zip(batch, stats, pvs):
            merge_stage(blk, m, l, pv)


def _dilated_attention(qz, kv, bias):
    bsz, seq, _ = qz.shape
    width = kv.shape[-1] // 2
    heads = width // HEAD
    t = ATT_TILE
    row_tile = lambda b, i, h: (b, i, h)

    def q_spec(gi):
        return pl.BlockSpec((None, t, HEAD), lambda b, i, h: (b, i, gi * heads + h))

    return pl.pallas_call(
        _attn_kernel,
        grid=(bsz, seq // t, heads),
        in_specs=[
            q_spec(0), q_spec(1), q_spec(2),
            pl.BlockSpec((None, t, HEAD), lambda b, i, h: (b, jnp.maximum(i - 1, 0), h)),
            pl.BlockSpec((None, t, HEAD), row_tile),
            pl.BlockSpec((None, t, HEAD), lambda b, i, h: (b, jnp.maximum(i - 1, 0), heads + h)),
            pl.BlockSpec((None, t, HEAD), lambda b, i, h: (b, i, heads + h)),
            pl.BlockSpec((None, t, HEAD), lambda b, i, h: (b, i, 3 * heads + h)),
            pl.BlockSpec((3, None, BAND, 2 * BAND), lambda b, i, h: (0, h, 0, 0)),
        ],
        out_specs=pl.BlockSpec((None, t, HEAD), row_tile),
        out_shape=jax.ShapeDtypeStruct((bsz, seq, width), BF16),
        scratch_shapes=[pltpu.VMEM((t, HEAD), F32)] * 3,
        compiler_params=_cparams(3),
        name="dilated_attention",
    )(qz, qz, qz, kv, kv, kv, kv, qz, bias)


def _t5_bucket(dist):
    max_exact = N_BUCKETS // 2
    n = jnp.maximum(dist, 0)
    large = max_exact + (jnp.log(jnp.maximum(n, 1).astype(F32) / max_exact)
                         / math.log(MAX_DISTANCE / max_exact)
                         * (N_BUCKETS - max_exact)).astype(jnp.int32)
    large = jnp.minimum(large, N_BUCKETS - 1)
    return jnp.where(n < max_exact, n, large)


def _band_bias(rel_bias, heads):
    out = []
    for gi, (_, dil) in enumerate(DILATION_GROUPS):
        pos = jnp.asarray(_block_positions(dil), jnp.int32)
        key_pos = jnp.concatenate([pos, pos + BAND])
        rel = pos[:, None] + BAND - key_pos[None, :]
        valid = (rel >= 0) & (rel <= BAND)
        table = rel_bias[:, gi * heads:(gi + 1) * heads].astype(F32)
        onehot = (_t5_bucket(rel * dil)[None] == jnp.arange(N_BUCKETS)[:, None, None]).astype(F32)
        b = jnp.einsum("nh,nqk->hqk", table, onehot, precision=lax.Precision.HIGHEST)
        out.append(jnp.where(valid[None], b, NEG))
    return jnp.stack(out, axis=0)


def _permute_kernel(x_ref, o_ref):
    d = o_ref.shape[-1]
    for k in range(PLANES_PER_STEP):
        o_ref[k * PLANE_ROWS:(k + 1) * PLANE_ROWS, :] = x_ref[:, k * d:(k + 1) * d]


def _permute_rows(x):
    bsz, seq, d = x.shape
    n_tiles = seq // ATT_TILE
    steps = PLANES // PLANES_PER_STEP
    return pl.pallas_call(
        _permute_kernel,
        grid=(bsz, n_tiles, steps),
        in_specs=[pl.BlockSpec((None, PLANE_ROWS, PLANES_PER_STEP * d), lambda b, t, r: (b, t, r))],
        out_specs=pl.BlockSpec((None, PLANES_PER_STEP * PLANE_ROWS, d), lambda b, t, r: (b, t * steps + r, 0)),
        out_shape=jax.ShapeDtypeStruct((bsz, seq, d), x.dtype),
        compiler_params=_cparams(3),
        name="permute_rows",
    )(x.reshape(bsz, seq // PLANES, PLANES * d))


def _rmsnorm_unpermute_kernel(x_ref, g_ref, o_ref):
    d = x_ref.shape[-1]
    for k in range(PLANES_PER_STEP):
        x = x_ref[k * PLANE_ROWS:(k + 1) * PLANE_ROWS, :]
        ms = jnp.mean(x * x, axis=-1, keepdims=True)
        o_ref[:, k * d:(k + 1) * d] = x * lax.rsqrt(ms + RMS_EPS) * g_ref[...]


def _rmsnorm_unpermute(x, gain):
    bsz, seq, d = x.shape
    n_tiles = seq // ATT_TILE
    steps = PLANES // PLANES_PER_STEP
    out = pl.pallas_call(
        _rmsnorm_unpermute_kernel,
        grid=(bsz, n_tiles, steps),
        in_specs=[pl.BlockSpec((None, PLANES_PER_STEP * PLANE_ROWS, d), lambda b, t, r: (b, t * steps + r, 0)),
                  pl.BlockSpec((1, d), lambda b, t, r: (0, 0))],
        out_specs=pl.BlockSpec((None, PLANE_ROWS, PLANES_PER_STEP * d), lambda b, t, r: (b, t, r)),
        out_shape=jax.ShapeDtypeStruct((bsz, seq // PLANES, PLANES * d), F32),
        compiler_params=_cparams(3),
        name="final_rmsnorm",
    )(x, gain.reshape(1, d))
    return out.reshape(bsz, seq, d)


def kernel(x, c, norm_gain, w_mod, b_mod, w_in_a, conv_w_a, a_log, dt_bias, o_norm_a, w_out_a,
           kv_gain, w_kv_mod, b_kv_mod, w_kv, w_in_b, w_out_b, rel_bias, final_gain):
    bsz, seq, d = x.shape
    depth = norm_gain.shape[0]
    n_a = w_in_a.shape[0]
    qk_width = d
    v_width = w_out_a.shape[1]
    n_vheads = v_width // HEAD
    conv_ch = 2 * qk_width + v_width
    dil_width = w_out_b.shape[1]
    dil_heads = dil_width // HEAD
    assert seq % ATT_TILE == 0 and seq % CHUNK == 0 and 2 * n_vheads <= HEAD

    mods = _mod_proj(c, w_mod, b_mod)
    kv_mod = _mod_proj(c, w_kv_mod[None], b_kv_mod[None])[0]
    bias = _band_bias(rel_bias, dil_heads)

    kv = None
    for layer in range(depth):
        shift, scale, gate = mods[layer, :, :d], mods[layer, :, d:2 * d], mods[layer, :, 2 * d:]
        if layer < n_a:
            w_in = w_in_a[layer].astype(BF16)
            qkv = _norm_matmul_conv(x, norm_gain[layer], shift, scale, w_in, conv_w_a[layer], qk_width)
            z = _norm_matmul(x, norm_gain[layer], shift, scale, w_in, col_start=conv_ch, n_cols=v_width)
            w_ba = jnp.pad(w_in[:, conv_ch + v_width:], ((0, 0), (0, HEAD - 2 * n_vheads)))
            ba = _norm_matmul(x, norm_gain[layer], shift, scale, w_ba)
            gate_par = jnp.zeros((2, HEAD), F32)
            gate_par = gate_par.at[0, n_vheads:2 * n_vheads].set(a_log[layer])
            gate_par = gate_par.at[1, n_vheads:2 * n_vheads].set(dt_bias[layer])
            y = _delta_rule(qkv, z, ba, gate_par, o_norm_a[layer], qk_width)
            x = _out_proj(y, w_out_a[layer].astype(BF16), x, gate)
        else:
            if kv is None:
                x = _permute_rows(x)
                kv = _norm_matmul(x, kv_gain, kv_mod[:, :d], kv_mod[:, d:], w_kv.astype(BF16))
            j = layer - n_a
            qz = _norm_matmul(x, norm_gain[layer], shift, scale, w_in_b[j].astype(BF16))
            y = _dilated_attention(qz, kv, bias)
            x = _out_proj(y, w_out_b[j].astype(BF16), x, gate)
    assert kv is not None
    return _rmsnorm_unpermute(x, final_gain)
```

```python
import functools
import math

import jax
import jax.numpy as jnp
from jax import lax
from jax.experimental import pallas as pl
from jax.experimental.pallas import tpu as pltpu

F32 = jnp.float32
BF16 = jnp.bfloat16

RMS_EPS = 1e-6
L2_EPS = 1e-6
HEAD = 128
BF16_SUBLANES = 16
CHUNK = 64
DELTA_PACK = 4
CONV_WIDTH = 4
HALO = 8
DILATION_GROUPS = ((128, 1), (512, 4), (2048, 16))
BAND = 128
ATT_TILE = 2048
PLANES = 16
PLANE_ROWS = ATT_TILE // PLANES
PLANES_PER_STEP = 4
ATT_BATCH = 4
N_BUCKETS = 32
MAX_DISTANCE = 2048
NEG = -1e30
VMEM_LIMIT = 56 * 1024 * 1024


def _cparams(n_axes):
    return pltpu.CompilerParams(
        dimension_semantics=("arbitrary",) * n_axes, vmem_limit_bytes=VMEM_LIMIT)


def _dot(a, b):
    return jnp.dot(a, b, preferred_element_type=F32)


def _dot_nt(a, b):
    return lax.dot_general(a, b, (((1,), (1,)), ((), ())), preferred_element_type=F32)


def _dot_tn(a, b):
    return lax.dot_general(a, b, (((0,), (0,)), ((), ())), preferred_element_type=F32)


def _sigmoid(x):
    return 1.0 / (1.0 + jnp.exp(-x))


def _silu(x):
    return x * _sigmoid(x)


def _softplus(x):
    return jnp.maximum(x, 0.0) + jnp.log(1.0 + jnp.exp(-jnp.abs(x)))


def _split3(x):
    hi = x.astype(BF16)
    r1 = x - hi.astype(F32)
    mid = r1.astype(BF16)
    lo = (r1 - mid.astype(F32)).astype(BF16)
    return hi, mid, lo


def _pick(n, prefs):
    for p in prefs:
        if n % p == 0:
            return p
    return n


def _mod_kernel(c_ref, w_ref, b_ref, o_ref):
    ca = _silu(c_ref[...])
    w = w_ref[...]
    c_hi, c_mid, c_lo = _split3(ca)
    w_hi = w.astype(BF16)
    w_lo = (w - w_hi.astype(F32)).astype(BF16)
    acc = _dot(c_hi, w_hi) + _dot(c_hi, w_lo) + _dot(c_mid, w_hi) + _dot(c_lo, w_hi)
    o_ref[...] = acc + b_ref[...]


def _mod_proj(c, w, b):
    nl, d, n = w.shape
    n_rows = c.shape[0]
    bsz = -(-n_rows // BF16_SUBLANES) * BF16_SUBLANES
    c = jnp.pad(c, ((0, bsz - n_rows), (0, 0)))
    tn = _pick(n, (512, 256, 128))
    out = pl.pallas_call(
        _mod_kernel,
        grid=(nl, n // tn),
        in_specs=[
            pl.BlockSpec((bsz, d), lambda l, j: (0, 0)),
            pl.BlockSpec((None, d, tn), lambda l, j: (l, 0, j)),
            pl.BlockSpec((None, 1, tn), lambda l, j: (l, 0, j)),
        ],
        out_specs=pl.BlockSpec((None, bsz, tn), lambda l, j: (l, 0, j)),
        out_shape=jax.ShapeDtypeStruct((nl, bsz, n), F32),
        compiler_params=_cparams(2),
        name="mod_proj",
    )(c, w, b.reshape(nl, 1, n))
    return out[:, :n_rows]


def _modnorm(x, gain, shift, scale):
    ms = jnp.mean(x * x, axis=-1, keepdims=True)
    return x * lax.rsqrt(ms + RMS_EPS) * (gain * (1.0 + scale)) + shift


def _nmm_plain_kernel(x_ref, g_ref, sh_ref, sc_ref, w_ref, o_ref, h_ref):
    @pl.when(pl.program_id(2) == 0)
    def _():
        h_ref[...] = _modnorm(x_ref[...], g_ref[...], sh_ref[...], sc_ref[...]).astype(BF16)

    o_ref[...] = _dot(h_ref[...], w_ref[...]).astype(o_ref.dtype)


def _norm_matmul(x, gain, shift, scale, w, layer=0, col_start=0, n_cols=None, out_dtype=F32):
    bsz, seq, d = x.shape
    w = w[None] if w.ndim == 2 else w
    n = w.shape[2] - col_start if n_cols is None else n_cols
    tm = _pick(seq, (1024, 512, 256, 128))
    tn = _pick(math.gcd(n, col_start) if col_start else n, (512, 256, 128))
    first = col_start // tn
    return pl.pallas_call(
        _nmm_plain_kernel,
        grid=(bsz, seq // tm, n // tn),
        in_specs=[
            pl.BlockSpec((None, tm, d), lambda b, i, j: (b, i, 0)),
            pl.BlockSpec((1, d), lambda b, i, j: (0, 0)),
            pl.BlockSpec((None, 1, d), lambda b, i, j: (b, 0, 0)),
            pl.BlockSpec((None, 1, d), lambda b, i, j: (b, 0, 0)),
            pl.BlockSpec((None, d, tn), lambda b, i, j: (layer, 0, first + j)),
        ],
        out_specs=pl.BlockSpec((None, tm, tn), lambda b, i, j: (b, i, j)),
        out_shape=jax.ShapeDtypeStruct((bsz, seq, n), out_dtype),
        scratch_shapes=[pltpu.VMEM((tm, d), BF16)],
        compiler_params=_cparams(3),
        name="norm_matmul",
    )(x, gain.reshape(1, d), shift.reshape(bsz, 1, d), scale.reshape(bsz, 1, d), w)


def _nmm_conv_kernel(x_ref, g_ref, sh_ref, sc_ref, w_ref, cw_ref, o_ref, h_ref, halo_ref, acc0_ref, acc1_ref,
                     *, n_q_tiles, n_qk_tiles):
    i = pl.program_id(1)
    j = pl.program_id(2)
    tm, tn = o_ref.shape
    jt = jnp.maximum(j - 1, 0)

    @pl.when(j == 0)
    def _():
        h_ref[...] = _modnorm(x_ref[...], g_ref[...], sh_ref[...], sc_ref[...]).astype(BF16)
        acc1_ref[...] = jnp.zeros(acc1_ref.shape, F32)

    @pl.when((i == 0) & (j == 0))
    def _():
        halo_ref[...] = jnp.zeros(halo_ref.shape, F32)

    def step(done_ref, next_ref):
        prev_rows = halo_ref[jt]
        done_ref[0:HALO, :] = prev_rows
        acc = done_ref[HALO:, :]
        halo_ref[jt] = jnp.where(j > 0, acc[tm - HALO:, :], prev_rows)
        cw = cw_ref[...]
        y = cw[3:4, :] * acc
        for back in (1, 2, 3):
            y = y + cw[3 - back:4 - back, :] * done_ref[pl.ds(HALO - back, tm), :]
        y = _silu(y)
        is_qk = jt < n_qk_tiles
        qscale = jnp.where(jt < n_q_tiles, HEAD ** -0.5, 1.0).astype(F32)
        for hh in range(tn // HEAD):
            yh = y[:, hh * HEAD:(hh + 1) * HEAD]
            ss = jnp.sum(yh * yh, axis=-1, keepdims=True)
            o_ref[:, hh * HEAD:(hh + 1) * HEAD] = yh * jnp.where(is_qk, lax.rsqrt(ss + L2_EPS) * qscale, 1.0)
        next_ref[HALO:, :] = _dot(h_ref[...], w_ref[...])

    @pl.when(j % 2 == 0)
    def _():
        step(acc1_ref, acc0_ref)

    @pl.when(j % 2 == 1)
    def _():
        step(acc0_ref, acc1_ref)


def _norm_matmul_conv(x, gain, shift, scale, w, layer, conv_w, qk_width):
    bsz, seq, d = x.shape
    n = conv_w.shape[1]
    tm = _pick(seq, (1024, 512, 256, 128))
    tn = _pick(qk_width, (512, 256, 128))
    n_tiles = n // tn
    kern = functools.partial(_nmm_conv_kernel, n_q_tiles=qk_width // tn, n_qk_tiles=2 * qk_width // tn)
    cur = lambda j: jnp.minimum(j, n_tiles - 1)
    old = lambda j: jnp.maximum(j - 1, 0)
    return pl.pallas_call(
        kern,
        grid=(bsz, seq // tm, n_tiles + 1),
        in_specs=[
            pl.BlockSpec((None, tm, d), lambda b, i, j: (b, i, 0)),
            pl.BlockSpec((1, d), lambda b, i, j: (0, 0)),
            pl.BlockSpec((None, 1, d), lambda b, i, j: (b, 0, 0)),
            pl.BlockSpec((None, 1, d), lambda b, i, j: (b, 0, 0)),
            pl.BlockSpec((None, d, tn), lambda b, i, j: (layer, 0, cur(j))),
            pl.BlockSpec((CONV_WIDTH, tn), lambda b, i, j: (0, old(j))),
        ],
        out_specs=pl.BlockSpec((None, tm, tn), lambda b, i, j: (b, i, old(j))),
        out_shape=jax.ShapeDtypeStruct((bsz, seq, n), F32),
        scratch_shapes=[pltpu.VMEM((tm, d), BF16), pltpu.VMEM((n_tiles, HALO, tn), F32),
                        pltpu.VMEM((HALO + tm, tn), F32), pltpu.VMEM((HALO + tm, tn), F32)],
        compiler_params=_cparams(3),
        name="norm_matmul_conv",
    )(x, gain.reshape(1, d), shift.reshape(bsz, 1, d), scale.reshape(bsz, 1, d), w, conv_w)


def _delta_kernel(q_ref, k_ref, v_ref, z_ref, ba_ref, par_ref, og_ref, o_ref,
                  state_ref, u_ref, w_ref, qd_ref, kd_ref, at_ref, a_ref,
                  *, heads_per_step, chunks_per_step, n_vheads):
    s = pl.program_id(2)
    hg = pl.program_id(1)
    g_heads = heads_per_step
    c = CHUNK
    slot = s % 2
    prev = 1 - slot

    @pl.when(s == 0)
    def _():
        state_ref[...] = jnp.zeros(state_ref.shape, F32)
        u_ref[1] = jnp.zeros(u_ref.shape[1:], F32)
        w_ref[1] = jnp.zeros(w_ref.shape[1:], BF16)
        qd_ref[1] = jnp.zeros(qd_ref.shape[1:], BF16)
        kd_ref[1] = jnp.zeros(kd_ref.shape[1:], BF16)
        at_ref[1] = jnp.zeros(at_ref.shape[1:], BF16)
        a_ref[1] = jnp.zeros(a_ref.shape[1:], F32)

    row = lax.broadcasted_iota(jnp.int32, (c, c), 0)
    col = lax.broadcasted_iota(jnp.int32, (c, c), 1)
    lower = row >= col
    strict = row > col
    eye = row == col
    lane = lax.broadcasted_iota(jnp.int32, (c, HEAD), 1)
    tril = jnp.where(lower, 1.0, 0.0).astype(BF16)

    items = [(cb, h) for cb in range(chunks_per_step) for h in range(g_heads)]
    rows_of = lambda cb: slice(cb * c, (cb + 1) * c)
    cols_of = lambda i: slice(i * HEAD, (i + 1) * HEAD)

    pcols = DELTA_PACK * c
    prow = lax.broadcasted_iota(jnp.int32, (c, pcols), 0)
    plane = lax.broadcasted_iota(jnp.int32, (c, pcols), 1)
    pcol = jnp.bitwise_and(plane, c - 1)
    pblk = jnp.right_shift(plane, c.bit_length() - 1)
    p_lower = prow >= pcol
    p_strict = prow > pcol
    p_eye = prow == pcol
    in_block = [pblk == i for i in range(DELTA_PACK)]
    packs = [(cb, qd) for cb in range(chunks_per_step) for qd in range(g_heads // DELTA_PACK)]
    heads_of = lambda qd: [DELTA_PACK * qd + i for i in range(DELTA_PACK)]

    def pack_columns(cols):
        out = cols[-1]
        for i in reversed(range(DELTA_PACK - 1)):
            out = jnp.where(in_block[i], cols[i], out)
        return out

    def only_block(mat, i):
        return jnp.where(in_block[i], mat, 0.0).astype(BF16)

    def block_diag(mat):
        return jnp.concatenate([only_block(mat, i) for i in range(DELTA_PACK)], axis=0)

    comb, kk, qk = {}, {}, {}
    xm, rhs, fac = {}, {}, {}

    def gates_stage():
        for cb in range(chunks_per_step):
            ba = ba_ref[rows_of(cb), :]
            beta_full = _sigmoid(ba)
            g_full = -jnp.exp(par_ref[0:1, :]) * _softplus(ba + par_ref[1:2, :])
            g_hi, g_mid, g_lo = _split3(g_full)
            gc_full = _dot(tril, g_hi) + _dot(tril, g_mid) + _dot(tril, g_lo)
            comb[cb] = jnp.where(lane < n_vheads, beta_full, gc_full)

    def gram_stage():
        for cb in range(chunks_per_step):
            for kh_idx in range(g_heads // 2):
                k16 = k_ref[rows_of(cb), cols_of(kh_idx)].astype(BF16)
                q16 = q_ref[rows_of(cb), cols_of(kh_idx)].astype(BF16)
                k_twice = jnp.concatenate([k16, k16], axis=0)
                both = _dot_nt(jnp.concatenate([k16, q16], axis=0), k_twice)
                kk[cb, kh_idx] = both[:c, :]
                qk[cb, kh_idx] = both[c:, :]

    def decay_stage():
        for cb, qd in packs:
            betas, gcs = [], []
            for h in heads_of(qd):
                head = hg * g_heads + h
                kh = k_ref[rows_of(cb), cols_of(h // 2)]
                qh = q_ref[rows_of(cb), cols_of(h // 2)]
                vh = v_ref[rows_of(cb), cols_of(h)]
                beta_c = jnp.sum(jnp.where(lane == head, comb[cb], 0.0), axis=-1, keepdims=True)
                gc_c = jnp.sum(jnp.where(lane == head + n_vheads, comb[cb], 0.0), axis=-1, keepdims=True)
                gl = gc_c[c - 1:c, :]
                eg = jnp.exp(gc_c)
                rhs[cb, h] = jnp.concatenate([vh * beta_c, kh * (beta_c * eg)], axis=1)
                qd_ref[slot, cb, h] = (qh * eg).astype(BF16)
                kd_ref[slot, cb, h] = (kh * jnp.exp(gl - gc_c)).astype(BF16)
                a_ref[slot, cb, h] = jnp.broadcast_to(jnp.exp(gl), (1, HEAD))
                betas.append(beta_c)
                gcs.append(gc_c)
            first_kh = DELTA_PACK * qd // 2
            kk_p = jnp.concatenate([kk[cb, first_kh + i] for i in range(DELTA_PACK // 2)], axis=1)
            qk_p = jnp.concatenate([qk[cb, first_kh + i] for i in range(DELTA_PACK // 2)], axis=1)
            gc_p = pack_columns(gcs)
            gc_r = jnp.sum(jnp.where(p_eye, gc_p, 0.0), axis=0, keepdims=True)
            decay = jnp.exp(jnp.where(p_lower, gc_p - gc_r, NEG))
            lmat = jnp.where(p_strict, kk_p * pack_columns(betas) * decay, 0.0)
            at_ref[slot, cb, qd] = block_diag(qk_p * decay)
            xm[cb, qd] = -lmat
            fac[cb, qd] = lmat

    def first_square_stage():
        for pk in packs:
            fac[pk] = _dot(fac[pk].astype(BF16), block_diag(fac[pk]))

    def round_stage():
        for pk in packs:
            both = jnp.concatenate([fac[pk].astype(BF16), xm[pk].astype(BF16)], axis=0)
            out = _dot(both, block_diag(fac[pk]))
            xm[pk] = xm[pk] + fac[pk] + out[c:, :]
            fac[pk] = out[:c, :]

    def last_round_stage():
        for pk in packs:
            xm[pk] = xm[pk] + fac[pk] + _dot(xm[pk].astype(BF16), block_diag(fac[pk]))

    def solve_stage():
        for cb, qd in packs:
            rhs_rows = jnp.concatenate([rhs[cb, h].astype(BF16) for h in heads_of(qd)], axis=0)
            sol = _dot(block_diag(xm[cb, qd]), rhs_rows)
            for i, h in enumerate(heads_of(qd)):
                uw = rhs[cb, h] + sol[i * c:(i + 1) * c, :]
                u_ref[slot, cb, h] = uw[:, :HEAD]
                w_ref[slot, cb, h] = uw[:, HEAD:].astype(BF16)

    factor_stages = [gates_stage, gram_stage, decay_stage, first_square_stage]
    factor_stages += [round_stage] * 4
    factor_stages += [last_round_stage, solve_stage]

    og = og_ref[...]
    states = [state_ref[h] for h in range(g_heads)]
    wqs, v16 = {}, {}

    def read_stage(cb):
        def run():
            for h in range(g_heads):
                wq = jnp.concatenate([w_ref[prev, cb, h], qd_ref[prev, cb, h]], axis=0)
                wqs[h] = _dot(wq, states[h].astype(BF16))
        return run

    def update_stage(cb):
        def run():
            for h in range(g_heads):
                v16[h] = (u_ref[prev, cb, h] - wqs[h][:c, :]).astype(BF16)
            intra = {}
            for qd in range(g_heads // DELTA_PACK):
                v_rows = jnp.concatenate([v16[h] for h in heads_of(qd)], axis=0)
                av = _dot(at_ref[prev, cb, qd], v_rows)
                for i, h in enumerate(heads_of(qd)):
                    intra[h] = av[i * c:(i + 1) * c, :]
            for h in range(g_heads):
                o = wqs[h][c:, :] + intra[h]
                states[h] = states[h] * a_ref[prev, cb, h] + _dot_tn(kd_ref[prev, cb, h], v16[h])
                ms = jnp.mean(o * o, axis=-1, keepdims=True)
                zh = z_ref[rows_of(cb), cols_of(h)]
                o_ref[rows_of(cb), cols_of(h)] = (o * lax.rsqrt(ms + RMS_EPS) * og * _silu(zh)).astype(o_ref.dtype)
        return run

    state_stages = []
    for cb in range(chunks_per_step):
        state_stages += [read_stage(cb), update_stage(cb)]

    stride = max(1, len(factor_stages) // (len(state_stages) + 1))
    for idx, stage in enumerate(factor_stages):
        stage()
        if idx % stride == stride - 1 and state_stages:
            state_stages.pop(0)()
    for stage in state_stages:
        stage()
    for h in range(g_heads):
        state_ref[h] = states[h]


def _delta_rule(qkv, z, ba, gate_par, o_gain, qk_width, heads_per_step=8, chunks_per_step=4):
    bsz, seq, _ = qkv.shape
    v_width = z.shape[-1]
    n_vheads = v_width // HEAD
    g = min(heads_per_step, n_vheads)
    assert g % DELTA_PACK == 0 and n_vheads % g == 0
    gk = g // 2
    cbs = chunks_per_step
    rows = cbs * CHUNK
    n_blocks = seq // rows
    n_kblocks = qk_width // (gk * HEAD)
    kern = functools.partial(_delta_kernel, heads_per_step=g, chunks_per_step=cbs, n_vheads=n_vheads)
    cur = lambda s: jnp.minimum(s, n_blocks - 1)
    old = lambda s: jnp.maximum(s - 1, 0)
    return pl.pallas_call(
        kern,
        grid=(bsz, n_vheads // g, n_blocks + 1),
        in_specs=[
            pl.BlockSpec((None, rows, gk * HEAD), lambda b, hg, s: (b, cur(s), hg)),
            pl.BlockSpec((None, rows, gk * HEAD), lambda b, hg, s: (b, cur(s), n_kblocks + hg)),
            pl.BlockSpec((None, rows, g * HEAD), lambda b, hg, s: (b, cur(s), 2 * qk_width // (g * HEAD) + hg)),
            pl.BlockSpec((None, rows, g * HEAD), lambda b, hg, s: (b, old(s), hg)),
            pl.BlockSpec((None, rows, HEAD), lambda b, hg, s: (b, cur(s), 0)),
            pl.BlockSpec((2, HEAD), lambda b, hg, s: (0, 0)),
            pl.BlockSpec((1, HEAD), lambda b, hg, s: (0, 0)),
        ],
        out_specs=pl.BlockSpec((None, rows, g * HEAD), lambda b, hg, s: (b, old(s), hg)),
        out_shape=jax.ShapeDtypeStruct((bsz, seq, v_width), BF16),
        scratch_shapes=[
            pltpu.VMEM((g, HEAD, HEAD), F32),
            pltpu.VMEM((2, cbs, g, CHUNK, HEAD), F32),
            pltpu.VMEM((2, cbs, g, CHUNK, HEAD), BF16),
            pltpu.VMEM((2, cbs, g, CHUNK, HEAD), BF16),
            pltpu.VMEM((2, cbs, g, CHUNK, HEAD), BF16),
            pltpu.VMEM((2, cbs, g // DELTA_PACK, DELTA_PACK * CHUNK, DELTA_PACK * CHUNK), BF16),
            pltpu.VMEM((2, cbs, g, 1, HEAD), F32),
        ],
        compiler_params=_cparams(3),
        name="delta_rule",
    )(qkv, qkv, qkv, z, ba, gate_par, o_gain.reshape(1, HEAD))


def _out_proj_kernel(y_ref, w_ref, x_ref, gate_ref, o_ref):
    o_ref[...] = x_ref[...] + gate_ref[...] * _dot(y_ref[...].astype(BF16), w_ref[...])


def _out_proj(y, w, layer, x, gate):
    bsz, seq, k = y.shape
    d = w.shape[2]
    tm = _pick(seq, (1024, 512, 256, 128))
    tn = _pick(d, (512, 256, 128))
    return pl.pallas_call(
        _out_proj_kernel,
        grid=(bsz, seq // tm, d // tn),
        in_specs=[
            pl.BlockSpec((None, tm, k), lambda b, i, j: (b, i, 0)),
            pl.BlockSpec((None, k, tn), lambda b, i, j: (layer, 0, j)),
            pl.BlockSpec((None, tm, tn), lambda b, i, j: (b, i, j)),
            pl.BlockSpec((None, 1, tn), lambda b, i, j: (b, 0, j)),
        ],
        out_specs=pl.BlockSpec((None, tm, tn), lambda b, i, j: (b, i, j)),
        out_shape=jax.ShapeDtypeStruct((bsz, seq, d), F32),
        compiler_params=_cparams(3),
        name="out_proj",
    )(y, w, x, gate.reshape(bsz, 1, d))


def _block_slabs(dil, res, nb):
    blocks_per_res = ATT_TILE // (BAND * dil)
    slab = PLANE_ROWS // blocks_per_res
    return [((res + dil * a) * PLANE_ROWS + slab * nb, slab) for a in range(PLANES // dil)]


def _block_positions(dil):
    n_slabs = PLANES // dil
    slab = BAND // n_slabs
    return [n_slabs * i + a for a in range(n_slabs) for i in range(slab)]


def _attn_blocks():
    blocks = []
    for gi, (_, dil) in enumerate(DILATION_GROUPS):
        blocks_per_res = ATT_TILE // (BAND * dil)
        for res in range(dil):
            for nb in range(blocks_per_res):
                prev_nb = (nb - 1) % blocks_per_res
                blocks.append((gi, _block_slabs(dil, res, nb), _block_slabs(dil, res, prev_nb), nb == 0))
    return blocks


def _gather_rows(ref, slabs):
    parts = [ref[start:start + size, :] for start, size in slabs]
    return parts[0] if len(parts) == 1 else jnp.concatenate(parts, axis=0)


def _scatter_rows(ref, slabs, val):
    off = 0
    for start, size in slabs:
        ref[start:start + size, :] = val[off:off + size, :].astype(ref.dtype)
        off += size


def _attn_kernel(q0_ref, q1_ref, q2_ref, kp_ref, ko_ref, vp_ref, vo_ref, z_ref, bias_ref, o_ref,
                 m_ref, l_ref, acc_ref):
    first_tile = pl.program_id(1) == 0
    col = lax.broadcasted_iota(jnp.int32, (1, 2 * BAND), 1)
    head_penalty = jnp.where((col < BAND) & first_tile, NEG, 0.0)
    scale = HEAD ** -0.5
    q_refs = (q0_ref, q1_ref, q2_ref)
    last_group = len(DILATION_GROUPS) - 1

    def score_stage(blk):
        gi, q_slabs, prev_slabs, prev_tile = blk
        q16 = _gather_rows(q_refs[gi], q_slabs).astype(BF16)
        k_prev = _gather_rows(kp_ref if prev_tile else ko_ref, prev_slabs)
        k16 = jnp.concatenate([k_prev, _gather_rows(ko_ref, q_slabs)], axis=0).astype(BF16)
        return _dot_nt(q16, k16)

    def softmax_stage(blk, s):
        gi, _, _, prev_tile = blk
        bias = bias_ref[gi]
        if prev_tile:
            bias = bias + head_penalty
        s = s * scale + bias
        m = jnp.max(s, axis=-1, keepdims=True)
        p = jnp.exp(s - m)
        return m, jnp.sum(p, axis=-1, keepdims=True), p.astype(BF16)

    def value_stage(blk, p16):
        _, q_slabs, prev_slabs, prev_tile = blk
        v_prev = _gather_rows(vp_ref if prev_tile else vo_ref, prev_slabs)
        v16 = jnp.concatenate([v_prev, _gather_rows(vo_ref, q_slabs)], axis=0).astype(BF16)
        return _dot(p16, v16)

    def merge_stage(blk, m, l, pv):
        gi, q_slabs, _, _ = blk
        if gi == 0:
            m_new = jnp.broadcast_to(m, (BAND, HEAD))
            l_new = jnp.broadcast_to(l, (BAND, HEAD))
            acc_new = pv
        else:
            m_old = _gather_rows(m_ref, q_slabs)
            m_new = jnp.maximum(m_old, m)
            w_old = jnp.exp(m_old - m_new)
            w_blk = jnp.exp(m - m_new)
            l_new = w_old * _gather_rows(l_ref, q_slabs) + w_blk * l
            acc_new = w_old * _gather_rows(acc_ref, q_slabs) + w_blk * pv
        if gi == last_group:
            _scatter_rows(o_ref, q_slabs, acc_new / l_new * _silu(_gather_rows(z_ref, q_slabs)))
        else:
            _scatter_rows(m_ref, q_slabs, m_new)
            _scatter_rows(l_ref, q_slabs, l_new)
            _scatter_rows(acc_ref, q_slabs, acc_new)

    blocks = _attn_blocks()
    batches = [blocks[i:i + ATT_BATCH] for i in range(0, len(blocks), ATT_BATCH)]
    scores_next = [score_stage(blk) for blk in batches[0]]
    for bi, batch in enumerate(batches):
        scores = scores_next
        if bi + 1 < len(batches):
            scores_next = [score_stage(blk) for blk in batches[bi + 1]]
        stats = [softmax_stage(blk, s) for blk, s in zip(batch, scores)]
        pvs = [value_stage(blk, p16) for blk, (_, _, p16) in zip(batch, stats)]
        for blk, (m, l, _), pv in zip(batch, stats, pvs):
            merge_stage(blk, m, l, pv)


def _dilated_attention(qz, kv, bias):
    bsz, seq, _ = qz.shape
    width = kv.shape[-1] // 2
    heads = width // HEAD
    t = ATT_TILE
    row_tile = lambda b, i, h: (b, i, h)

    def q_spec(gi):
        return pl.BlockSpec((None, t, HEAD), lambda b, i, h: (b, i, gi * heads + h))

    return pl.pallas_call(
        _attn_kernel,
        grid=(bsz, seq // t, heads),
        in_specs=[
            q_spec(0), q_spec(1), q_spec(2),
            pl.BlockSpec((None, t, HEAD), lambda b, i, h: (b, jnp.maximum(i - 1, 0), h)),
            pl.BlockSpec((None, t, HEAD), row_tile),
            pl.BlockSpec((None, t, HEAD), lambda b, i, h: (b, jnp.maximum(i - 1, 0), heads + h)),
            pl.BlockSpec((None, t, HEAD), lambda b, i, h: (b, i, heads + h)),
            pl.BlockSpec((None, t, HEAD), lambda b, i, h: (b, i, 3 * heads + h)),
            pl.BlockSpec((3, None, BAND, 2 * BAND), lambda b, i, h: (0, h, 0, 0)),
        ],
        out_specs=pl.BlockSpec((None, t, HEAD), row_tile),
        out_shape=jax.ShapeDtypeStruct((bsz, seq, width), BF16),
        scratch_shapes=[pltpu.VMEM((t, HEAD), F32)] * 3,
        compiler_params=_cparams(3),
        name="dilated_attention",
    )(qz, qz, qz, kv, kv, kv, kv, qz, bias)


def _t5_bucket(dist):
    max_exact = N_BUCKETS // 2
    n = jnp.maximum(dist, 0)
    large = max_exact + (jnp.log(jnp.maximum(n, 1).astype(F32) / max_exact)
                         / math.log(MAX_DISTANCE / max_exact)
                         * (N_BUCKETS - max_exact)).astype(jnp.int32)
    large = jnp.minimum(large, N_BUCKETS - 1)
    return jnp.where(n < max_exact, n, large)


def _band_bias(rel_bias, heads):
    out = []
    for gi, (_, dil) in enumerate(DILATION_GROUPS):
        pos = jnp.asarray(_block_positions(dil), jnp.int32)
        key_pos = jnp.concatenate([pos, pos + BAND])
        rel = pos[:, None] + BAND - key_pos[None, :]
        valid = (rel >= 0) & (rel <= BAND)
        table = rel_bias[:, gi * heads:(gi + 1) * heads].astype(F32)
        onehot = (_t5_bucket(rel * dil)[None] == jnp.arange(N_BUCKETS)[:, None, None]).astype(F32)
        b = jnp.einsum("nh,nqk->hqk", table, onehot, precision=lax.Precision.HIGHEST)
        out.append(jnp.where(valid[None], b, NEG))
    return jnp.stack(out, axis=0)


def _permute_kernel(x_ref, o_ref):
    d = o_ref.shape[-1]
    for k in range(PLANES_PER_STEP):
        o_ref[k * PLANE_ROWS:(k + 1) * PLANE_ROWS, :] = x_ref[:, k * d:(k + 1) * d]


def _permute_rows(x):
    bsz, seq, d = x.shape
    n_tiles = seq // ATT_TILE
    steps = PLANES // PLANES_PER_STEP
    return pl.pallas_call(
        _permute_kernel,
        grid=(bsz, n_tiles, steps),
        in_specs=[pl.BlockSpec((None, PLANE_ROWS, PLANES_PER_STEP * d), lambda b, t, r: (b, t, r))],
        out_specs=pl.BlockSpec((None, PLANES_PER_STEP * PLANE_ROWS, d), lambda b, t, r: (b, t * steps + r, 0)),
        out_shape=jax.ShapeDtypeStruct((bsz, seq, d), x.dtype),
        compiler_params=_cparams(3),
        name="permute_rows",
    )(x.reshape(bsz, seq // PLANES, PLANES * d))


def _rmsnorm_unpermute_kernel(x_ref, g_ref, o_ref):
    d = x_ref.shape[-1]
    for k in range(PLANES_PER_STEP):
        x = x_ref[k * PLANE_ROWS:(k + 1) * PLANE_ROWS, :]
        ms = jnp.mean(x * x, axis=-1, keepdims=True)
        o_ref[:, k * d:(k + 1) * d] = x * lax.rsqrt(ms + RMS_EPS) * g_ref[...]


def _rmsnorm_unpermute(x, gain):
    bsz, seq, d = x.shape
    n_tiles = seq // ATT_TILE
    steps = PLANES // PLANES_PER_STEP
    out = pl.pallas_call(
        _rmsnorm_unpermute_kernel,
        grid=(bsz, n_tiles, steps),
        in_specs=[pl.BlockSpec((None, PLANES_PER_STEP * PLANE_ROWS, d), lambda b, t, r: (b, t * steps + r, 0)),
                  pl.BlockSpec((1, d), lambda b, t, r: (0, 0))],
        out_specs=pl.BlockSpec((None, PLANE_ROWS, PLANES_PER_STEP * d), lambda b, t, r: (b, t, r)),
        out_shape=jax.ShapeDtypeStruct((bsz, seq // PLANES, PLANES * d), F32),
        compiler_params=_cparams(3),
        name="final_rmsnorm",
    )(x, gain.reshape(1, d))
    return out.reshape(bsz, seq, d)


def kernel(x, c, norm_gain, w_mod, b_mod, w_in_a, conv_w_a, a_log, dt_bias, o_norm_a, w_out_a,
           kv_gain, w_kv_mod, b_kv_mod, w_kv, w_in_b, w_out_b, rel_bias, final_gain):
    bsz, seq, d = x.shape
    depth = norm_gain.shape[0]
    n_a = w_in_a.shape[0]
    qk_width = d
    v_width = w_out_a.shape[1]
    n_vheads = v_width // HEAD
    conv_ch = 2 * qk_width + v_width
    dil_width = w_out_b.shape[1]
    dil_heads = dil_width // HEAD
    assert seq % ATT_TILE == 0 and seq % CHUNK == 0 and 2 * n_vheads <= HEAD

    mods = _mod_proj(c, w_mod, b_mod)
    kv_mod = _mod_proj(c, w_kv_mod[None], b_kv_mod[None])[0]
    bias = _band_bias(rel_bias, dil_heads)

    w_in_a16, w_out_a16 = w_in_a.astype(BF16), w_out_a.astype(BF16)
    w_in_b16, w_out_b16 = w_in_b.astype(BF16), w_out_b.astype(BF16)
    w_ba16 = jnp.pad(w_in_a[:, :, conv_ch + v_width:].astype(BF16), ((0, 0), (0, 0), (0, HEAD - 2 * n_vheads)))

    kv = None
    for layer in range(depth):
        shift, scale, gate = mods[layer, :, :d], mods[layer, :, d:2 * d], mods[layer, :, 2 * d:]
        if layer < n_a:
            qkv = _norm_matmul_conv(x, norm_gain[layer], shift, scale, w_in_a16, layer, conv_w_a[layer], qk_width)
            z = _norm_matmul(x, norm_gain[layer], shift, scale, w_in_a16, layer, col_start=conv_ch, n_cols=v_width)
            ba = _norm_matmul(x, norm_gain[layer], shift, scale, w_ba16, layer)
            gate_par = jnp.zeros((2, HEAD), F32)
            gate_par = gate_par.at[0, n_vheads:2 * n_vheads].set(a_log[layer])
            gate_par = gate_par.at[1, n_vheads:2 * n_vheads].set(dt_bias[layer])
            y = _delta_rule(qkv, z, ba, gate_par, o_norm_a[layer], qk_width)
            x = _out_proj(y, w_out_a16, layer, x, gate)
        else:
            if kv is None:
                x = _permute_rows(x)
                kv = _norm_matmul(x, kv_gain, kv_mod[:, :d], kv_mod[:, d:], w_kv.astype(BF16))
            j = layer - n_a
            qz = _norm_matmul(x, norm_gain[layer], shift, scale, w_in_b16, j)
            y = _dilated_attention(qz, kv, bias)
            x = _out_proj(y, w_out_b16, j, x, gate)
    assert kv is not None
    return _rmsnorm_unpermute(x, final_gain)
```

```python
import functools
import math

import jax
import jax.numpy as jnp
from jax import lax
from jax.experimental import pallas as pl
from jax.experimental.pallas import tpu as pltpu

F32 = jnp.float32
BF16 = jnp.bfloat16

RMS_EPS = 1e-6
L2_EPS = 1e-6
HEAD = 128
BF16_SUBLANES = 16
CHUNK = 64
DELTA_PACK = 4
INV_BASE = 8
CONV_WIDTH = 4
HALO = 8
DILATION_GROUPS =((128, 1), (512, 4), (2048, 16))
BAND = 128
ATT_TILE = 2048
PLANES = 16
PLANE_ROWS = ATT_TILE // PLANES
ATT_BATCH = 4
N_BUCKETS = 32
MAX_DISTANCE = 2048
NEG = -1e30
VMEM_LIMIT = 56 * 1024 * 1024


def _cparams(n_axes):
    return pltpu.CompilerParams(
        dimension_semantics=("arbitrary",) * n_axes, vmem_limit_bytes=VMEM_LIMIT)


def _dot(a, b):
    return jnp.dot(a, b, preferred_element_type=F32)


def _dot_nt(a, b):
    return lax.dot_general(a, b, (((1,), (1,)), ((), ())), preferred_element_type=F32)


def _dot_tn(a, b):
    return lax.dot_general(a, b, (((0,), (0,)), ((), ())), preferred_element_type=F32)


def _sigmoid(x):
    return 1.0 / (1.0 + jnp.exp(-x))


def _silu(x):
    return x * _sigmoid(x)


def _softplus(x):
    return jnp.maximum(x, 0.0) + jnp.log(1.0 + jnp.exp(-jnp.abs(x)))


def _split3(x):
    hi = x.astype(BF16)
    r1 = x - hi.astype(F32)
    mid = r1.astype(BF16)
    lo = (r1 - mid.astype(F32)).astype(BF16)
    return hi, mid, lo


def _pick(n, prefs):
    for p in prefs:
        if n % p == 0:
            return p
    return n


def _mod_kernel(c_ref, w_ref, b_ref, o_ref):
    ca = _silu(c_ref[...])
    w = w_ref[...]
    c_hi, c_mid, c_lo = _split3(ca)
    w_hi = w.astype(BF16)
    w_lo = (w - w_hi.astype(F32)).astype(BF16)
    acc = _dot(c_hi, w_hi) + _dot(c_hi, w_lo) + _dot(c_mid, w_hi) + _dot(c_lo, w_hi)
    o_ref[...] = acc + b_ref[...]


def _mod_proj(c, w, b):
    nl, d, n = w.shape
    n_rows = c.shape[0]
    bsz = -(-n_rows // BF16_SUBLANES) * BF16_SUBLANES
    c = jnp.pad(c, ((0, bsz - n_rows), (0, 0)))
    tn = _pick(n, (512, 256, 128))
    out = pl.pallas_call(
        _mod_kernel,
        grid=(nl, n // tn),
        in_specs=[
            pl.BlockSpec((bsz, d), lambda l, j: (0, 0)),
            pl.BlockSpec((None, d, tn), lambda l, j: (l, 0, j)),
            pl.BlockSpec((None, 1, tn), lambda l, j: (l, 0, j)),
        ],
        out_specs=pl.BlockSpec((None, bsz, tn), lambda l, j: (l, 0, j)),
        out_shape=jax.ShapeDtypeStruct((nl, bsz, n), F32),
        compiler_params=_cparams(2),
        name="mod_proj",
    )(c, w, b.reshape(nl, 1, n))
    return out[:, :n_rows]


def _modnorm(x, gain, shift, scale):
    ms = jnp.mean(x * x, axis=-1, keepdims=True)
    return x * lax.rsqrt(ms + RMS_EPS) * (gain * (1.0 + scale)) + shift


def _nmm_plain_kernel(x_ref, g_ref, sh_ref, sc_ref, w_ref, o_ref, h_ref):
    @pl.when(pl.program_id(2) == 0)
    def _():
        h_ref[...] = _modnorm(x_ref[...], g_ref[...], sh_ref[...], sc_ref[...]).astype(BF16)

    o_ref[...] = _dot(h_ref[...], w_ref[...]).astype(o_ref.dtype)


def _norm_matmul(x, gain, shift, scale, w, layer=0, col_start=0, n_cols=None, out_dtype=F32):
    bsz, seq, d = x.shape
    w = w[None] if w.ndim == 2 else w
    n = w.shape[2] - col_start if n_cols is None else n_cols
    tm = _pick(seq, (1024, 512, 256, 128))
    tn = _pick(math.gcd(n, col_start) if col_start else n, (512, 256, 128))
    first = col_start // tn
    return pl.pallas_call(
        _nmm_plain_kernel,
        grid=(bsz, seq // tm, n // tn),
        in_specs=[
            pl.BlockSpec((None, tm, d), lambda b, i, j: (b, i, 0)),
            pl.BlockSpec((1, d), lambda b, i, j: (0, 0)),
            pl.BlockSpec((None, 1, d), lambda b, i, j: (b, 0, 0)),
            pl.BlockSpec((None, 1, d), lambda b, i, j: (b, 0, 0)),
            pl.BlockSpec((None, d, tn), lambda b, i, j: (layer, 0, first + j)),
        ],
        out_specs=pl.BlockSpec((None, tm, tn), lambda b, i, j: (b, i, j)),
        out_shape=jax.ShapeDtypeStruct((bsz, seq, n), out_dtype),
        scratch_shapes=[pltpu.VMEM((tm, d), BF16)],
        compiler_params=_cparams(3),
        name="norm_matmul",
    )(x, gain.reshape(1, d), shift.reshape(bsz, 1, d), scale.reshape(bsz, 1, d), w)


def _nmm_conv_kernel(x_ref, g_ref, sh_ref, sc_ref, w_ref, cw_ref, o_ref, h_ref, halo_ref, acc0_ref, acc1_ref,
                     *, n_q_tiles, n_qk_tiles):
    i = pl.program_id(1)
    j = pl.program_id(2)
    tm, tn = o_ref.shape
    jt = jnp.maximum(j - 1, 0)

    @pl.when(j == 0)
    def _():
        h_ref[...] = _modnorm(x_ref[...], g_ref[...], sh_ref[...], sc_ref[...]).astype(BF16)
        acc1_ref[...] = jnp.zeros(acc1_ref.shape, F32)

    @pl.when((i == 0) & (j == 0))
    def _():
        halo_ref[...] = jnp.zeros(halo_ref.shape, F32)

    def step(done_ref, next_ref):
        prev_rows = halo_ref[jt]
        done_ref[0:HALO, :] = prev_rows
        acc = done_ref[HALO:, :]
        halo_ref[jt] = jnp.where(j > 0, acc[tm - HALO:, :], prev_rows)
        cw = cw_ref[...]
        y = cw[3:4, :] * acc
        for back in (1, 2, 3):
            y = y + cw[3 - back:4 - back, :] * done_ref[pl.ds(HALO - back, tm), :]
        y = _silu(y)
        is_qk = jt < n_qk_tiles
        qscale = jnp.where(jt < n_q_tiles, HEAD ** -0.5, 1.0).astype(F32)
        for hh in range(tn // HEAD):
            yh = y[:, hh * HEAD:(hh + 1) * HEAD]
            ss = jnp.sum(yh * yh, axis=-1, keepdims=True)
            o_ref[:, hh * HEAD:(hh + 1) * HEAD] = yh * jnp.where(is_qk, lax.rsqrt(ss + L2_EPS) * qscale, 1.0)
        next_ref[HALO:, :] = _dot(h_ref[...], w_ref[...])

    @pl.when(j % 2 == 0)
    def _():
        step(acc1_ref, acc0_ref)

    @pl.when(j % 2 == 1)
    def _():
        step(acc0_ref, acc1_ref)


def _norm_matmul_conv(x, gain, shift, scale, w, layer, conv_w, qk_width):
    bsz, seq, d = x.shape
    n = conv_w.shape[1]
    tm = _pick(seq, (1024, 512, 256, 128))
    tn = _pick(qk_width, (512, 256, 128))
    n_tiles = n // tn
    kern = functools.partial(_nmm_conv_kernel, n_q_tiles=qk_width // tn, n_qk_tiles=2 * qk_width // tn)
    cur = lambda j: jnp.minimum(j, n_tiles - 1)
    old = lambda j: jnp.maximum(j - 1, 0)
    return pl.pallas_call(
        kern,
        grid=(bsz, seq // tm, n_tiles + 1),
        in_specs=[
            pl.BlockSpec((None, tm, d), lambda b, i, j: (b, i, 0)),
            pl.BlockSpec((1, d), lambda b, i, j: (0, 0)),
            pl.BlockSpec((None, 1, d), lambda b, i, j: (b, 0, 0)),
            pl.BlockSpec((None, 1, d), lambda b, i, j: (b, 0, 0)),
            pl.BlockSpec((None, d, tn), lambda b, i, j: (layer, 0, cur(j))),
            pl.BlockSpec((CONV_WIDTH, tn), lambda b, i, j: (0, old(j))),
        ],
        out_specs=pl.BlockSpec((None, tm, tn), lambda b, i, j: (b, i, old(j))),
        out_shape=jax.ShapeDtypeStruct((bsz, seq, n), F32),
        scratch_shapes=[pltpu.VMEM((tm, d), BF16), pltpu.VMEM((n_tiles, HALO, tn), F32),
                        pltpu.VMEM((HALO + tm, tn), F32), pltpu.VMEM((HALO + tm, tn), F32)],
        compiler_params=_cparams(3),
        name="norm_matmul_conv",
    )(x, gain.reshape(1, d), shift.reshape(bsz, 1, d), scale.reshape(bsz, 1, d), w, conv_w)


def _delta_kernel(q_ref, k_ref, v_ref, z_ref, ba_ref, par_ref, og_ref, o_ref,
                  state_ref, u_ref, w_ref, qd_ref, kd_ref, at_ref, a_ref,
                  *, heads_per_step, chunks_per_step, n_vheads):
    s = pl.program_id(2)
    hg = pl.program_id(1)
    g_heads = heads_per_step
    c = CHUNK
    slot = s % 2
    prev = 1 - slot

    @pl.when(s == 0)
    def _():
        state_ref[...] = jnp.zeros(state_ref.shape, F32)
        u_ref[1] = jnp.zeros(u_ref.shape[1:], F32)
        w_ref[1] = jnp.zeros(w_ref.shape[1:], BF16)
        qd_ref[1] = jnp.zeros(qd_ref.shape[1:], BF16)
        kd_ref[1] = jnp.zeros(kd_ref.shape[1:], BF16)
        at_ref[1] = jnp.zeros(at_ref.shape[1:], BF16)
        a_ref[1] = jnp.zeros(a_ref.shape[1:], F32)

    row = lax.broadcasted_iota(jnp.int32, (c, c), 0)
    col = lax.broadcasted_iota(jnp.int32, (c, c), 1)
    lower = row >= col
    strict = row > col
    eye = row == col
    lane = lax.broadcasted_iota(jnp.int32, (c, HEAD), 1)
    tril = jnp.where(lower, 1.0, 0.0).astype(BF16)

    items = [(cb, h) for cb in range(chunks_per_step) for h in range(g_heads)]
    rows_of = lambda cb: slice(cb * c, (cb + 1) * c)
    cols_of = lambda i: slice(i * HEAD, (i + 1) * HEAD)

    pcols = DELTA_PACK * c
    prow = lax.broadcasted_iota(jnp.int32, (c, pcols), 0)
    plane = lax.broadcasted_iota(jnp.int32, (c, pcols), 1)
    pcol = jnp.bitwise_and(plane, c - 1)
    pblk = jnp.right_shift(plane, c.bit_length() - 1)
    p_lower = prow >= pcol
    p_strict = prow > pcol
    p_eye = prow == pcol
    in_block = [pblk == i for i in range(DELTA_PACK)]
    packs = [(cb, qd) for cb in range(chunks_per_step) for qd in range(g_heads // DELTA_PACK)]
    heads_of = lambda qd: [DELTA_PACK * qd + i for i in range(DELTA_PACK)]

    def pack_columns(cols):
        out = cols[-1]
        for i in reversed(range(DELTA_PACK - 1)):
            out = jnp.where(in_block[i], cols[i], out)
        return out

    def only_block(mat, i):
        return jnp.where(in_block[i], mat, 0.0).astype(BF16)

    def block_diag(mat):
        return jnp.concatenate([only_block(mat, i) for i in range(DELTA_PACK)], axis=0)

    def same_block(size):
        shift = size.bit_length() - 1
        return jnp.right_shift(prow, shift) == jnp.right_shift(pcol, shift)

    comb, kk, qk = {}, {}, {}
    xm, rhs, fac, lfull = {}, {}, {}, {}

    def gates_stage():
        for cb in range(chunks_per_step):
            ba = ba_ref[rows_of(cb), :]
            beta_full = _sigmoid(ba)
            g_full = -jnp.exp(par_ref[0:1, :]) * _softplus(ba + par_ref[1:2, :])
            g_hi, g_mid, g_lo = _split3(g_full)
            gc_full = _dot(tril, g_hi) + _dot(tril, g_mid) + _dot(tril, g_lo)
            comb[cb] = jnp.where(lane < n_vheads, beta_full, gc_full)

    def gram_stage():
        for cb in range(chunks_per_step):
            for kh_idx in range(g_heads // 2):
                k16 = k_ref[rows_of(cb), cols_of(kh_idx)].astype(BF16)
                q16 = q_ref[rows_of(cb), cols_of(kh_idx)].astype(BF16)
                k_twice = jnp.concatenate([k16, k16], axis=0)
                both = _dot_nt(jnp.concatenate([k16, q16], axis=0), k_twice)
                kk[cb, kh_idx] = both[:c, :]
                qk[cb, kh_idx] = both[c:, :]

    def decay_stage():
        for cb, qd in packs:
            betas, gcs = [], []
            for h in heads_of(qd):
                head = hg * g_heads + h
                kh = k_ref[rows_of(cb), cols_of(h // 2)]
                qh = q_ref[rows_of(cb), cols_of(h // 2)]
                vh = v_ref[rows_of(cb), cols_of(h)]
                beta_c = jnp.sum(jnp.where(lane == head, comb[cb], 0.0), axis=-1, keepdims=True)
                gc_c = jnp.sum(jnp.where(lane == head + n_vheads, comb[cb], 0.0), axis=-1, keepdims=True)
                gl = gc_c[c - 1:c, :]
                eg = jnp.exp(gc_c)
                rhs[cb, h] = jnp.concatenate([vh * beta_c, kh * (beta_c * eg)], axis=1)
                qd_ref[slot, cb, h] = (qh * eg).astype(BF16)
                kd_ref[slot, cb, h] = (kh * jnp.exp(gl - gc_c)).astype(BF16)
                a_ref[slot, cb, h] = jnp.broadcast_to(jnp.exp(gl), (1, HEAD))
                betas.append(beta_c)
                gcs.append(gc_c)
            first_kh = DELTA_PACK * qd // 2
            kk_p = jnp.concatenate([kk[cb, first_kh + i] for i in range(DELTA_PACK // 2)], axis=1)
            qk_p = jnp.concatenate([qk[cb, first_kh + i] for i in range(DELTA_PACK // 2)], axis=1)
            gc_p = pack_columns(gcs)
            gc_r = jnp.sum(jnp.where(p_eye, gc_p, 0.0), axis=0, keepdims=True)
            decay = jnp.exp(jnp.where(p_lower, gc_p - gc_r, NEG))
            lmat = jnp.where(p_strict, kk_p * pack_columns(betas) * decay, 0.0)
            at_ref[slot, cb, qd] = block_diag(qk_p * decay)
            lfull[cb, qd] = lmat
            ldiag = jnp.where(same_block(INV_BASE), lmat, 0.0)
            xm[cb, qd] = -ldiag
            fac[cb, qd] = ldiag

    def base_square_stage():
        for pk in packs:
            fac[pk] = _dot(fac[pk].astype(BF16), block_diag(fac[pk]))

    def base_round_stage():
        for pk in packs:
            both = jnp.concatenate([fac[pk].astype(BF16), xm[pk].astype(BF16)], axis=0)
            out = _dot(both, block_diag(fac[pk]))
            xm[pk] = xm[pk] + fac[pk] + out[c:, :]
            fac[pk] = out[:c, :]

    def base_last_stage():
        for pk in packs:
            xm[pk] = xm[pk] + fac[pk] + _dot(xm[pk].astype(BF16), block_diag(fac[pk]))

    def join_right_stage(size):
        def run():
            for pk in packs:
                joint = jnp.where(same_block(2 * size) & ~same_block(size), lfull[pk], 0.0)
                fac[pk] = joint + _dot(joint.astype(BF16), block_diag(xm[pk]))
        return run

    def join_left_stage():
        for pk in packs:
            xm[pk] = xm[pk] - fac[pk] - _dot(xm[pk].astype(BF16), block_diag(fac[pk]))

    def solve_stage():
        for cb, qd in packs:
            rhs_rows = jnp.concatenate([rhs[cb, h].astype(BF16) for h in heads_of(qd)], axis=0)
            sol = _dot(block_diag(xm[cb, qd]), rhs_rows)
            for i, h in enumerate(heads_of(qd)):
                uw = rhs[cb, h] + sol[i * c:(i + 1) * c, :]
                u_ref[slot, cb, h] = uw[:, :HEAD]
                w_ref[slot, cb, h] = uw[:, HEAD:].astype(BF16)

    factor_stages = [gates_stage, gram_stage, decay_stage, base_square_stage, base_round_stage, base_last_stage]
    size = INV_BASE
    while size < c:
        factor_stages += [join_right_stage(size), join_left_stage]
        size *= 2
    factor_stages.append(solve_stage)

    og = og_ref[...]
    states = [state_ref[h] for h in range(g_heads)]
    wqs, v16 = {}, {}

    def read_stage(cb):
        def run():
            for h in range(g_heads):
                wq = jnp.concatenate([w_ref[prev, cb, h], qd_ref[prev, cb, h]], axis=0)
                wqs[h] = _dot(wq, states[h].astype(BF16))
        return run

    def update_stage(cb):
        def run():
            for h in range(g_heads):
                v16[h] = (u_ref[prev, cb, h] - wqs[h][:c, :]).astype(BF16)
            intra = {}
            for qd in range(g_heads // DELTA_PACK):
                v_rows = jnp.concatenate([v16[h] for h in heads_of(qd)], axis=0)
                av = _dot(at_ref[prev, cb, qd], v_rows)
                for i, h in enumerate(heads_of(qd)):
                    intra[h] = av[i * c:(i + 1) * c, :]
            for h in range(g_heads):
                o = wqs[h][c:, :] + intra[h]
                states[h] = states[h] * a_ref[prev, cb, h] + _dot_tn(kd_ref[prev, cb, h], v16[h])
                ms = jnp.mean(o * o, axis=-1, keepdims=True)
                zh = z_ref[rows_of(cb), cols_of(h)]
                o_ref[rows_of(cb), cols_of(h)] = (o * lax.rsqrt(ms + RMS_EPS) * og * _silu(zh)).astype(o_ref.dtype)
        return run

    state_stages = []
    for cb in range(chunks_per_step):
        state_stages += [read_stage(cb), update_stage(cb)]

    stride = max(1, len(factor_stages) // (len(state_stages) + 1))
    for idx, stage in enumerate(factor_stages):
        stage()
        if idx % stride == stride - 1 and state_stages:
            state_stages.pop(0)()
    for stage in state_stages:
        stage()
    for h in range(g_heads):
        state_ref[h] = states[h]


def _delta_rule(qkv, z, ba, gate_par, o_gain, qk_width, heads_per_step=8, chunks_per_step=4):
    bsz, seq, _ = qkv.shape
    v_width = z.shape[-1]
    n_vheads = v_width // HEAD
    g = min(heads_per_step, n_vheads)
    assert g % DELTA_PACK == 0 and n_vheads % g == 0
    gk = g // 2
    cbs = chunks_per_step
    rows = cbs * CHUNK
    n_blocks = seq // rows
    n_kblocks = qk_width // (gk * HEAD)
    kern = functools.partial(_delta_kernel, heads_per_step=g, chunks_per_step=cbs, n_vheads=n_vheads)
    cur = lambda s: jnp.minimum(s, n_blocks - 1)
    old = lambda s: jnp.maximum(s - 1, 0)
    return pl.pallas_call(
        kern,
        grid=(bsz, n_vheads // g, n_blocks + 1),
        in_specs=[
            pl.BlockSpec((None, rows, gk * HEAD), lambda b, hg, s: (b, cur(s), hg)),
            pl.BlockSpec((None, rows, gk * HEAD), lambda b, hg, s: (b, cur(s), n_kblocks + hg)),
            pl.BlockSpec((None, rows, g * HEAD), lambda b, hg, s: (b, cur(s), 2 * qk_width // (g * HEAD) + hg)),
            pl.BlockSpec((None, rows, g * HEAD), lambda b, hg, s: (b, old(s), hg)),
            pl.BlockSpec((None, rows, HEAD), lambda b, hg, s: (b, cur(s), 0)),
            pl.BlockSpec((2, HEAD), lambda b, hg, s: (0, 0)),
            pl.BlockSpec((1, HEAD), lambda b, hg, s: (0, 0)),
        ],
        out_specs=pl.BlockSpec((None, rows, g * HEAD), lambda b, hg, s: (b, old(s), hg)),
        out_shape=jax.ShapeDtypeStruct((bsz, seq, v_width), BF16),
        scratch_shapes=[
            pltpu.VMEM((g, HEAD, HEAD), F32),
            pltpu.VMEM((2, cbs, g, CHUNK, HEAD), F32),
            pltpu.VMEM((2, cbs, g, CHUNK, HEAD), BF16),
            pltpu.VMEM((2, cbs, g, CHUNK, HEAD), BF16),
            pltpu.VMEM((2, cbs, g, CHUNK, HEAD), BF16),
            pltpu.VMEM((2, cbs, g // DELTA_PACK, DELTA_PACK * CHUNK, DELTA_PACK * CHUNK), BF16),
            pltpu.VMEM((2, cbs, g, 1, HEAD), F32),
        ],
        compiler_params=_cparams(3),
        name="delta_rule",
    )(qkv, qkv, qkv, z, ba, gate_par, o_gain.reshape(1, HEAD))


def _out_proj_kernel(y_ref, w_ref, x_ref, gate_ref, o_ref):
    o_ref[...] = x_ref[...] + gate_ref[...] * _dot(y_ref[...].astype(BF16), w_ref[...])


def _out_proj_norm_kernel(y_ref, w_ref, x_ref, gate_ref, g_ref, o_ref):
    x = x_ref[...] + gate_ref[...] * _dot(y_ref[...].astype(BF16), w_ref[...])
    ms = jnp.mean(x * x, axis=-1, keepdims=True)
    o_ref[...] = x * lax.rsqrt(ms + RMS_EPS) * g_ref[...]


def _out_proj(y, w, layer, x, gate, final_gain=None):
    bsz, seq, k = y.shape
    d = w.shape[2]
    fused_norm = final_gain is not None
    tm = _pick(seq, (512, 256, 128)) if fused_norm else _pick(seq, (1024, 512, 256, 128))
    tn = d if fused_norm else _pick(d, (512, 256, 128))
    in_specs = [
        pl.BlockSpec((None, tm, k), lambda b, i, j: (b, i, 0)),
        pl.BlockSpec((None, k, tn), lambda b, i, j: (layer, 0, j)),
        pl.BlockSpec((None, tm, tn), lambda b, i, j: (b, i, j)),
        pl.BlockSpec((None, 1, tn), lambda b, i, j: (b, 0, j)),
    ]
    args = [y, w, x, gate.reshape(bsz, 1, d)]
    if fused_norm:
        in_specs.append(pl.BlockSpec((1, d), lambda b, i, j: (0, 0)))
        args.append(final_gain.reshape(1, d))
    return pl.pallas_call(
        _out_proj_norm_kernel if fused_norm else _out_proj_kernel,
        grid=(bsz, seq // tm, d // tn),
        in_specs=in_specs,
        out_specs=pl.BlockSpec((None, tm, tn), lambda b, i, j: (b, i, j)),
        out_shape=jax.ShapeDtypeStruct((bsz, seq, d), F32),
        compiler_params=_cparams(3),
        name="out_proj_norm" if fused_norm else "out_proj",
    )(*args)


def _block_slabs(dil, res, nb):
    blocks_per_res = ATT_TILE // (BAND * dil)
    slab = PLANE_ROWS // blocks_per_res
    return [((res + dil * a) * PLANE_ROWS + slab * nb, slab) for a in range(PLANES // dil)]


def _block_positions(dil):
    n_slabs = PLANES // dil
    slab = BAND // n_slabs
    return [n_slabs * i + a for a in range(n_slabs) for i in range(slab)]


def _attn_blocks():
    blocks = []
    for gi, (_, dil) in enumerate(DILATION_GROUPS):
        blocks_per_res = ATT_TILE // (BAND * dil)
        for res in range(dil):
            for nb in range(blocks_per_res):
                prev_nb = (nb - 1) % blocks_per_res
                blocks.append((gi, _block_slabs(dil, res, nb), _block_slabs(dil, res, prev_nb), nb == 0))
    return blocks


def _gather_rows(ref, slabs):
    parts = [ref[start:start + size, :] for start, size in slabs]
    return parts[0] if len(parts) == 1 else jnp.concatenate(parts, axis=0)


def _scatter_rows(ref, slabs, val):
    off = 0
    for start, size in slabs:
        ref[start:start + size, :] = val[off:off + size, :].astype(ref.dtype)
        off += size


def _attn_kernel(q0_ref, q1_ref, q2_ref, kp_ref, ko_ref, vp_ref, vo_ref, z_ref, bias_ref, o_ref,
                 m_ref, l_ref, acc_ref):
    first_tile = pl.program_id(1) == 0
    col = lax.broadcasted_iota(jnp.int32, (1, 2 * BAND), 1)
    head_penalty = jnp.where((col < BAND) & first_tile, NEG, 0.0)
    scale = HEAD ** -0.5
    q_refs = (q0_ref, q1_ref, q2_ref)
    last_group = len(DILATION_GROUPS) - 1

    def score_stage(blk):
        gi, q_slabs, prev_slabs, prev_tile = blk
        q16 = _gather_rows(q_refs[gi], q_slabs).astype(BF16)
        k_prev = _gather_rows(kp_ref if prev_tile else ko_ref, prev_slabs)
        k16 = jnp.concatenate([k_prev, _gather_rows(ko_ref, q_slabs)], axis=0).astype(BF16)
        return _dot_nt(q16, k16)

    def softmax_stage(blk, s):
        gi, _, _, prev_tile = blk
        bias = bias_ref[gi]
        if prev_tile:
            bias = bias + head_penalty
        s = s * scale + bias
        m = jnp.max(s, axis=-1, keepdims=True)
        p = jnp.exp(s - m)
        return m, jnp.sum(p, axis=-1, keepdims=True), p.astype(BF16)

    def value_stage(blk, p16):
        _, q_slabs, prev_slabs, prev_tile = blk
        v_prev = _gather_rows(vp_ref if prev_tile else vo_ref, prev_slabs)
        v16 = jnp.concatenate([v_prev, _gather_rows(vo_ref, q_slabs)], axis=0).astype(BF16)
        return _dot(p16, v16)

    def merge_stage(blk, m, l, pv):
        gi, q_slabs, _, _ = blk
        if gi == 0:
            m_new = jnp.broadcast_to(m, (BAND, HEAD))
            l_new = jnp.broadcast_to(l, (BAND, HEAD))
            acc_new = pv
        else:
            m_old = _gather_rows(m_ref, q_slabs)
            m_new = jnp.maximum(m_old, m)
            w_old = jnp.exp(m_old - m_new)
            w_blk = jnp.exp(m - m_new)
            l_new = w_old * _gather_rows(l_ref, q_slabs) + w_blk * l
            acc_new = w_old * _gather_rows(acc_ref, q_slabs) + w_blk * pv
        if gi == last_group:
            _scatter_rows(o_ref, q_slabs, acc_new / l_new * _silu(_gather_rows(z_ref, q_slabs)))
        else:
            _scatter_rows(m_ref, q_slabs, m_new)
            _scatter_rows(l_ref, q_slabs, l_new)
            _scatter_rows(acc_ref, q_slabs, acc_new)

    blocks = _attn_blocks()
    batches = [blocks[i:i + ATT_BATCH] for i in range(0, len(blocks), ATT_BATCH)]
    scores_next = [score_stage(blk) for blk in batches[0]]
    for bi, batch in enumerate(batches):
        scores = scores_next
        if bi + 1 < len(batches):
            scores_next = [score_stage(blk) for blk in batches[bi + 1]]
        stats = [softmax_stage(blk, s) for blk, s in zip(batch, scores)]
        pvs = [value_stage(blk, p16) for blk, (_, _, p16) in zip(batch, stats)]
        for blk, (m, l, _), pv in zip(batch, stats, pvs):
            merge_stage(blk, m, l, pv)


def _dilated_attention(qz, kv, bias):
    bsz, seq, _ = qz.shape
    width = kv.shape[-1] // 2
    heads = width // HEAD
    t = ATT_TILE
    row_tile = lambda b, i, h: (b, i, h)

    def q_spec(gi):
        return pl.BlockSpec((None, t, HEAD), lambda b, i, h: (b, i, gi * heads + h))

    return pl.pallas_call(
        _attn_kernel,
        grid=(bsz, seq // t, heads),
        in_specs=[
            q_spec(0), q_spec(1), q_spec(2),
            pl.BlockSpec((None, t, HEAD), lambda b, i, h: (b, jnp.maximum(i - 1, 0), h)),
            pl.BlockSpec((None, t, HEAD), row_tile),
            pl.BlockSpec((None, t, HEAD), lambda b, i, h: (b, jnp.maximum(i - 1, 0), heads + h)),
            pl.BlockSpec((None, t, HEAD), lambda b, i, h: (b, i, heads + h)),
            pl.BlockSpec((None, t, HEAD), lambda b, i, h: (b, i, 3 * heads + h)),
            pl.BlockSpec((3, None, BAND, 2 * BAND), lambda b, i, h: (0, h, 0, 0)),
        ],
        out_specs=pl.BlockSpec((None, t, HEAD), row_tile),
        out_shape=jax.ShapeDtypeStruct((bsz, seq, width), BF16),
        scratch_shapes=[pltpu.VMEM((t, HEAD), F32)] * 3,
        compiler_params=_cparams(3),
        name="dilated_attention",
    )(qz, qz, qz, kv, kv, kv, kv, qz, bias)


def _t5_bucket(dist):
    max_exact = N_BUCKETS // 2
    n = jnp.maximum(dist, 0)
    large = max_exact + (jnp.log(jnp.maximum(n, 1).astype(F32) / max_exact)
                         / math.log(MAX_DISTANCE / max_exact)
                         * (N_BUCKETS - max_exact)).astype(jnp.int32)
    large = jnp.minimum(large, N_BUCKETS - 1)
    return jnp.where(n < max_exact, n, large)


def _band_bias(rel_bias, heads):
    out = []
    for gi, (_, dil) in enumerate(DILATION_GROUPS):
        pos = jnp.asarray(_block_positions(dil), jnp.int32)
        key_pos = jnp.concatenate([pos, pos + BAND])
        rel = pos[:, None] + BAND - key_pos[None, :]
        valid = (rel >= 0) & (rel <= BAND)
        table = rel_bias[:, gi * heads:(gi + 1) * heads].astype(F32)
        onehot = (_t5_bucket(rel * dil)[None] == jnp.arange(N_BUCKETS)[:, None, None]).astype(F32)
        b = jnp.einsum("nh,nqk->hqk", table, onehot, precision=lax.Precision.HIGHEST)
        out.append(jnp.where(valid[None], b, NEG))
    return jnp.stack(out, axis=0)


def _permute_kernel(x_ref, o_ref):
    for r in range(PLANES):
        o_ref[r * PLANE_ROWS:(r + 1) * PLANE_ROWS, :] = x_ref[pl.ds(r, PLANE_ROWS, stride=PLANES), :]


def _unpermute_kernel(x_ref, o_ref):
    for r in range(PLANES):
        o_ref[pl.ds(r, PLANE_ROWS, stride=PLANES), :] = x_ref[r * PLANE_ROWS:(r + 1) * PLANE_ROWS, :]


def _move_rows(x, body, name):
    bsz, seq, d = x.shape
    spec = pl.BlockSpec((None, ATT_TILE, HEAD), lambda b, t, j: (b, t, j))
    return pl.pallas_call(
        body,
        grid=(bsz, seq // ATT_TILE, d // HEAD),
        in_specs=[spec],
        out_specs=spec,
        out_shape=jax.ShapeDtypeStruct((bsz, seq, d), x.dtype),
        compiler_params=_cparams(3),
        name=name,
    )(x)


def kernel(x, c, norm_gain, w_mod, b_mod, w_in_a, conv_w_a, a_log, dt_bias, o_norm_a, w_out_a,
           kv_gain, w_kv_mod, b_kv_mod, w_kv, w_in_b, w_out_b, rel_bias, final_gain):
    bsz, seq, d = x.shape
    depth = norm_gain.shape[0]
    n_a = w_in_a.shape[0]
    qk_width = d
    v_width = w_out_a.shape[1]
    n_vheads = v_width // HEAD
    conv_ch = 2 * qk_width + v_width
    dil_width = w_out_b.shape[1]
    dil_heads = dil_width // HEAD
    assert seq % ATT_TILE == 0 and seq % CHUNK == 0 and 2 * n_vheads <= HEAD

    mods = _mod_proj(c, w_mod, b_mod)
    kv_mod = _mod_proj(c, w_kv_mod[None], b_kv_mod[None])[0]
    bias = _band_bias(rel_bias, dil_heads)

    w_in_a16, w_out_a16 = w_in_a.astype(BF16), w_out_a.astype(BF16)
    w_in_b16, w_out_b16 = w_in_b.astype(BF16), w_out_b.astype(BF16)
    w_ba16 = jnp.pad(w_in_a[:, :, conv_ch + v_width:].astype(BF16), ((0, 0), (0, 0), (0, HEAD - 2 * n_vheads)))

    kv = None
    for layer in range(depth):
        shift, scale, gate = mods[layer, :, :d], mods[layer, :, d:2 * d], mods[layer, :, 2 * d:]
        if layer < n_a:
            qkv = _norm_matmul_conv(x, norm_gain[layer], shift, scale, w_in_a16, layer, conv_w_a[layer], qk_width)
            z = _norm_matmul(x, norm_gain[layer], shift, scale, w_in_a16, layer, col_start=conv_ch, n_cols=v_width)
            ba = _norm_matmul(x, norm_gain[layer], shift, scale, w_ba16, layer)
            gate_par = jnp.zeros((2, HEAD), F32)
            gate_par = gate_par.at[0, n_vheads:2 * n_vheads].set(a_log[layer])
            gate_par = gate_par.at[1, n_vheads:2 * n_vheads].set(dt_bias[layer])
            y = _delta_rule(qkv, z, ba, gate_par, o_norm_a[layer], qk_width)
            x = _out_proj(y, w_out_a16, layer, x, gate)
        else:
            if kv is None:
                x = _move_rows(x, _permute_kernel, "permute_rows")
                kv = _norm_matmul(x, kv_gain, kv_mod[:, :d], kv_mod[:, d:], w_kv.astype(BF16))
            j = layer - n_a
            qz = _norm_matmul(x, norm_gain[layer], shift, scale, w_in_b16, j)
            y = _dilated_attention(qz, kv, bias)
            x = _out_proj(y, w_out_b16, j, x, gate, final_gain if layer == depth - 1 else None)
    assert kv is not None and depth > n_a
    return _move_rows(x, _unpermute_kernel, "unpermute_rows")
```

```python
import functools
import math

import jax
import jax.numpy as jnp
from jax import lax
from jax.experimental import pallas as pl
from jax.experimental.pallas import tpu as pltpu

F32 = jnp.float32
BF16 = jnp.bfloat16

RMS_EPS = 1e-6
L2_EPS = 1e-6
HEAD = 128
BF16_SUBLANES = 16
CHUNK = 64
DELTA_PACK = 4
INV_BASE = 8
CONV_WIDTH = 4
HALO = 8
DILATION_GROUPS =((128, 1), (512, 4), (2048, 16))
BAND = 128
ATT_TILE = 2048
PLANES = 16
PLANE_ROWS = ATT_TILE // PLANES
ATT_BATCH = 4
N_BUCKETS = 32
MAX_DISTANCE = 2048
NEG = -1e30
LOG2_E = math.log2(math.e)
VMEM_LIMIT = 56 * 1024 * 1024


def _cparams(n_axes):
    return pltpu.CompilerParams(
        dimension_semantics=("arbitrary",) * n_axes, vmem_limit_bytes=VMEM_LIMIT)


def _dot(a, b):
    return jnp.dot(a, b, preferred_element_type=F32)


def _dot_nt(a, b):
    return lax.dot_general(a, b, (((1,), (1,)), ((), ())), preferred_element_type=F32)


def _dot_tn(a, b):
    return lax.dot_general(a, b, (((0,), (0,)), ((), ())), preferred_element_type=F32)


def _sigmoid(x):
    return 1.0 / (1.0 + jnp.exp(-x))


def _silu(x):
    return x * _sigmoid(x)


def _softplus(x):
    return jnp.maximum(x, 0.0) + jnp.log(1.0 + jnp.exp(-jnp.abs(x)))


def _split3(x):
    hi = x.astype(BF16)
    r1 = x - hi.astype(F32)
    mid = r1.astype(BF16)
    lo = (r1 - mid.astype(F32)).astype(BF16)
    return hi, mid, lo


def _pick(n, prefs):
    for p in prefs:
        if n % p == 0:
            return p
    return n


def _mod_kernel(c_ref, w_ref, b_ref, o_ref):
    ca = _silu(c_ref[...])
    w = w_ref[...]
    c_hi, c_mid, c_lo = _split3(ca)
    w_hi = w.astype(BF16)
    w_lo = (w - w_hi.astype(F32)).astype(BF16)
    acc = _dot(c_hi, w_hi) + _dot(c_hi, w_lo) + _dot(c_mid, w_hi) + _dot(c_lo, w_hi)
    o_ref[...] = acc + b_ref[...]


def _mod_proj(c, w, b):
    nl, d, n = w.shape
    n_rows = c.shape[0]
    bsz = -(-n_rows // BF16_SUBLANES) * BF16_SUBLANES
    c = jnp.pad(c, ((0, bsz - n_rows), (0, 0)))
    tn = _pick(n, (512, 256, 128))
    out = pl.pallas_call(
        _mod_kernel,
        grid=(nl, n // tn),
        in_specs=[
            pl.BlockSpec((bsz, d), lambda l, j: (0, 0)),
            pl.BlockSpec((None, d, tn), lambda l, j: (l, 0, j)),
            pl.BlockSpec((None, 1, tn), lambda l, j: (l, 0, j)),
        ],
        out_specs=pl.BlockSpec((None, bsz, tn), lambda l, j: (l, 0, j)),
        out_shape=jax.ShapeDtypeStruct((nl, bsz, n), F32),
        compiler_params=_cparams(2),
        name="mod_proj",
    )(c, w, b.reshape(nl, 1, n))
    return out[:, :n_rows]


def _modnorm(x, gain, shift, scale):
    ms = jnp.mean(x * x, axis=-1, keepdims=True)
    return x * lax.rsqrt(ms + RMS_EPS) * (gain * (1.0 + scale)) + shift


def _nmm_plain_kernel(x_ref, g_ref, sh_ref, sc_ref, w_ref, o_ref, h_ref):
    @pl.when(pl.program_id(2) == 0)
    def _():
        h_ref[...] = _modnorm(x_ref[...], g_ref[...], sh_ref[...], sc_ref[...]).astype(BF16)

    o_ref[...] = _dot(h_ref[...], w_ref[...]).astype(o_ref.dtype)


def _norm_matmul(x, gain, shift, scale, w, layer=0, col_start=0, n_cols=None, out_dtype=F32):
    bsz, seq, d = x.shape
    w = w[None] if w.ndim == 2 else w
    n = w.shape[2] - col_start if n_cols is None else n_cols
    tm = _pick(seq, (1024, 512, 256, 128))
    tn = _pick(math.gcd(n, col_start) if col_start else n, (512, 256, 128))
    first = col_start // tn
    return pl.pallas_call(
        _nmm_plain_kernel,
        grid=(bsz, seq // tm, n // tn),
        in_specs=[
            pl.BlockSpec((None, tm, d), lambda b, i, j: (b, i, 0)),
            pl.BlockSpec((1, d), lambda b, i, j: (0, 0)),
            pl.BlockSpec((None, 1, d), lambda b, i, j: (b, 0, 0)),
            pl.BlockSpec((None, 1, d), lambda b, i, j: (b, 0, 0)),
            pl.BlockSpec((None, d, tn), lambda b, i, j: (layer, 0, first + j)),
        ],
        out_specs=pl.BlockSpec((None, tm, tn), lambda b, i, j: (b, i, j)),
        out_shape=jax.ShapeDtypeStruct((bsz, seq, n), out_dtype),
        scratch_shapes=[pltpu.VMEM((tm, d), BF16)],
        compiler_params=_cparams(3),
        name="norm_matmul",
    )(x, gain.reshape(1, d), shift.reshape(bsz, 1, d), scale.reshape(bsz, 1, d), w)


def _nmm_conv_kernel(x_ref, g_ref, sh_ref, sc_ref, w_ref, cw_ref, o_ref, h_ref, halo_ref, acc0_ref, acc1_ref,
                     *, n_q_tiles, n_qk_tiles):
    i = pl.program_id(1)
    j = pl.program_id(2)
    tm, tn = o_ref.shape
    jt = jnp.maximum(j - 1, 0)

    @pl.when(j == 0)
    def _():
        h_ref[...] = _modnorm(x_ref[...], g_ref[...], sh_ref[...], sc_ref[...]).astype(BF16)
        acc1_ref[...] = jnp.zeros(acc1_ref.shape, F32)

    @pl.when((i == 0) & (j == 0))
    def _():
        halo_ref[...] = jnp.zeros(halo_ref.shape, F32)

    def step(done_ref, next_ref):
        prev_rows = halo_ref[jt]
        done_ref[0:HALO, :] = prev_rows
        acc = done_ref[HALO:, :]
        halo_ref[jt] = jnp.where(j > 0, acc[tm - HALO:, :], prev_rows)
        cw = cw_ref[...]
        y = cw[3:4, :] * acc
        for back in (1, 2, 3):
            y = y + cw[3 - back:4 - back, :] * done_ref[pl.ds(HALO - back, tm), :]
        y = _silu(y)
        is_qk = jt < n_qk_tiles
        qscale = jnp.where(jt < n_q_tiles, HEAD ** -0.5, 1.0).astype(F32)
        for hh in range(tn // HEAD):
            yh = y[:, hh * HEAD:(hh + 1) * HEAD]
            ss = jnp.sum(yh * yh, axis=-1, keepdims=True)
            o_ref[:, hh * HEAD:(hh + 1) * HEAD] = yh * jnp.where(is_qk, lax.rsqrt(ss + L2_EPS) * qscale, 1.0)
        next_ref[HALO:, :] = _dot(h_ref[...], w_ref[...])

    @pl.when(j % 2 == 0)
    def _():
        step(acc1_ref, acc0_ref)

    @pl.when(j % 2 == 1)
    def _():
        step(acc0_ref, acc1_ref)


def _norm_matmul_conv(x, gain, shift, scale, w, layer, conv_w, qk_width):
    bsz, seq, d = x.shape
    n = conv_w.shape[1]
    tm = _pick(seq, (1024, 512, 256, 128))
    tn = _pick(qk_width, (512, 256, 128))
    n_tiles = n // tn
    kern = functools.partial(_nmm_conv_kernel, n_q_tiles=qk_width // tn, n_qk_tiles=2 * qk_width // tn)
    cur = lambda j: jnp.minimum(j, n_tiles - 1)
    old = lambda j: jnp.maximum(j - 1, 0)
    return pl.pallas_call(
        kern,
        grid=(bsz, seq // tm, n_tiles + 1),
        in_specs=[
            pl.BlockSpec((None, tm, d), lambda b, i, j: (b, i, 0)),
            pl.BlockSpec((1, d), lambda b, i, j: (0, 0)),
            pl.BlockSpec((None, 1, d), lambda b, i, j: (b, 0, 0)),
            pl.BlockSpec((None, 1, d), lambda b, i, j: (b, 0, 0)),
            pl.BlockSpec((None, d, tn), lambda b, i, j: (layer, 0, cur(j))),
            pl.BlockSpec((CONV_WIDTH, tn), lambda b, i, j: (0, old(j))),
        ],
        out_specs=pl.BlockSpec((None, tm, tn), lambda b, i, j: (b, i, old(j))),
        out_shape=jax.ShapeDtypeStruct((bsz, seq, n), F32),
        scratch_shapes=[pltpu.VMEM((tm, d), BF16), pltpu.VMEM((n_tiles, HALO, tn), F32),
                        pltpu.VMEM((HALO + tm, tn), F32), pltpu.VMEM((HALO + tm, tn), F32)],
        compiler_params=_cparams(3),
        name="norm_matmul_conv",
    )(x, gain.reshape(1, d), shift.reshape(bsz, 1, d), scale.reshape(bsz, 1, d), w, conv_w)


def _delta_kernel(q_ref, k_ref, v_ref, z_ref, ba_ref, par_ref, og_ref, o_ref,
                  state_ref, u_ref, w_ref, qd_ref, kd_ref, at_ref, a_ref,
                  *, heads_per_step, chunks_per_step, n_vheads):
    s = pl.program_id(2)
    hg = pl.program_id(1)
    g_heads = heads_per_step
    c = CHUNK
    slot = s % 2
    prev = 1 - slot

    @pl.when(s == 0)
    def _():
        state_ref[...] = jnp.zeros(state_ref.shape, F32)
        u_ref[1] = jnp.zeros(u_ref.shape[1:], F32)
        w_ref[1] = jnp.zeros(w_ref.shape[1:], BF16)
        qd_ref[1] = jnp.zeros(qd_ref.shape[1:], BF16)
        kd_ref[1] = jnp.zeros(kd_ref.shape[1:], BF16)
        at_ref[1] = jnp.zeros(at_ref.shape[1:], BF16)
        a_ref[1] = jnp.zeros(a_ref.shape[1:], F32)

    row = lax.broadcasted_iota(jnp.int32, (c, c), 0)
    col = lax.broadcasted_iota(jnp.int32, (c, c), 1)
    lower = row >= col
    strict = row > col
    eye = row == col
    lane = lax.broadcasted_iota(jnp.int32, (c, HEAD), 1)
    tril = jnp.where(lower, 1.0, 0.0).astype(BF16)

    items = [(cb, h) for cb in range(chunks_per_step) for h in range(g_heads)]
    rows_of = lambda cb: slice(cb * c, (cb + 1) * c)
    cols_of = lambda i: slice(i * HEAD, (i + 1) * HEAD)

    pcols = DELTA_PACK * c
    prow = lax.broadcasted_iota(jnp.int32, (c, pcols), 0)
    plane = lax.broadcasted_iota(jnp.int32, (c, pcols), 1)
    pcol = jnp.bitwise_and(plane, c - 1)
    pblk = jnp.right_shift(plane, c.bit_length() - 1)
    p_lower = prow >= pcol
    p_strict = prow > pcol
    p_eye = prow == pcol
    in_block = [pblk == i for i in range(DELTA_PACK)]
    packs = [(cb, qd) for cb in range(chunks_per_step) for qd in range(g_heads // DELTA_PACK)]
    heads_of = lambda qd: [DELTA_PACK * qd + i for i in range(DELTA_PACK)]

    def pack_columns(cols):
        out = cols[-1]
        for i in reversed(range(DELTA_PACK - 1)):
            out = jnp.where(in_block[i], cols[i], out)
        return out

    block_keep = [jnp.where(in_block[i], 1.0, 0.0).astype(BF16) for i in range(DELTA_PACK)]

    def block_diag(mat):
        mat16 = mat.astype(BF16)
        return jnp.concatenate([mat16 * block_keep[i] for i in range(DELTA_PACK)], axis=0)

    def same_block(size):
        shift = size.bit_length() - 1
        return jnp.right_shift(prow, shift) == jnp.right_shift(pcol, shift)

    base_mask = same_block(INV_BASE)
    join_mask = {}
    size = INV_BASE
    while size < c:
        join_mask[size] = same_block(2 * size) & ~same_block(size)
        size *= 2

    comb, kk, qk = {}, {}, {}
    xm, rhs, fac, lfull = {}, {}, {}, {}

    def gates_stage():
        for cb in range(chunks_per_step):
            ba = ba_ref[rows_of(cb), :]
            beta_full = _sigmoid(ba)
            g_full = -jnp.exp(par_ref[0:1, :]) * _softplus(ba + par_ref[1:2, :])
            g_hi, g_mid, g_lo = _split3(g_full)
            gc_full = _dot(tril, g_hi) + _dot(tril, g_mid) + _dot(tril, g_lo)
            comb[cb] = jnp.where(lane < n_vheads, beta_full, gc_full)

    def gram_stage():
        for cb in range(chunks_per_step):
            for kh_idx in range(g_heads // 2):
                k16 = k_ref[rows_of(cb), cols_of(kh_idx)].astype(BF16)
                q16 = q_ref[rows_of(cb), cols_of(kh_idx)].astype(BF16)
                k_twice = jnp.concatenate([k16, k16], axis=0)
                both = _dot_nt(jnp.concatenate([k16, q16], axis=0), k_twice)
                kk[cb, kh_idx] = both[:c, :]
                qk[cb, kh_idx] = both[c:, :]

    def decay_stage():
        for cb, qd in packs:
            betas, gcs = [], []
            for h in heads_of(qd):
                head = hg * g_heads + h
                kh = k_ref[rows_of(cb), cols_of(h // 2)]
                qh = q_ref[rows_of(cb), cols_of(h // 2)]
                vh = v_ref[rows_of(cb), cols_of(h)]
                beta_c = jnp.sum(jnp.where(lane == head, comb[cb], 0.0), axis=-1, keepdims=True)
                gc_c = jnp.sum(jnp.where(lane == head + n_vheads, comb[cb], 0.0), axis=-1, keepdims=True)
                gl = gc_c[c - 1:c, :]
                eg = jnp.exp(gc_c)
                rhs[cb, h] = jnp.concatenate([vh * beta_c, kh * (beta_c * eg)], axis=1)
                qd_ref[slot, cb, h] = (qh * eg).astype(BF16)
                kd_ref[slot, cb, h] = (kh * jnp.exp(gl - gc_c)).astype(BF16)
                a_ref[slot, cb, h] = jnp.broadcast_to(jnp.exp(gl), (1, HEAD))
                betas.append(beta_c)
                gcs.append(gc_c)
            first_kh = DELTA_PACK * qd // 2
            kk_p = jnp.concatenate([kk[cb, first_kh + i] for i in range(DELTA_PACK // 2)], axis=1)
            qk_p = jnp.concatenate([qk[cb, first_kh + i] for i in range(DELTA_PACK // 2)], axis=1)
            gc_p = pack_columns(gcs)
            gc_r = jnp.sum(jnp.where(p_eye, gc_p, 0.0), axis=0, keepdims=True)
            decay = jnp.exp(jnp.where(p_lower, gc_p - gc_r, NEG))
            lmat = jnp.where(p_strict, kk_p * pack_columns(betas) * decay, 0.0)
            at_ref[slot, cb, qd] = block_diag(qk_p * decay)
            lfull[cb, qd] = lmat
            ldiag = jnp.where(base_mask, lmat, 0.0)
            xm[cb, qd] = -ldiag
            fac[cb, qd] = ldiag

    def base_square_stage():
        for pk in packs:
            fac[pk] = _dot(fac[pk].astype(BF16), block_diag(fac[pk]))

    def base_round_stage():
        for pk in packs:
            both = jnp.concatenate([fac[pk].astype(BF16), xm[pk].astype(BF16)], axis=0)
            out = _dot(both, block_diag(fac[pk]))
            xm[pk] = xm[pk] + fac[pk] + out[c:, :]
            fac[pk] = out[:c, :]

    def base_last_stage():
        for pk in packs:
            xm[pk] = xm[pk] + fac[pk] + _dot(xm[pk].astype(BF16), block_diag(fac[pk]))

    def join_right_stage(size):
        def run():
            for pk in packs:
                joint = jnp.where(join_mask[size], lfull[pk], 0.0)
                fac[pk] = joint + _dot(joint.astype(BF16), block_diag(xm[pk]))
        return run

    def join_left_stage():
        for pk in packs:
            xm[pk] = xm[pk] - fac[pk] - _dot(xm[pk].astype(BF16), block_diag(fac[pk]))

    def solve_stage():
        for cb, qd in packs:
            rhs_rows = jnp.concatenate([rhs[cb, h].astype(BF16) for h in heads_of(qd)], axis=0)
            sol = _dot(block_diag(xm[cb, qd]), rhs_rows)
            for i, h in enumerate(heads_of(qd)):
                uw = rhs[cb, h] + sol[i * c:(i + 1) * c, :]
                u_ref[slot, cb, h] = uw[:, :HEAD]
                w_ref[slot, cb, h] = uw[:, HEAD:].astype(BF16)

    factor_stages = [gates_stage, gram_stage, decay_stage, base_square_stage]
    factor_stages += [base_round_stage] * (INV_BASE.bit_length() - 3)
    factor_stages.append(base_last_stage)
    size = INV_BASE
    while size < c:
        factor_stages += [join_right_stage(size), join_left_stage]
        size *= 2
    factor_stages.append(solve_stage)

    og = og_ref[...]
    states = [state_ref[h] for h in range(g_heads)]
    wqs, v16 = {}, {}

    def read_stage(cb):
        def run():
            for h in range(g_heads):
                wq = jnp.concatenate([w_ref[prev, cb, h], qd_ref[prev, cb, h]], axis=0)
                wqs[h] = _dot(wq, states[h].astype(BF16))
        return run

    def update_stage(cb):
        def run():
            for h in range(g_heads):
                v16[h] = (u_ref[prev, cb, h] - wqs[h][:c, :]).astype(BF16)
            intra = {}
            for qd in range(g_heads // DELTA_PACK):
                v_rows = jnp.concatenate([v16[h] for h in heads_of(qd)], axis=0)
                av = _dot(at_ref[prev, cb, qd], v_rows)
                for i, h in enumerate(heads_of(qd)):
                    intra[h] = av[i * c:(i + 1) * c, :]
            for h in range(g_heads):
                o = wqs[h][c:, :] + intra[h]
                states[h] = states[h] * a_ref[prev, cb, h] + _dot_tn(kd_ref[prev, cb, h], v16[h])
                ms = jnp.mean(o * o, axis=-1, keepdims=True)
                zh = z_ref[rows_of(cb), cols_of(h)]
                o_ref[rows_of(cb), cols_of(h)] = (o * lax.rsqrt(ms + RMS_EPS) * og * _silu(zh)).astype(o_ref.dtype)
        return run

    state_stages = []
    for cb in range(chunks_per_step):
        state_stages += [read_stage(cb), update_stage(cb)]

    stride = max(1, len(factor_stages) // (len(state_stages) + 1))
    for idx, stage in enumerate(factor_stages):
        stage()
        if idx % stride == stride - 1 and state_stages:
            state_stages.pop(0)()
    for stage in state_stages:
        stage()
    for h in range(g_heads):
        state_ref[h] = states[h]


def _delta_rule(qkv, z, ba, gate_par, o_gain, qk_width, heads_per_step=8, chunks_per_step=4):
    bsz, seq, _ = qkv.shape
    v_width = z.shape[-1]
    n_vheads = v_width // HEAD
    g = min(heads_per_step, n_vheads)
    assert g % DELTA_PACK == 0 and n_vheads % g == 0
    gk = g // 2
    cbs = chunks_per_step
    rows = cbs * CHUNK
    n_blocks = seq // rows
    n_kblocks = qk_width // (gk * HEAD)
    kern = functools.partial(_delta_kernel, heads_per_step=g, chunks_per_step=cbs, n_vheads=n_vheads)
    cur = lambda s: jnp.minimum(s, n_blocks - 1)
    old = lambda s: jnp.maximum(s - 1, 0)
    return pl.pallas_call(
        kern,
        grid=(bsz, n_vheads // g, n_blocks + 1),
        in_specs=[
            pl.BlockSpec((None, rows, gk * HEAD), lambda b, hg, s: (b, cur(s), hg)),
            pl.BlockSpec((None, rows, gk * HEAD), lambda b, hg, s: (b, cur(s), n_kblocks + hg)),
            pl.BlockSpec((None, rows, g * HEAD), lambda b, hg, s: (b, cur(s), 2 * qk_width // (g * HEAD) + hg)),
            pl.BlockSpec((None, rows, g * HEAD), lambda b, hg, s: (b, old(s), hg)),
            pl.BlockSpec((None, rows, HEAD), lambda b, hg, s: (b, cur(s), 0)),
            pl.BlockSpec((2, HEAD), lambda b, hg, s: (0, 0)),
            pl.BlockSpec((1, HEAD), lambda b, hg, s: (0, 0)),
        ],
        out_specs=pl.BlockSpec((None, rows, g * HEAD), lambda b, hg, s: (b, old(s), hg)),
        out_shape=jax.ShapeDtypeStruct((bsz, seq, v_width), BF16),
        scratch_shapes=[
            pltpu.VMEM((g, HEAD, HEAD), F32),
            pltpu.VMEM((2, cbs, g, CHUNK, HEAD), F32),
            pltpu.VMEM((2, cbs, g, CHUNK, HEAD), BF16),
            pltpu.VMEM((2, cbs, g, CHUNK, HEAD), BF16),
            pltpu.VMEM((2, cbs, g, CHUNK, HEAD), BF16),
            pltpu.VMEM((2, cbs, g // DELTA_PACK, DELTA_PACK * CHUNK, DELTA_PACK * CHUNK), BF16),
            pltpu.VMEM((2, cbs, g, 1, HEAD), F32),
        ],
        compiler_params=_cparams(3),
        name="delta_rule",
    )(qkv, qkv, qkv, z, ba, gate_par, o_gain.reshape(1, HEAD))


def _out_proj_kernel(y_ref, w_ref, x_ref, gate_ref, o_ref):
    o_ref[...] = x_ref[...] + gate_ref[...] * _dot(y_ref[...].astype(BF16), w_ref[...])


def _out_proj_norm_kernel(y_ref, w_ref, x_ref, gate_ref, g_ref, o_ref):
    x = x_ref[...] + gate_ref[...] * _dot(y_ref[...].astype(BF16), w_ref[...])
    ms = jnp.mean(x * x, axis=-1, keepdims=True)
    o_ref[...] = x * lax.rsqrt(ms + RMS_EPS) * g_ref[...]


def _out_proj(y, w, layer, x, gate, final_gain=None):
    bsz, seq, k = y.shape
    d = w.shape[2]
    fused_norm = final_gain is not None
    tm = _pick(seq, (512, 256, 128)) if fused_norm else _pick(seq, (1024, 512, 256, 128))
    tn = d if fused_norm else _pick(d, (512, 256, 128))
    in_specs = [
        pl.BlockSpec((None, tm, k), lambda b, i, j: (b, i, 0)),
        pl.BlockSpec((None, k, tn), lambda b, i, j: (layer, 0, j)),
        pl.BlockSpec((None, tm, tn), lambda b, i, j: (b, i, j)),
        pl.BlockSpec((None, 1, tn), lambda b, i, j: (b, 0, j)),
    ]
    args = [y, w, x, gate.reshape(bsz, 1, d)]
    if fused_norm:
        in_specs.append(pl.BlockSpec((1, d), lambda b, i, j: (0, 0)))
        args.append(final_gain.reshape(1, d))
    return pl.pallas_call(
        _out_proj_norm_kernel if fused_norm else _out_proj_kernel,
        grid=(bsz, seq // tm, d // tn),
        in_specs=in_specs,
        out_specs=pl.BlockSpec((None, tm, tn), lambda b, i, j: (b, i, j)),
        out_shape=jax.ShapeDtypeStruct((bsz, seq, d), F32),
        compiler_params=_cparams(3),
        name="out_proj_norm" if fused_norm else "out_proj",
    )(*args)


def _block_slabs(dil, res, nb):
    blocks_per_res = ATT_TILE // (BAND * dil)
    slab = PLANE_ROWS // blocks_per_res
    return [((res + dil * a) * PLANE_ROWS + slab * nb, slab) for a in range(PLANES // dil)]


def _block_positions(dil):
    n_slabs = PLANES // dil
    slab = BAND // n_slabs
    return [n_slabs * i + a for a in range(n_slabs) for i in range(slab)]


def _attn_blocks():
    blocks = []
    for gi, (_, dil) in enumerate(DILATION_GROUPS):
        blocks_per_res = ATT_TILE // (BAND * dil)
        for res in range(dil):
            for nb in range(blocks_per_res):
                prev_nb = (nb - 1) % blocks_per_res
                blocks.append((gi, _block_slabs(dil, res, nb), _block_slabs(dil, res, prev_nb), nb == 0))
    return blocks


def _gather_rows(ref, slabs):
    parts = [ref[start:start + size, :] for start, size in slabs]
    return parts[0] if len(parts) == 1 else jnp.concatenate(parts, axis=0)


def _scatter_rows(ref, slabs, val):
    off = 0
    for start, size in slabs:
        ref[start:start + size, :] = val[off:off + size, :].astype(ref.dtype)
        off += size


def _attn_kernel(q0_ref, q1_ref, q2_ref, kp_ref, ko_ref, vp_ref, vo_ref, z_ref, bias_ref, o_ref,
                 m_ref, l_ref, acc_ref):
    first_tile = pl.program_id(1) == 0
    col = lax.broadcasted_iota(jnp.int32, (1, 2 * BAND), 1)
    head_penalty = jnp.where((col < BAND) & first_tile, NEG, 0.0)
    scale = HEAD ** -0.5 * LOG2_E
    q_refs = (q0_ref, q1_ref, q2_ref)
    last_group = len(DILATION_GROUPS) - 1

    def score_stage(blk):
        gi, q_slabs, prev_slabs, prev_tile = blk
        q16 = _gather_rows(q_refs[gi], q_slabs).astype(BF16)
        k_prev = _gather_rows(kp_ref if prev_tile else ko_ref, prev_slabs)
        k16 = jnp.concatenate([k_prev, _gather_rows(ko_ref, q_slabs)], axis=0).astype(BF16)
        return _dot_nt(q16, k16)

    def softmax_stage(blk, s):
        gi, _, _, prev_tile = blk
        bias = bias_ref[gi]
        if prev_tile:
            bias = bias + head_penalty
        s = s * scale + bias
        m = jnp.max(s, axis=-1, keepdims=True)
        p = jnp.exp2(s - m)
        return m, jnp.sum(p, axis=-1, keepdims=True), p.astype(BF16)

    def value_stage(blk, p16):
        _, q_slabs, prev_slabs, prev_tile = blk
        v_prev = _gather_rows(vp_ref if prev_tile else vo_ref, prev_slabs)
        v16 = jnp.concatenate([v_prev, _gather_rows(vo_ref, q_slabs)], axis=0).astype(BF16)
        return _dot(p16, v16)

    def merge_stage(blk, m, l, pv):
        gi, q_slabs, _, _ = blk
        if gi == 0:
            m_new = jnp.broadcast_to(m, (BAND, HEAD))
            l_new = jnp.broadcast_to(l, (BAND, HEAD))
            acc_new = pv
        else:
            m_old = _gather_rows(m_ref, q_slabs)
            m_new = jnp.maximum(m_old, m)
            w_old = jnp.exp2(m_old - m_new)
            w_blk = jnp.exp2(m - m_new)
            l_new = w_old * _gather_rows(l_ref, q_slabs) + w_blk * l
            acc_new = w_old * _gather_rows(acc_ref, q_slabs) + w_blk * pv
        if gi == last_group:
            _scatter_rows(o_ref, q_slabs, acc_new / l_new * _silu(_gather_rows(z_ref, q_slabs)))
        else:
            _scatter_rows(m_ref, q_slabs, m_new)
            _scatter_rows(l_ref, q_slabs, l_new)
            _scatter_rows(acc_ref, q_slabs, acc_new)

    blocks = _attn_blocks()
    batches = [blocks[i:i + ATT_BATCH] for i in range(0, len(blocks), ATT_BATCH)]
    scores_next = [score_stage(blk) for blk in batches[0]]
    for bi, batch in enumerate(batches):
        scores = scores_next
        if bi + 1 < len(batches):
            scores_next = [score_stage(blk) for blk in batches[bi + 1]]
        stats = [softmax_stage(blk, s) for blk, s in zip(batch, scores)]
        pvs = [value_stage(blk, p16) for blk, (_, _, p16) in zip(batch, stats)]
        for blk, (m, l, _), pv in zip(batch, stats, pvs):
            merge_stage(blk, m, l, pv)


def _dilated_attention(qz, kv, bias):
    bsz, seq, _ = qz.shape
    width = kv.shape[-1] // 2
    heads = width // HEAD
    t = ATT_TILE
    row_tile = lambda b, i, h: (b, i, h)

    def q_spec(gi):
        return pl.BlockSpec((None, t, HEAD), lambda b, i, h: (b, i, gi * heads + h))

    return pl.pallas_call(
        _attn_kernel,
        grid=(bsz, seq // t, heads),
        in_specs=[
            q_spec(0), q_spec(1), q_spec(2),
            pl.BlockSpec((None, t, HEAD), lambda b, i, h: (b, jnp.maximum(i - 1, 0), h)),
            pl.BlockSpec((None, t, HEAD), row_tile),
            pl.BlockSpec((None, t, HEAD), lambda b, i, h: (b, jnp.maximum(i - 1, 0), heads + h)),
            pl.BlockSpec((None, t, HEAD), lambda b, i, h: (b, i, heads + h)),
            pl.BlockSpec((None, t, HEAD), lambda b, i, h: (b, i, 3 * heads + h)),
            pl.BlockSpec((3, None, BAND, 2 * BAND), lambda b, i, h: (0, h, 0, 0)),
        ],
        out_specs=pl.BlockSpec((None, t, HEAD), row_tile),
        out_shape=jax.ShapeDtypeStruct((bsz, seq, width), BF16),
        scratch_shapes=[pltpu.VMEM((t, HEAD), F32)] * 3,
        compiler_params=_cparams(3),
        name="dilated_attention",
    )(qz, qz, qz, kv, kv, kv, kv, qz, bias)


def _t5_bucket(dist):
    max_exact = N_BUCKETS // 2
    n = jnp.maximum(dist, 0)
    large = max_exact + (jnp.log(jnp.maximum(n, 1).astype(F32) / max_exact)
                         / math.log(MAX_DISTANCE / max_exact)
                         * (N_BUCKETS - max_exact)).astype(jnp.int32)
    large = jnp.minimum(large, N_BUCKETS - 1)
    return jnp.where(n < max_exact, n, large)


def _band_bias(rel_bias, heads):
    out = []
    for gi, (_, dil) in enumerate(DILATION_GROUPS):
        pos = jnp.asarray(_block_positions(dil), jnp.int32)
        key_pos = jnp.concatenate([pos, pos + BAND])
        rel = pos[:, None] + BAND - key_pos[None, :]
        valid = (rel >= 0) & (rel <= BAND)
        table = rel_bias[:, gi * heads:(gi + 1) * heads].astype(F32)
        onehot = (_t5_bucket(rel * dil)[None] == jnp.arange(N_BUCKETS)[:, None, None]).astype(F32)
        b = jnp.einsum("nh,nqk->hqk", table, onehot, precision=lax.Precision.HIGHEST)
        out.append(jnp.where(valid[None], b * LOG2_E, NEG))
    return jnp.stack(out, axis=0)


def _permute_kernel(x_ref, o_ref):
    for r in range(PLANES):
        o_ref[r * PLANE_ROWS:(r + 1) * PLANE_ROWS, :] = x_ref[pl.ds(r, PLANE_ROWS, stride=PLANES), :]


def _unpermute_kernel(x_ref, o_ref):
    for r in range(PLANES):
        o_ref[pl.ds(r, PLANE_ROWS, stride=PLANES), :] = x_ref[r * PLANE_ROWS:(r + 1) * PLANE_ROWS, :]


def _move_rows(x, body, name):
    bsz, seq, d = x.shape
    spec = pl.BlockSpec((None, ATT_TILE, HEAD), lambda b, t, j: (b, t, j))
    return pl.pallas_call(
        body,
        grid=(bsz, seq // ATT_TILE, d // HEAD),
        in_specs=[spec],
        out_specs=spec,
        out_shape=jax.ShapeDtypeStruct((bsz, seq, d), x.dtype),
        compiler_params=_cparams(3),
        name=name,
    )(x)


def kernel(x, c, norm_gain, w_mod, b_mod, w_in_a, conv_w_a, a_log, dt_bias, o_norm_a, w_out_a,
           kv_gain, w_kv_mod, b_kv_mod, w_kv, w_in_b, w_out_b, rel_bias, final_gain):
    bsz, seq, d = x.shape
    depth = norm_gain.shape[0]
    n_a = w_in_a.shape[0]
    qk_width = d
    v_width = w_out_a.shape[1]
    n_vheads = v_width // HEAD
    conv_ch = 2 * qk_width + v_width
    dil_width = w_out_b.shape[1]
    dil_heads = dil_width // HEAD
    assert seq % ATT_TILE == 0 and seq % CHUNK == 0 and 2 * n_vheads <= HEAD

    mods = _mod_proj(c, w_mod, b_mod)
    kv_mod = _mod_proj(c, w_kv_mod[None], b_kv_mod[None])[0]
    bias = _band_bias(rel_bias, dil_heads)

    w_in_a16, w_out_a16 = w_in_a.astype(BF16), w_out_a.astype(BF16)
    w_in_b16, w_out_b16 = w_in_b.astype(BF16), w_out_b.astype(BF16)
    w_ba16 = jnp.pad(w_in_a[:, :, conv_ch + v_width:].astype(BF16), ((0, 0), (0, 0), (0, HEAD - 2 * n_vheads)))

    kv = None
    for layer in range(depth):
        shift, scale, gate = mods[layer, :, :d], mods[layer, :, d:2 * d], mods[layer, :, 2 * d:]
        if layer < n_a:
            qkv = _norm_matmul_conv(x, norm_gain[layer], shift, scale, w_in_a16, layer, conv_w_a[layer], qk_width)
            z = _norm_matmul(x, norm_gain[layer], shift, scale, w_in_a16, layer, col_start=conv_ch, n_cols=v_width)
            ba = _norm_matmul(x, norm_gain[layer], shift, scale, w_ba16, layer)
            gate_par = jnp.zeros((2, HEAD), F32)
            gate_par = gate_par.at[0, n_vheads:2 * n_vheads].set(a_log[layer])
            gate_par = gate_par.at[1, n_vheads:2 * n_vheads].set(dt_bias[layer])
            y = _delta_rule(qkv, z, ba, gate_par, o_norm_a[layer], qk_width)
            x = _out_proj(y, w_out_a16, layer, x, gate)
        else:
            if kv is None:
                x = _move_rows(x, _permute_kernel, "permute_rows")
                kv = _norm_matmul(x, kv_gain, kv_mod[:, :d], kv_mod[:, d:], w_kv.astype(BF16))
            j = layer - n_a
            qz = _norm_matmul(x, norm_gain[layer], shift, scale, w_in_b16, j)
            y = _dilated_attention(qz, kv, bias)
            x = _out_proj(y, w_out_b16, j, x, gate, final_gain if layer == depth - 1 else None)
    assert kv is not None and depth > n_a
    return _move_rows(x, _unpermute_kernel, "unpermute_rows")
```

```python
import functools
import math

import jax
import jax.numpy as jnp
from jax import lax
from jax.experimental import pallas as pl
from jax.experimental.pallas import tpu as pltpu

F32 = jnp.float32
BF16 = jnp.bfloat16

RMS_EPS = 1e-6
L2_EPS = 1e-6
HEAD = 128
BF16_SUBLANES = 16
CHUNK = 64
DELTA_PACK = 4
INV_BASE = 8
CONV_WIDTH = 4
HALO = 8
DILATION_GROUPS =((128, 1), (512, 4), (2048, 16))
BAND = 128
ATT_TILE = 2048
PLANES = 16
PLANE_ROWS = ATT_TILE // PLANES
ATT_BATCH = 4
N_BUCKETS = 32
MAX_DISTANCE = 2048
NEG = -1e30
LOG2_E = math.log2(math.e)
VMEM_LIMIT = 56 * 1024 * 1024


def _cparams(n_axes):
    return pltpu.CompilerParams(
        dimension_semantics=("arbitrary",) * n_axes, vmem_limit_bytes=VMEM_LIMIT)


def _dot(a, b):
    return jnp.dot(a, b, preferred_element_type=F32)


def _dot_nt(a, b):
    return lax.dot_general(a, b, (((1,), (1,)), ((), ())), preferred_element_type=F32)


def _dot_tn(a, b):
    return lax.dot_general(a, b, (((0,), (0,)), ((), ())), preferred_element_type=F32)


def _sigmoid(x):
    return 1.0 / (1.0 + jnp.exp(-x))


def _silu(x):
    return x * _sigmoid(x)


def _softplus(x):
    return jnp.maximum(x, 0.0) + jnp.log(1.0 + jnp.exp(-jnp.abs(x)))


def _split3(x):
    hi = x.astype(BF16)
    r1 = x - hi.astype(F32)
    mid = r1.astype(BF16)
    lo = (r1 - mid.astype(F32)).astype(BF16)
    return hi, mid, lo


def _pick(n, prefs):
    for p in prefs:
        if n % p == 0:
            return p
    return n


def _mod_kernel(c_ref, w_ref, b_ref, o_ref):
    ca = _silu(c_ref[...])
    w = w_ref[...]
    c_hi, c_mid, c_lo = _split3(ca)
    w_hi = w.astype(BF16)
    w_lo = (w - w_hi.astype(F32)).astype(BF16)
    acc = _dot(c_hi, w_hi) + _dot(c_hi, w_lo) + _dot(c_mid, w_hi) + _dot(c_lo, w_hi)
    o_ref[...] = acc + b_ref[...]


def _mod_proj(c, w, b):
    nl, d, n = w.shape
    n_rows = c.shape[0]
    bsz = -(-n_rows // BF16_SUBLANES) * BF16_SUBLANES
    c = jnp.pad(c, ((0, bsz - n_rows), (0, 0)))
    tn = _pick(n, (512, 256, 128))
    out = pl.pallas_call(
        _mod_kernel,
        grid=(nl, n // tn),
        in_specs=[
            pl.BlockSpec((bsz, d), lambda l, j: (0, 0)),
            pl.BlockSpec((None, d, tn), lambda l, j: (l, 0, j)),
            pl.BlockSpec((None, 1, tn), lambda l, j: (l, 0, j)),
        ],
        out_specs=pl.BlockSpec((None, bsz, tn), lambda l, j: (l, 0, j)),
        out_shape=jax.ShapeDtypeStruct((nl, bsz, n), F32),
        compiler_params=_cparams(2),
        name="mod_proj",
    )(c, w, b.reshape(nl, 1, n))
    return out[:, :n_rows]


def _modnorm(x, gain, shift, scale):
    ms = jnp.mean(x * x, axis=-1, keepdims=True)
    return x * lax.rsqrt(ms + RMS_EPS) * (gain * (1.0 + scale)) + shift


def _nmm_plain_kernel(x_ref, g_ref, sh_ref, sc_ref, w_ref, o_ref, h_ref):
    @pl.when(pl.program_id(2) == 0)
    def _():
        h_ref[...] = _modnorm(x_ref[...], g_ref[...], sh_ref[...], sc_ref[...]).astype(BF16)

    o_ref[...] = _dot(h_ref[...], w_ref[...]).astype(o_ref.dtype)


def _nmm_side_kernel(x_ref, g_ref, sh_ref, sc_ref, w_ref, ws_ref, o_ref, os_ref, h_ref):
    @pl.when(pl.program_id(2) == 0)
    def _():
        h_ref[...] = _modnorm(x_ref[...], g_ref[...], sh_ref[...], sc_ref[...]).astype(BF16)
        os_ref[...] = _dot(h_ref[...], ws_ref[...])

    o_ref[...] = _dot(h_ref[...], w_ref[...]).astype(o_ref.dtype)


def _norm_matmul(x, gain, shift, scale, w, layer=0, col_start=0, n_cols=None, out_dtype=F32, w_side=None):
    bsz, seq, d = x.shape
    w = w[None] if w.ndim == 2 else w
    n = w.shape[2] - col_start if n_cols is None else n_cols
    tm = _pick(seq, (1024, 512, 256, 128))
    tn = _pick(math.gcd(n, col_start) if col_start else n, (512, 256, 128))
    first = col_start // tn
    in_specs = [
        pl.BlockSpec((None, tm, d), lambda b, i, j: (b, i, 0)),
        pl.BlockSpec((1, d), lambda b, i, j: (0, 0)),
        pl.BlockSpec((None, 1, d), lambda b, i, j: (b, 0, 0)),
        pl.BlockSpec((None, 1, d), lambda b, i, j: (b, 0, 0)),
        pl.BlockSpec((None, d, tn), lambda b, i, j: (layer, 0, first + j)),
    ]
    args = [x, gain.reshape(1, d), shift.reshape(bsz, 1, d), scale.reshape(bsz, 1, d), w]
    out_specs = pl.BlockSpec((None, tm, tn), lambda b, i, j: (b, i, j))
    out_shape = jax.ShapeDtypeStruct((bsz, seq, n), out_dtype)
    if w_side is not None:
        in_specs.append(pl.BlockSpec((None, d, HEAD), lambda b, i, j: (layer, 0, 0)))
        args.append(w_side)
        out_specs = [out_specs, pl.BlockSpec((None, tm, HEAD), lambda b, i, j: (b, i, 0))]
        out_shape = [out_shape, jax.ShapeDtypeStruct((bsz, seq, HEAD), F32)]
    return pl.pallas_call(
        _nmm_plain_kernel if w_side is None else _nmm_side_kernel,
        grid=(bsz, seq // tm, n // tn),
        in_specs=in_specs,
        out_specs=out_specs,
        out_shape=out_shape,
        scratch_shapes=[pltpu.VMEM((tm, d), BF16)],
        compiler_params=_cparams(3),
        name="norm_matmul",
    )(*args)


def _nmm_conv_kernel(x_ref, g_ref, sh_ref, sc_ref, w_ref, cw_ref, o_ref, h_ref, halo_ref, acc0_ref, acc1_ref,
                     *, n_q_tiles, n_qk_tiles):
    i = pl.program_id(1)
    j = pl.program_id(2)
    tm, tn = o_ref.shape
    jt = jnp.maximum(j - 1, 0)

    @pl.when(j == 0)
    def _():
        h_ref[...] = _modnorm(x_ref[...], g_ref[...], sh_ref[...], sc_ref[...]).astype(BF16)
        acc1_ref[...] = jnp.zeros(acc1_ref.shape, F32)

    @pl.when((i == 0) & (j == 0))
    def _():
        halo_ref[...] = jnp.zeros(halo_ref.shape, F32)

    def step(done_ref, next_ref):
        prev_rows = halo_ref[jt]
        done_ref[0:HALO, :] = prev_rows
        acc = done_ref[HALO:, :]
        halo_ref[jt] = jnp.where(j > 0, acc[tm - HALO:, :], prev_rows)
        cw = cw_ref[...]
        y = cw[3:4, :] * acc
        for back in (1, 2, 3):
            y = y + cw[3 - back:4 - back, :] * done_ref[pl.ds(HALO - back, tm), :]
        y = _silu(y)
        is_qk = jt < n_qk_tiles
        qscale = jnp.where(jt < n_q_tiles, HEAD ** -0.5, 1.0).astype(F32)
        for hh in range(tn // HEAD):
            yh = y[:, hh * HEAD:(hh + 1) * HEAD]
            ss = jnp.sum(yh * yh, axis=-1, keepdims=True)
            o_ref[:, hh * HEAD:(hh + 1) * HEAD] = yh * jnp.where(is_qk, lax.rsqrt(ss + L2_EPS) * qscale, 1.0)
        next_ref[HALO:, :] = _dot(h_ref[...], w_ref[...])

    @pl.when(j % 2 == 0)
    def _():
        step(acc1_ref, acc0_ref)

    @pl.when(j % 2 == 1)
    def _():
        step(acc0_ref, acc1_ref)


def _norm_matmul_conv(x, gain, shift, scale, w, layer, conv_w, qk_width):
    bsz, seq, d = x.shape
    n = conv_w.shape[1]
    tm = _pick(seq, (1024, 512, 256, 128))
    tn = _pick(qk_width, (512, 256, 128))
    n_tiles = n // tn
    kern = functools.partial(_nmm_conv_kernel, n_q_tiles=qk_width // tn, n_qk_tiles=2 * qk_width // tn)
    cur = lambda j: jnp.minimum(j, n_tiles - 1)
    old = lambda j: jnp.maximum(j - 1, 0)
    return pl.pallas_call(
        kern,
        grid=(bsz, seq // tm, n_tiles + 1),
        in_specs=[
            pl.BlockSpec((None, tm, d), lambda b, i, j: (b, i, 0)),
            pl.BlockSpec((1, d), lambda b, i, j: (0, 0)),
            pl.BlockSpec((None, 1, d), lambda b, i, j: (b, 0, 0)),
            pl.BlockSpec((None, 1, d), lambda b, i, j: (b, 0, 0)),
            pl.BlockSpec((None, d, tn), lambda b, i, j: (layer, 0, cur(j))),
            pl.BlockSpec((CONV_WIDTH, tn), lambda b, i, j: (0, old(j))),
        ],
        out_specs=pl.BlockSpec((None, tm, tn), lambda b, i, j: (b, i, old(j))),
        out_shape=jax.ShapeDtypeStruct((bsz, seq, n), F32),
        scratch_shapes=[pltpu.VMEM((tm, d), BF16), pltpu.VMEM((n_tiles, HALO, tn), F32),
                        pltpu.VMEM((HALO + tm, tn), F32), pltpu.VMEM((HALO + tm, tn), F32)],
        compiler_params=_cparams(3),
        name="norm_matmul_conv",
    )(x, gain.reshape(1, d), shift.reshape(bsz, 1, d), scale.reshape(bsz, 1, d), w, conv_w)


def _delta_kernel(q_ref, k_ref, v_ref, z_ref, ba_ref, par_ref, og_ref, o_ref,
                  state_ref, u_ref, w_ref, qd_ref, kd_ref, at_ref, a_ref,
                  *, heads_per_step, chunks_per_step, n_vheads):
    s = pl.program_id(2)
    hg = pl.program_id(1)
    g_heads = heads_per_step
    c = CHUNK
    slot = s % 2
    prev = 1 - slot

    @pl.when(s == 0)
    def _():
        state_ref[...] = jnp.zeros(state_ref.shape, F32)
        u_ref[1] = jnp.zeros(u_ref.shape[1:], F32)
        w_ref[1] = jnp.zeros(w_ref.shape[1:], BF16)
        qd_ref[1] = jnp.zeros(qd_ref.shape[1:], BF16)
        kd_ref[1] = jnp.zeros(kd_ref.shape[1:], BF16)
        at_ref[1] = jnp.zeros(at_ref.shape[1:], BF16)
        a_ref[1] = jnp.zeros(a_ref.shape[1:], F32)

    row = lax.broadcasted_iota(jnp.int32, (c, c), 0)
    col = lax.broadcasted_iota(jnp.int32, (c, c), 1)
    lower = row >= col
    strict = row > col
    eye = row == col
    lane = lax.broadcasted_iota(jnp.int32, (c, HEAD), 1)
    tril = jnp.where(lower, 1.0, 0.0).astype(BF16)

    items = [(cb, h) for cb in range(chunks_per_step) for h in range(g_heads)]
    rows_of = lambda cb: slice(cb * c, (cb + 1) * c)
    cols_of = lambda i: slice(i * HEAD, (i + 1) * HEAD)

    pcols = DELTA_PACK * c
    prow = lax.broadcasted_iota(jnp.int32, (c, pcols), 0)
    plane = lax.broadcasted_iota(jnp.int32, (c, pcols), 1)
    pcol = jnp.bitwise_and(plane, c - 1)
    pblk = jnp.right_shift(plane, c.bit_length() - 1)
    p_lower = prow >= pcol
    p_strict = prow > pcol
    p_eye = prow == pcol
    in_block = [pblk == i for i in range(DELTA_PACK)]
    packs = [(cb, qd) for cb in range(chunks_per_step) for qd in range(g_heads // DELTA_PACK)]
    heads_of = lambda qd: [DELTA_PACK * qd + i for i in range(DELTA_PACK)]

    def pack_columns(cols):
        out = cols[-1]
        for i in reversed(range(DELTA_PACK - 1)):
            out = jnp.where(in_block[i], cols[i], out)
        return out

    block_keep = [jnp.where(in_block[i], 1.0, 0.0).astype(BF16) for i in range(DELTA_PACK)]

    def block_diag(mat):
        mat16 = mat.astype(BF16)
        return jnp.concatenate([mat16 * block_keep[i] for i in range(DELTA_PACK)], axis=0)

    def same_block(size):
        shift = size.bit_length() - 1
        return jnp.right_shift(prow, shift) == jnp.right_shift(pcol, shift)

    base_mask = same_block(INV_BASE)
    join_mask = {}
    size = INV_BASE
    while size < c:
        join_mask[size] = same_block(2 * size) & ~same_block(size)
        size *= 2

    comb, kk, qk = {}, {}, {}
    xm, rhs, fac, lfull = {}, {}, {}, {}

    def gates_stage():
        for cb in range(chunks_per_step):
            ba = ba_ref[rows_of(cb), :]
            beta_full = _sigmoid(ba)
            g_full = -jnp.exp(par_ref[0:1, :]) * _softplus(ba + par_ref[1:2, :])
            g_hi, g_mid, g_lo = _split3(g_full)
            gc_full = _dot(tril, g_hi) + _dot(tril, g_mid) + _dot(tril, g_lo)
            comb[cb] = jnp.where(lane < n_vheads, beta_full, gc_full)

    def gram_stage():
        for cb in range(chunks_per_step):
            for kh_idx in range(g_heads // 2):
                k16 = k_ref[rows_of(cb), cols_of(kh_idx)].astype(BF16)
                q16 = q_ref[rows_of(cb), cols_of(kh_idx)].astype(BF16)
                k_twice = jnp.concatenate([k16, k16], axis=0)
                both = _dot_nt(jnp.concatenate([k16, q16], axis=0), k_twice)
                kk[cb, kh_idx] = both[:c, :]
                qk[cb, kh_idx] = both[c:, :]

    def decay_stage():
        for cb, qd in packs:
            betas, gcs = [], []
            for h in heads_of(qd):
                head = hg * g_heads + h
                kh = k_ref[rows_of(cb), cols_of(h // 2)]
                qh = q_ref[rows_of(cb), cols_of(h // 2)]
                vh = v_ref[rows_of(cb), cols_of(h)]
                beta_c = jnp.sum(jnp.where(lane == head, comb[cb], 0.0), axis=-1, keepdims=True)
                gc_c = jnp.sum(jnp.where(lane == head + n_vheads, comb[cb], 0.0), axis=-1, keepdims=True)
                gl = gc_c[c - 1:c, :]
                eg = jnp.exp(gc_c)
                rhs[cb, h] = jnp.concatenate([vh * beta_c, kh * (beta_c * eg)], axis=1)
                qd_ref[slot, cb, h] = (qh * eg).astype(BF16)
                kd_ref[slot, cb, h] = (kh * jnp.exp(gl - gc_c)).astype(BF16)
                a_ref[slot, cb, h] = jnp.broadcast_to(jnp.exp(gl), (1, HEAD))
                betas.append(beta_c)
                gcs.append(gc_c)
            first_kh = DELTA_PACK * qd // 2
            kk_p = jnp.concatenate([kk[cb, first_kh + i] for i in range(DELTA_PACK // 2)], axis=1)
            qk_p = jnp.concatenate([qk[cb, first_kh + i] for i in range(DELTA_PACK // 2)], axis=1)
            gc_p = pack_columns(gcs)
            gc_r = jnp.sum(jnp.where(p_eye, gc_p, 0.0), axis=0, keepdims=True)
            decay = jnp.exp(jnp.where(p_lower, gc_p - gc_r, NEG))
            lmat = jnp.where(p_strict, kk_p * pack_columns(betas) * decay, 0.0)
            at_ref[slot, cb, qd] = block_diag(qk_p * decay)
            lfull[cb, qd] = lmat
            ldiag = jnp.where(base_mask, lmat, 0.0)
            xm[cb, qd] = -ldiag
            fac[cb, qd] = ldiag

    def base_square_stage():
        for pk in packs:
            fac[pk] = _dot(fac[pk].astype(BF16), block_diag(fac[pk]))

    def base_round_stage():
        for pk in packs:
            both = jnp.concatenate([fac[pk].astype(BF16), xm[pk].astype(BF16)], axis=0)
            out = _dot(both, block_diag(fac[pk]))
            xm[pk] = xm[pk] + fac[pk] + out[c:, :]
            fac[pk] = out[:c, :]

    def base_last_stage():
        for pk in packs:
            xm[pk] = xm[pk] + fac[pk] + _dot(xm[pk].astype(BF16), block_diag(fac[pk]))

    def join_right_stage(size):
        def run():
            for pk in packs:
                joint = jnp.where(join_mask[size], lfull[pk], 0.0)
                fac[pk] = joint + _dot(joint.astype(BF16), block_diag(xm[pk]))
        return run

    def join_left_stage():
        for pk in packs:
            xm[pk] = xm[pk] - fac[pk] - _dot(xm[pk].astype(BF16), block_diag(fac[pk]))

    def solve_stage():
        for cb, qd in packs:
            rhs_rows = jnp.concatenate([rhs[cb, h].astype(BF16) for h in heads_of(qd)], axis=0)
            sol = _dot(block_diag(xm[cb, qd]), rhs_rows)
            for i, h in enumerate(heads_of(qd)):
                uw = rhs[cb, h] + sol[i * c:(i + 1) * c, :]
                u_ref[slot, cb, h] = uw[:, :HEAD]
                w_ref[slot, cb, h] = uw[:, HEAD:].astype(BF16)

    factor_stages = [gates_stage, gram_stage, decay_stage, base_square_stage]
    factor_stages += [base_round_stage] * (INV_BASE.bit_length() - 3)
    factor_stages.append(base_last_stage)
    size = INV_BASE
    while size < c:
        factor_stages += [join_right_stage(size), join_left_stage]
        size *= 2
    factor_stages.append(solve_stage)

    og = og_ref[...]
    states = [state_ref[h] for h in range(g_heads)]
    wqs, v16 = {}, {}

    def read_stage(cb):
        def run():
            for h in range(g_heads):
                wq = jnp.concatenate([w_ref[prev, cb, h], qd_ref[prev, cb, h]], axis=0)
                wqs[h] = _dot(wq, states[h].astype(BF16))
        return run

    def update_stage(cb):
        def run():
            for h in range(g_heads):
                v16[h] = (u_ref[prev, cb, h] - wqs[h][:c, :]).astype(BF16)
            intra = {}
            for qd in range(g_heads // DELTA_PACK):
                v_rows = jnp.concatenate([v16[h] for h in heads_of(qd)], axis=0)
                av = _dot(at_ref[prev, cb, qd], v_rows)
                for i, h in enumerate(heads_of(qd)):
                    intra[h] = av[i * c:(i + 1) * c, :]
            for h in range(g_heads):
                o = wqs[h][c:, :] + intra[h]
                states[h] = states[h] * a_ref[prev, cb, h] + _dot_tn(kd_ref[prev, cb, h], v16[h])
                ms = jnp.mean(o * o, axis=-1, keepdims=True)
                zh = z_ref[rows_of(cb), cols_of(h)]
                o_ref[rows_of(cb), cols_of(h)] = (o * lax.rsqrt(ms + RMS_EPS) * og * _silu(zh)).astype(o_ref.dtype)
        return run

    state_stages = []
    for cb in range(chunks_per_step):
        state_stages += [read_stage(cb), update_stage(cb)]

    stride = max(1, len(factor_stages) // (len(state_stages) + 1))
    for idx, stage in enumerate(factor_stages):
        stage()
        if idx % stride == stride - 1 and state_stages:
            state_stages.pop(0)()
    for stage in state_stages:
        stage()
    for h in range(g_heads):
        state_ref[h] = states[h]


def _delta_rule(qkv, z, ba, gate_par, o_gain, qk_width, heads_per_step=8, chunks_per_step=4):
    bsz, seq, _ = qkv.shape
    v_width = z.shape[-1]
    n_vheads = v_width // HEAD
    g = min(heads_per_step, n_vheads)
    assert g % DELTA_PACK == 0 and n_vheads % g == 0
    gk = g // 2
    cbs = chunks_per_step
    rows = cbs * CHUNK
    n_blocks = seq // rows
    n_kblocks = qk_width // (gk * HEAD)
    kern = functools.partial(_delta_kernel, heads_per_step=g, chunks_per_step=cbs, n_vheads=n_vheads)
    cur = lambda s: jnp.minimum(s, n_blocks - 1)
    old = lambda s: jnp.maximum(s - 1, 0)
    return pl.pallas_call(
        kern,
        grid=(bsz, n_vheads // g, n_blocks + 1),
        in_specs=[
            pl.BlockSpec((None, rows, gk * HEAD), lambda b, hg, s: (b, cur(s), hg)),
            pl.BlockSpec((None, rows, gk * HEAD), lambda b, hg, s: (b, cur(s), n_kblocks + hg)),
            pl.BlockSpec((None, rows, g * HEAD), lambda b, hg, s: (b, cur(s), 2 * qk_width // (g * HEAD) + hg)),
            pl.BlockSpec((None, rows, g * HEAD), lambda b, hg, s: (b, old(s), hg)),
            pl.BlockSpec((None, rows, HEAD), lambda b, hg, s: (b, cur(s), 0)),
            pl.BlockSpec((2, HEAD), lambda b, hg, s: (0, 0)),
            pl.BlockSpec((1, HEAD), lambda b, hg, s: (0, 0)),
        ],
        out_specs=pl.BlockSpec((None, rows, g * HEAD), lambda b, hg, s: (b, old(s), hg)),
        out_shape=jax.ShapeDtypeStruct((bsz, seq, v_width), BF16),
        scratch_shapes=[
            pltpu.VMEM((g, HEAD, HEAD), F32),
            pltpu.VMEM((2, cbs, g, CHUNK, HEAD), F32),
            pltpu.VMEM((2, cbs, g, CHUNK, HEAD), BF16),
            pltpu.VMEM((2, cbs, g, CHUNK, HEAD), BF16),
            pltpu.VMEM((2, cbs, g, CHUNK, HEAD), BF16),
            pltpu.VMEM((2, cbs, g // DELTA_PACK, DELTA_PACK * CHUNK, DELTA_PACK * CHUNK), BF16),
            pltpu.VMEM((2, cbs, g, 1, HEAD), F32),
        ],
        compiler_params=_cparams(3),
        name="delta_rule",
    )(qkv, qkv, qkv, z, ba, gate_par, o_gain.reshape(1, HEAD))


def _out_proj_kernel(y_ref, w_ref, x_ref, gate_ref, o_ref):
    o_ref[...] = x_ref[...] + gate_ref[...] * _dot(y_ref[...].astype(BF16), w_ref[...])


def _out_proj_norm_kernel(y_ref, w_ref, x_ref, gate_ref, g_ref, o_ref):
    x = x_ref[...] + gate_ref[...] * _dot(y_ref[...].astype(BF16), w_ref[...])
    ms = jnp.mean(x * x, axis=-1, keepdims=True)
    o_ref[...] = x * lax.rsqrt(ms + RMS_EPS) * g_ref[...]


def _out_proj(y, w, layer, x, gate, final_gain=None):
    bsz, seq, k = y.shape
    d = w.shape[2]
    fused_norm = final_gain is not None
    tm = _pick(seq, (512, 256, 128)) if fused_norm else _pick(seq, (1024, 512, 256, 128))
    tn = d if fused_norm else _pick(d, (512, 256, 128))
    in_specs = [
        pl.BlockSpec((None, tm, k), lambda b, i, j: (b, i, 0)),
        pl.BlockSpec((None, k, tn), lambda b, i, j: (layer, 0, j)),
        pl.BlockSpec((None, tm, tn), lambda b, i, j: (b, i, j)),
        pl.BlockSpec((None, 1, tn), lambda b, i, j: (b, 0, j)),
    ]
    args = [y, w, x, gate.reshape(bsz, 1, d)]
    if fused_norm:
        in_specs.append(pl.BlockSpec((1, d), lambda b, i, j: (0, 0)))
        args.append(final_gain.reshape(1, d))
    return pl.pallas_call(
        _out_proj_norm_kernel if fused_norm else _out_proj_kernel,
        grid=(bsz, seq // tm, d // tn),
        in_specs=in_specs,
        out_specs=pl.BlockSpec((None, tm, tn), lambda b, i, j: (b, i, j)),
        out_shape=jax.ShapeDtypeStruct((bsz, seq, d), F32),
        compiler_params=_cparams(3),
        name="out_proj_norm" if fused_norm else "out_proj",
    )(*args)


def _block_slabs(dil, res, nb):
    blocks_per_res = ATT_TILE // (BAND * dil)
    slab = PLANE_ROWS // blocks_per_res
    return [((res + dil * a) * PLANE_ROWS + slab * nb, slab) for a in range(PLANES // dil)]


def _block_positions(dil):
    n_slabs = PLANES // dil
    slab = BAND // n_slabs
    return [n_slabs * i + a for a in range(n_slabs) for i in range(slab)]


def _attn_blocks():
    blocks = []
    for gi, (_, dil) in enumerate(DILATION_GROUPS):
        blocks_per_res = ATT_TILE // (BAND * dil)
        for res in range(dil):
            for nb in range(blocks_per_res):
                prev_nb = (nb - 1) % blocks_per_res
                blocks.append((gi, _block_slabs(dil, res, nb), _block_slabs(dil, res, prev_nb), nb == 0))
    return blocks


def _gather_rows(ref, slabs):
    parts = [ref[start:start + size, :] for start, size in slabs]
    return parts[0] if len(parts) == 1 else jnp.concatenate(parts, axis=0)


def _scatter_rows(ref, slabs, val):
    off = 0
    for start, size in slabs:
        ref[start:start + size, :] = val[off:off + size, :].astype(ref.dtype)
        off += size


def _attn_kernel(q0_ref, q1_ref, q2_ref, kp_ref, ko_ref, vp_ref, vo_ref, z_ref, bias_ref, o_ref,
                 m_ref, l_ref, acc_ref):
    first_tile = pl.program_id(1) == 0
    col = lax.broadcasted_iota(jnp.int32, (1, 2 * BAND), 1)
    head_penalty = jnp.where((col < BAND) & first_tile, NEG, 0.0)
    scale = HEAD ** -0.5 * LOG2_E
    q_refs = (q0_ref, q1_ref, q2_ref)
    last_group = len(DILATION_GROUPS) - 1

    def score_stage(blk):
        gi, q_slabs, prev_slabs, prev_tile = blk
        q16 = _gather_rows(q_refs[gi], q_slabs).astype(BF16)
        k_prev = _gather_rows(kp_ref if prev_tile else ko_ref, prev_slabs)
        k16 = jnp.concatenate([k_prev, _gather_rows(ko_ref, q_slabs)], axis=0).astype(BF16)
        return _dot_nt(q16, k16)

    def softmax_stage(blk, s):
        gi, _, _, prev_tile = blk
        bias = bias_ref[gi]
        if prev_tile:
            bias = bias + head_penalty
        s = s * scale + bias
        m = jnp.max(s, axis=-1, keepdims=True)
        p = jnp.exp2(s - m)
        return m, jnp.sum(p, axis=-1, keepdims=True), p.astype(BF16)

    def value_stage(blk, p16):
        _, q_slabs, prev_slabs, prev_tile = blk
        v_prev = _gather_rows(vp_ref if prev_tile else vo_ref, prev_slabs)
        v16 = jnp.concatenate([v_prev, _gather_rows(vo_ref, q_slabs)], axis=0).astype(BF16)
        return _dot(p16, v16)

    def merge_stage(blk, m, l, pv):
        gi, q_slabs, _, _ = blk
        if gi == 0:
            m_new = jnp.broadcast_to(m, (BAND, HEAD))
            l_new = jnp.broadcast_to(l, (BAND, HEAD))
            acc_new = pv
        else:
            m_old = _gather_rows(m_ref, q_slabs)
            m_new = jnp.maximum(m_old, m)
            w_old = jnp.exp2(m_old - m_new)
            w_blk = jnp.exp2(m - m_new)
            l_new = w_old * _gather_rows(l_ref, q_slabs) + w_blk * l
            acc_new = w_old * _gather_rows(acc_ref, q_slabs) + w_blk * pv
        if gi == last_group:
            _scatter_rows(o_ref, q_slabs, acc_new / l_new * _silu(_gather_rows(z_ref, q_slabs)))
        else:
            _scatter_rows(m_ref, q_slabs, m_new)
            _scatter_rows(l_ref, q_slabs, l_new)
            _scatter_rows(acc_ref, q_slabs, acc_new)

    blocks = _attn_blocks()
    batches = [blocks[i:i + ATT_BATCH] for i in range(0, len(blocks), ATT_BATCH)]
    scores_next = [score_stage(blk) for blk in batches[0]]
    for bi, batch in enumerate(batches):
        scores = scores_next
        if bi + 1 < len(batches):
            scores_next = [score_stage(blk) for blk in batches[bi + 1]]
        stats = [softmax_stage(blk, s) for blk, s in zip(batch, scores)]
        pvs = [value_stage(blk, p16) for blk, (_, _, p16) in zip(batch, stats)]
        for blk, (m, l, _), pv in zip(batch, stats, pvs):
            merge_stage(blk, m, l, pv)


def _dilated_attention(qz, kv, bias):
    bsz, seq, _ = qz.shape
    width = kv.shape[-1] // 2
    heads = width // HEAD
    t = ATT_TILE
    row_tile = lambda b, i, h: (b, i, h)

    def q_spec(gi):
        return pl.BlockSpec((None, t, HEAD), lambda b, i, h: (b, i, gi * heads + h))

    return pl.pallas_call(
        _attn_kernel,
        grid=(bsz, seq // t, heads),
        in_specs=[
            q_spec(0), q_spec(1), q_spec(2),
            pl.BlockSpec((None, t, HEAD), lambda b, i, h: (b, jnp.maximum(i - 1, 0), h)),
            pl.BlockSpec((None, t, HEAD), row_tile),
            pl.BlockSpec((None, t, HEAD), lambda b, i, h: (b, jnp.maximum(i - 1, 0), heads + h)),
            pl.BlockSpec((None, t, HEAD), lambda b, i, h: (b, i, heads + h)),
            pl.BlockSpec((None, t, HEAD), lambda b, i, h: (b, i, 3 * heads + h)),
            pl.BlockSpec((3, None, BAND, 2 * BAND), lambda b, i, h: (0, h, 0, 0)),
        ],
        out_specs=pl.BlockSpec((None, t, HEAD), row_tile),
        out_shape=jax.ShapeDtypeStruct((bsz, seq, width), BF16),
        scratch_shapes=[pltpu.VMEM((t, HEAD), F32)] * 3,
        compiler_params=_cparams(3),
        name="dilated_attention",
    )(qz, qz, qz, kv, kv, kv, kv, qz, bias)


def _t5_bucket(dist):
    max_exact = N_BUCKETS // 2
    n = jnp.maximum(dist, 0)
    large = max_exact + (jnp.log(jnp.maximum(n, 1).astype(F32) / max_exact)
                         / math.log(MAX_DISTANCE / max_exact)
                         * (N_BUCKETS - max_exact)).astype(jnp.int32)
    large = jnp.minimum(large, N_BUCKETS - 1)
    return jnp.where(n < max_exact, n, large)


def _band_bias(rel_bias, heads):
    out = []
    for gi, (_, dil) in enumerate(DILATION_GROUPS):
        pos = jnp.asarray(_block_positions(dil), jnp.int32)
        key_pos = jnp.concatenate([pos, pos + BAND])
        rel = pos[:, None] + BAND - key_pos[None, :]
        valid = (rel >= 0) & (rel <= BAND)
        table = rel_bias[:, gi * heads:(gi + 1) * heads].astype(F32)
        onehot = (_t5_bucket(rel * dil)[None] == jnp.arange(N_BUCKETS)[:, None, None]).astype(F32)
        b = jnp.einsum("nh,nqk->hqk", table, onehot, precision=lax.Precision.HIGHEST)
        out.append(jnp.where(valid[None], b * LOG2_E, NEG))
    return jnp.stack(out, axis=0)


def _permute_kernel(x_ref, o_ref):
    for r in range(PLANES):
        o_ref[r * PLANE_ROWS:(r + 1) * PLANE_ROWS, :] = x_ref[pl.ds(r, PLANE_ROWS, stride=PLANES), :]


def _unpermute_kernel(x_ref, o_ref):
    for r in range(PLANES):
        o_ref[pl.ds(r, PLANE_ROWS, stride=PLANES), :] = x_ref[r * PLANE_ROWS:(r + 1) * PLANE_ROWS, :]


def _move_rows(x, body, name):
    bsz, seq, d = x.shape
    spec = pl.BlockSpec((None, ATT_TILE, HEAD), lambda b, t, j: (b, t, j))
    return pl.pallas_call(
        body,
        grid=(bsz, seq // ATT_TILE, d // HEAD),
        in_specs=[spec],
        out_specs=spec,
        out_shape=jax.ShapeDtypeStruct((bsz, seq, d), x.dtype),
        compiler_params=_cparams(3),
        name=name,
    )(x)


def kernel(x, c, norm_gain, w_mod, b_mod, w_in_a, conv_w_a, a_log, dt_bias, o_norm_a, w_out_a,
           kv_gain, w_kv_mod, b_kv_mod, w_kv, w_in_b, w_out_b, rel_bias, final_gain):
    bsz, seq, d = x.shape
    depth = norm_gain.shape[0]
    n_a = w_in_a.shape[0]
    qk_width = d
    v_width = w_out_a.shape[1]
    n_vheads = v_width // HEAD
    conv_ch = 2 * qk_width + v_width
    dil_width = w_out_b.shape[1]
    dil_heads = dil_width // HEAD
    assert seq % ATT_TILE == 0 and seq % CHUNK == 0 and 2 * n_vheads <= HEAD

    mods = _mod_proj(c, w_mod, b_mod)
    kv_mod = _mod_proj(c, w_kv_mod[None], b_kv_mod[None])[0]
    bias = _band_bias(rel_bias, dil_heads)

    w_in_a16, w_out_a16 = w_in_a.astype(BF16), w_out_a.astype(BF16)
    w_in_b16, w_out_b16 = w_in_b.astype(BF16), w_out_b.astype(BF16)
    w_ba16 = jnp.pad(w_in_a[:, :, conv_ch + v_width:].astype(BF16), ((0, 0), (0, 0), (0, HEAD - 2 * n_vheads)))

    kv = None
    for layer in range(depth):
        shift, scale, gate = mods[layer, :, :d], mods[layer, :, d:2 * d], mods[layer, :, 2 * d:]
        if layer < n_a:
            qkv = _norm_matmul_conv(x, norm_gain[layer], shift, scale, w_in_a16, layer, conv_w_a[layer], qk_width)
            z, ba = _norm_matmul(x, norm_gain[layer], shift, scale, w_in_a16, layer, col_start=conv_ch,
                                 n_cols=v_width, w_side=w_ba16)
            gate_par = jnp.zeros((2, HEAD), F32)
            gate_par = gate_par.at[0, n_vheads:2 * n_vheads].set(a_log[layer])
            gate_par = gate_par.at[1, n_vheads:2 * n_vheads].set(dt_bias[layer])
            y = _delta_rule(qkv, z, ba, gate_par, o_norm_a[layer], qk_width)
            x = _out_proj(y, w_out_a16, layer, x, gate)
        else:
            if kv is None:
                x = _move_rows(x, _permute_kernel, "permute_rows")
                kv = _norm_matmul(x, kv_gain, kv_mod[:, :d], kv_mod[:, d:], w_kv.astype(BF16))
            j = layer - n_a
            qz = _norm_matmul(x, norm_gain[layer], shift, scale, w_in_b16, j)
            y = _dilated_attention(qz, kv, bias)
            x = _out_proj(y, w_out_b16, j, x, gate, final_gain if layer == depth - 1 else None)
    assert kv is not None and depth > n_a
    return _move_rows(x, _unpermute_kernel, "unpermute_rows")
```

```python
import functools
import math

import jax
import jax.numpy as jnp
from jax import lax
from jax.experimental import pallas as pl
from jax.experimental.pallas import tpu as pltpu

F32 = jnp.float32
BF16 = jnp.bfloat16

RMS_EPS = 1e-6
L2_EPS = 1e-6
HEAD = 128
BF16_SUBLANES = 16
CHUNK = 64
DELTA_PACK = 4
INV_BASE = 8
CONV_WIDTH = 4
HALO = 8
DILATION_GROUPS =((128, 1), (512, 4), (2048, 16))
BAND = 128
ATT_TILE = 2048
PLANES = 16
PLANE_ROWS = ATT_TILE // PLANES
ATT_BATCH = 4
N_BUCKETS = 32
MAX_DISTANCE = 2048
NEG = -1e30
LOG2_E = math.log2(math.e)
VMEM_LIMIT = 56 * 1024 * 1024


def _cparams(n_axes):
    return pltpu.CompilerParams(
        dimension_semantics=("arbitrary",) * n_axes, vmem_limit_bytes=VMEM_LIMIT)


def _dot(a, b):
    return jnp.dot(a, b, preferred_element_type=F32)


def _dot_nt(a, b):
    return lax.dot_general(a, b, (((1,), (1,)), ((), ())), preferred_element_type=F32)


def _dot_tn(a, b):
    return lax.dot_general(a, b, (((0,), (0,)), ((), ())), preferred_element_type=F32)


def _sigmoid(x):
    return 1.0 / (1.0 + jnp.exp(-x))


def _silu(x):
    return x * _sigmoid(x)


def _softplus(x):
    return jnp.maximum(x, 0.0) + jnp.log(1.0 + jnp.exp(-jnp.abs(x)))


def _split3(x):
    hi = x.astype(BF16)
    r1 = x - hi.astype(F32)
    mid = r1.astype(BF16)
    lo = (r1 - mid.astype(F32)).astype(BF16)
    return hi, mid, lo


def _pick(n, prefs):
    for p in prefs:
        if n % p == 0:
            return p
    return n


def _mod_kernel(c_ref, w_ref, b_ref, o_ref):
    ca = _silu(c_ref[...])
    w = w_ref[...]
    c_hi, c_mid, c_lo = _split3(ca)
    w_hi = w.astype(BF16)
    w_lo = (w - w_hi.astype(F32)).astype(BF16)
    acc = _dot(c_hi, w_hi) + _dot(c_hi, w_lo) + _dot(c_mid, w_hi) + _dot(c_lo, w_hi)
    o_ref[...] = acc + b_ref[...]


def _mod_proj(c, w, b):
    nl, d, n = w.shape
    n_rows = c.shape[0]
    bsz = -(-n_rows // BF16_SUBLANES) * BF16_SUBLANES
    c = jnp.pad(c, ((0, bsz - n_rows), (0, 0)))
    tn = _pick(n, (512, 256, 128))
    out = pl.pallas_call(
        _mod_kernel,
        grid=(nl, n // tn),
        in_specs=[
            pl.BlockSpec((bsz, d), lambda l, j: (0, 0)),
            pl.BlockSpec((None, d, tn), lambda l, j: (l, 0, j)),
            pl.BlockSpec((None, 1, tn), lambda l, j: (l, 0, j)),
        ],
        out_specs=pl.BlockSpec((None, bsz, tn), lambda l, j: (l, 0, j)),
        out_shape=jax.ShapeDtypeStruct((nl, bsz, n), F32),
        compiler_params=_cparams(2),
        name="mod_proj",
    )(c, w, b.reshape(nl, 1, n))
    return out[:, :n_rows]


def _modnorm(x, gain, shift, scale):
    ms = jnp.mean(x * x, axis=-1, keepdims=True)
    return x * lax.rsqrt(ms + RMS_EPS) * (gain * (1.0 + scale)) + shift


def _nmm_plain_kernel(x_ref, g_ref, sh_ref, sc_ref, w_ref, o_ref, h_ref):
    @pl.when(pl.program_id(2) == 0)
    def _():
        h_ref[...] = _modnorm(x_ref[...], g_ref[...], sh_ref[...], sc_ref[...]).astype(BF16)

    o_ref[...] = _dot(h_ref[...], w_ref[...]).astype(o_ref.dtype)


def _nmm_side_kernel(x_ref, g_ref, sh_ref, sc_ref, w_ref, ws_ref, o_ref, os_ref, h_ref):
    @pl.when(pl.program_id(2) == 0)
    def _():
        h_ref[...] = _modnorm(x_ref[...], g_ref[...], sh_ref[...], sc_ref[...]).astype(BF16)
        os_ref[...] = _dot(h_ref[...], ws_ref[...])

    o_ref[...] = _dot(h_ref[...], w_ref[...]).astype(o_ref.dtype)


def _norm_matmul(x, gain, shift, scale, w, layer=0, col_start=0, n_cols=None, out_dtype=F32, w_side=None):
    bsz, seq, d = x.shape
    w = w[None] if w.ndim == 2 else w
    n = w.shape[2] - col_start if n_cols is None else n_cols
    tm = _pick(seq, (1024, 512, 256, 128))
    tn = _pick(math.gcd(n, col_start) if col_start else n, (512, 256, 128))
    first = col_start // tn
    in_specs = [
        pl.BlockSpec((None, tm, d), lambda b, i, j: (b, i, 0)),
        pl.BlockSpec((1, d), lambda b, i, j: (0, 0)),
        pl.BlockSpec((None, 1, d), lambda b, i, j: (b, 0, 0)),
        pl.BlockSpec((None, 1, d), lambda b, i, j: (b, 0, 0)),
        pl.BlockSpec((None, d, tn), lambda b, i, j: (layer, 0, first + j)),
    ]
    args = [x, gain.reshape(1, d), shift.reshape(bsz, 1, d), scale.reshape(bsz, 1, d), w]
    out_specs = pl.BlockSpec((None, tm, tn), lambda b, i, j: (b, i, j))
    out_shape = jax.ShapeDtypeStruct((bsz, seq, n), out_dtype)
    if w_side is not None:
        in_specs.append(pl.BlockSpec((None, d, HEAD), lambda b, i, j: (layer, 0, 0)))
        args.append(w_side)
        out_specs = [out_specs, pl.BlockSpec((None, tm, HEAD), lambda b, i, j: (b, i, 0))]
        out_shape = [out_shape, jax.ShapeDtypeStruct((bsz, seq, HEAD), F32)]
    return pl.pallas_call(
        _nmm_plain_kernel if w_side is None else _nmm_side_kernel,
        grid=(bsz, seq // tm, n // tn),
        in_specs=in_specs,
        out_specs=out_specs,
        out_shape=out_shape,
        scratch_shapes=[pltpu.VMEM((tm, d), BF16)],
        compiler_params=_cparams(3),
        name="norm_matmul",
    )(*args)


def _nmm_conv_kernel(x_ref, g_ref, sh_ref, sc_ref, w_ref, cw_ref, o_ref, h_ref, halo_ref, acc0_ref, acc1_ref,
                     *, n_q_tiles, n_qk_tiles, n_col_tiles, n_steps):
    s = pl.program_id(1)
    tm, tn = o_ref.shape
    jt = jnp.maximum(s - 1, 0) % n_col_tiles

    @pl.when((s % n_col_tiles == 0) & (s < n_steps - 1))
    def _():
        h_ref[...] = _modnorm(x_ref[...], g_ref[...], sh_ref[...], sc_ref[...]).astype(BF16)

    @pl.when(s == 0)
    def _():
        acc1_ref[...] = jnp.zeros(acc1_ref.shape, F32)
        halo_ref[...] = jnp.zeros(halo_ref.shape, F32)

    def step(done_ref, next_ref):
        prev_rows = halo_ref[jt]
        done_ref[0:HALO, :] = prev_rows
        acc = done_ref[HALO:, :]
        halo_ref[jt] = jnp.where(s > 0, acc[tm - HALO:, :], prev_rows)
        cw = cw_ref[...]
        y = cw[3:4, :] * acc
        for back in (1, 2, 3):
            y = y + cw[3 - back:4 - back, :] * done_ref[pl.ds(HALO - back, tm), :]
        y = _silu(y)
        is_qk = jt < n_qk_tiles
        qscale = jnp.where(jt < n_q_tiles, HEAD ** -0.5, 1.0).astype(F32)
        for hh in range(tn // HEAD):
            yh = y[:, hh * HEAD:(hh + 1) * HEAD]
            ss = jnp.sum(yh * yh, axis=-1, keepdims=True)
            o_ref[:, hh * HEAD:(hh + 1) * HEAD] = yh * jnp.where(is_qk, lax.rsqrt(ss + L2_EPS) * qscale, 1.0)
        next_ref[HALO:, :] = _dot(h_ref[...], w_ref[...])

    @pl.when(s % 2 == 0)
    def _():
        step(acc1_ref, acc0_ref)

    @pl.when(s % 2 == 1)
    def _():
        step(acc0_ref, acc1_ref)


def _norm_matmul_conv(x, gain, shift, scale, w, layer, conv_w, qk_width):
    bsz, seq, d = x.shape
    n = conv_w.shape[1]
    tm = _pick(seq, (1024, 512, 256, 128))
    tn = _pick(qk_width, (512, 256, 128))
    n_tiles = n // tn
    n_steps = (seq // tm) * n_tiles + 1
    kern = functools.partial(_nmm_conv_kernel, n_q_tiles=qk_width // tn, n_qk_tiles=2 * qk_width // tn,
                             n_col_tiles=n_tiles, n_steps=n_steps)
    cur = lambda s: jnp.minimum(s, n_steps - 2)
    old = lambda s: jnp.maximum(s - 1, 0)
    return pl.pallas_call(
        kern,
        grid=(bsz, n_steps),
        in_specs=[
            pl.BlockSpec((None, tm, d), lambda b, s: (b, cur(s) // n_tiles, 0)),
            pl.BlockSpec((1, d), lambda b, s: (0, 0)),
            pl.BlockSpec((None, 1, d), lambda b, s: (b, 0, 0)),
            pl.BlockSpec((None, 1, d), lambda b, s: (b, 0, 0)),
            pl.BlockSpec((None, d, tn), lambda b, s: (layer, 0, cur(s) % n_tiles)),
            pl.BlockSpec((CONV_WIDTH, tn), lambda b, s: (0, old(s) % n_tiles)),
        ],
        out_specs=pl.BlockSpec((None, tm, tn), lambda b, s: (b, old(s) // n_tiles, old(s) % n_tiles)),
        out_shape=jax.ShapeDtypeStruct((bsz, seq, n), F32),
        scratch_shapes=[pltpu.VMEM((tm, d), BF16), pltpu.VMEM((n_tiles, HALO, tn), F32),
                        pltpu.VMEM((HALO + tm, tn), F32), pltpu.VMEM((HALO + tm, tn), F32)],
        compiler_params=_cparams(2),
        name="norm_matmul_conv",
    )(x, gain.reshape(1, d), shift.reshape(bsz, 1, d), scale.reshape(bsz, 1, d), w, conv_w)


def _delta_kernel(q_ref, k_ref, v_ref, z_ref, ba_ref, par_ref, og_ref, o_ref,
                  state_ref, u_ref, w_ref, qd_ref, kd_ref, at_ref, a_ref,
                  *, heads_per_step, chunks_per_step, n_vheads):
    s = pl.program_id(2)
    hg = pl.program_id(1)
    g_heads = heads_per_step
    c = CHUNK
    slot = s % 2
    prev = 1 - slot

    @pl.when(s == 0)
    def _():
        state_ref[...] = jnp.zeros(state_ref.shape, F32)
        u_ref[1] = jnp.zeros(u_ref.shape[1:], F32)
        w_ref[1] = jnp.zeros(w_ref.shape[1:], BF16)
        qd_ref[1] = jnp.zeros(qd_ref.shape[1:], BF16)
        kd_ref[1] = jnp.zeros(kd_ref.shape[1:], BF16)
        at_ref[1] = jnp.zeros(at_ref.shape[1:], BF16)
        a_ref[1] = jnp.zeros(a_ref.shape[1:], F32)

    row = lax.broadcasted_iota(jnp.int32, (c, c), 0)
    col = lax.broadcasted_iota(jnp.int32, (c, c), 1)
    lower = row >= col
    strict = row > col
    eye = row == col
    lane = lax.broadcasted_iota(jnp.int32, (c, HEAD), 1)
    tril = jnp.where(lower, 1.0, 0.0).astype(BF16)

    items = [(cb, h) for cb in range(chunks_per_step) for h in range(g_heads)]
    rows_of = lambda cb: slice(cb * c, (cb + 1) * c)
    cols_of = lambda i: slice(i * HEAD, (i + 1) * HEAD)

    pcols = DELTA_PACK * c
    prow = lax.broadcasted_iota(jnp.int32, (c, pcols), 0)
    plane = lax.broadcasted_iota(jnp.int32, (c, pcols), 1)
    pcol = jnp.bitwise_and(plane, c - 1)
    pblk = jnp.right_shift(plane, c.bit_length() - 1)
    p_lower = prow >= pcol
    p_strict = prow > pcol
    p_eye = prow == pcol
    in_block = [pblk == i for i in range(DELTA_PACK)]
    packs = [(cb, qd) for cb in range(chunks_per_step) for qd in range(g_heads // DELTA_PACK)]
    heads_of = lambda qd: [DELTA_PACK * qd + i for i in range(DELTA_PACK)]

    def pack_columns(cols):
        out = cols[-1]
        for i in reversed(range(DELTA_PACK - 1)):
            out = jnp.where(in_block[i], cols[i], out)
        return out

    block_keep = [jnp.where(in_block[i], 1.0, 0.0).astype(BF16) for i in range(DELTA_PACK)]

    def block_diag(mat):
        mat16 = mat.astype(BF16)
        return jnp.concatenate([mat16 * block_keep[i] for i in range(DELTA_PACK)], axis=0)

    def same_block(size):
        shift = size.bit_length() - 1
        return jnp.right_shift(prow, shift) == jnp.right_shift(pcol, shift)

    base_mask = same_block(INV_BASE)
    join_mask = {}
    size = INV_BASE
    while size < c:
        join_mask[size] = same_block(2 * size) & ~same_block(size)
        size *= 2

    comb, kk, qk = {}, {}, {}
    xm, rhs, fac, lfull = {}, {}, {}, {}

    def gates_stage():
        for cb in range(chunks_per_step):
            ba = ba_ref[rows_of(cb), :]
            beta_full = _sigmoid(ba)
            g_full = -jnp.exp(par_ref[0:1, :]) * _softplus(ba + par_ref[1:2, :])
            g_hi, g_mid, g_lo = _split3(g_full)
            gc_full = _dot(tril, g_hi) + _dot(tril, g_mid) + _dot(tril, g_lo)
            comb[cb] = jnp.where(lane < n_vheads, beta_full, gc_full)

    def gram_stage():
        for cb in range(chunks_per_step):
            for kh_idx in range(g_heads // 2):
                k16 = k_ref[rows_of(cb), cols_of(kh_idx)].astype(BF16)
                q16 = q_ref[rows_of(cb), cols_of(kh_idx)].astype(BF16)
                k_twice = jnp.concatenate([k16, k16], axis=0)
                both = _dot_nt(jnp.concatenate([k16, q16], axis=0), k_twice)
                kk[cb, kh_idx] = both[:c, :]
                qk[cb, kh_idx] = both[c:, :]

    def decay_stage():
        for cb, qd in packs:
            betas, gcs = [], []
            for h in heads_of(qd):
                head = hg * g_heads + h
                kh = k_ref[rows_of(cb), cols_of(h // 2)]
                qh = q_ref[rows_of(cb), cols_of(h // 2)]
                vh = v_ref[rows_of(cb), cols_of(h)]
                beta_c = jnp.sum(jnp.where(lane == head, comb[cb], 0.0), axis=-1, keepdims=True)
                gc_c = jnp.sum(jnp.where(lane == head + n_vheads, comb[cb], 0.0), axis=-1, keepdims=True)
                gl = gc_c[c - 1:c, :]
                eg = jnp.exp(gc_c)
                rhs[cb, h] = jnp.concatenate([vh * beta_c, kh * (beta_c * eg)], axis=1)
                qd_ref[slot, cb, h] = (qh * eg).astype(BF16)
                kd_ref[slot, cb, h] = (kh * jnp.exp(gl - gc_c)).astype(BF16)
                a_ref[slot, cb, h] = jnp.broadcast_to(jnp.exp(gl), (1, HEAD))
                betas.append(beta_c)
                gcs.append(gc_c)
            first_kh = DELTA_PACK * qd // 2
            kk_p = jnp.concatenate([kk[cb, first_kh + i] for i in range(DELTA_PACK // 2)], axis=1)
            qk_p = jnp.concatenate([qk[cb, first_kh + i] for i in range(DELTA_PACK // 2)], axis=1)
            gc_p = pack_columns(gcs)
            gc_r = jnp.sum(jnp.where(p_eye, gc_p, 0.0), axis=0, keepdims=True)
            decay = jnp.exp(jnp.where(p_lower, gc_p - gc_r, NEG))
            lmat = jnp.where(p_strict, kk_p * pack_columns(betas) * decay, 0.0)
            at_ref[slot, cb, qd] = block_diag(qk_p * decay)
            lfull[cb, qd] = lmat
            ldiag = jnp.where(base_mask, lmat, 0.0)
            xm[cb, qd] = -ldiag
            fac[cb, qd] = ldiag

    def base_square_stage():
        for pk in packs:
            fac[pk] = _dot(fac[pk].astype(BF16), block_diag(fac[pk]))

    def base_round_stage():
        for pk in packs:
            both = jnp.concatenate([fac[pk].astype(BF16), xm[pk].astype(BF16)], axis=0)
            out = _dot(both, block_diag(fac[pk]))
            xm[pk] = xm[pk] + fac[pk] + out[c:, :]
            fac[pk] = out[:c, :]

    def base_last_stage():
        for pk in packs:
            xm[pk] = xm[pk] + fac[pk] + _dot(xm[pk].astype(BF16), block_diag(fac[pk]))

    def join_right_stage(size):
        def run():
            for pk in packs:
                joint = jnp.where(join_mask[size], lfull[pk], 0.0)
                fac[pk] = joint + _dot(joint.astype(BF16), block_diag(xm[pk]))
        return run

    def join_left_stage():
        for pk in packs:
            xm[pk] = xm[pk] - fac[pk] - _dot(xm[pk].astype(BF16), block_diag(fac[pk]))

    def solve_stage():
        for cb, qd in packs:
            rhs_rows = jnp.concatenate([rhs[cb, h].astype(BF16) for h in heads_of(qd)], axis=0)
            sol = _dot(block_diag(xm[cb, qd]), rhs_rows)
            for i, h in enumerate(heads_of(qd)):
                uw = rhs[cb, h] + sol[i * c:(i + 1) * c, :]
                u_ref[slot, cb, h] = uw[:, :HEAD]
                w_ref[slot, cb, h] = uw[:, HEAD:].astype(BF16)

    factor_stages = [gates_stage, gram_stage, decay_stage, base_square_stage]
    factor_stages += [base_round_stage] * (INV_BASE.bit_length() - 3)
    factor_stages.append(base_last_stage)
    size = INV_BASE
    while size < c:
        factor_stages += [join_right_stage(size), join_left_stage]
        size *= 2
    factor_stages.append(solve_stage)

    og = og_ref[...]
    states = [state_ref[h] for h in range(g_heads)]
    wqs, v16 = {}, {}

    def read_stage(cb):
        def run():
            for h in range(g_heads):
                wq = jnp.concatenate([w_ref[prev, cb, h], qd_ref[prev, cb, h]], axis=0)
                wqs[h] = _dot(wq, states[h].astype(BF16))
        return run

    def update_stage(cb):
        def run():
            for h in range(g_heads):
                v16[h] = (u_ref[prev, cb, h] - wqs[h][:c, :]).astype(BF16)
            intra = {}
            for qd in range(g_heads // DELTA_PACK):
                v_rows = jnp.concatenate([v16[h] for h in heads_of(qd)], axis=0)
                av = _dot(at_ref[prev, cb, qd], v_rows)
                for i, h in enumerate(heads_of(qd)):
                    intra[h] = av[i * c:(i + 1) * c, :]
            for h in range(g_heads):
                o = wqs[h][c:, :] + intra[h]
                states[h] = states[h] * a_ref[prev, cb, h] + _dot_tn(kd_ref[prev, cb, h], v16[h])
                ms = jnp.mean(o * o, axis=-1, keepdims=True)
                zh = z_ref[rows_of(cb), cols_of(h)]
                o_ref[rows_of(cb), cols_of(h)] = (o * lax.rsqrt(ms + RMS_EPS) * og * _silu(zh)).astype(o_ref.dtype)
        return run

    state_stages = []
    for cb in range(chunks_per_step):
        state_stages += [read_stage(cb), update_stage(cb)]

    stride = max(1, len(factor_stages) // (len(state_stages) + 1))
    for idx, stage in enumerate(factor_stages):
        stage()
        if idx % stride == stride - 1 and state_stages:
            state_stages.pop(0)()
    for stage in state_stages:
        stage()
    for h in range(g_heads):
        state_ref[h] = states[h]


def _delta_rule(qkv, z, ba, gate_par, o_gain, qk_width, heads_per_step=8, chunks_per_step=4):
    bsz, seq, _ = qkv.shape
    v_width = z.shape[-1]
    n_vheads = v_width // HEAD
    g = min(heads_per_step, n_vheads)
    assert g % DELTA_PACK == 0 and n_vheads % g == 0
    gk = g // 2
    cbs = chunks_per_step
    rows = cbs * CHUNK
    n_blocks = seq // rows
    n_kblocks = qk_width // (gk * HEAD)
    kern = functools.partial(_delta_kernel, heads_per_step=g, chunks_per_step=cbs, n_vheads=n_vheads)
    cur = lambda s: jnp.minimum(s, n_blocks - 1)
    old = lambda s: jnp.maximum(s - 1, 0)
    return pl.pallas_call(
        kern,
        grid=(bsz, n_vheads // g, n_blocks + 1),
        in_specs=[
            pl.BlockSpec((None, rows, gk * HEAD), lambda b, hg, s: (b, cur(s), hg)),
            pl.BlockSpec((None, rows, gk * HEAD), lambda b, hg, s: (b, cur(s), n_kblocks + hg)),
            pl.BlockSpec((None, rows, g * HEAD), lambda b, hg, s: (b, cur(s), 2 * qk_width // (g * HEAD) + hg)),
            pl.BlockSpec((None, rows, g * HEAD), lambda b, hg, s: (b, old(s), hg)),
            pl.BlockSpec((None, rows, HEAD), lambda b, hg, s: (b, cur(s), 0)),
            pl.BlockSpec((2, HEAD), lambda b, hg, s: (0, 0)),
            pl.BlockSpec((1, HEAD), lambda b, hg, s: (0, 0)),
        ],
        out_specs=pl.BlockSpec((None, rows, g * HEAD), lambda b, hg, s: (b, old(s), hg)),
        out_shape=jax.ShapeDtypeStruct((bsz, seq, v_width), BF16),
        scratch_shapes=[
            pltpu.VMEM((g, HEAD, HEAD), F32),
            pltpu.VMEM((2, cbs, g, CHUNK, HEAD), F32),
            pltpu.VMEM((2, cbs, g, CHUNK, HEAD), BF16),
            pltpu.VMEM((2, cbs, g, CHUNK, HEAD), BF16),
            pltpu.VMEM((2, cbs, g, CHUNK, HEAD), BF16),
            pltpu.VMEM((2, cbs, g // DELTA_PACK, DELTA_PACK * CHUNK, DELTA_PACK * CHUNK), BF16),
            pltpu.VMEM((2, cbs, g, 1, HEAD), F32),
        ],
        compiler_params=_cparams(3),
        name="delta_rule",
    )(qkv, qkv, qkv, z, ba, gate_par, o_gain.reshape(1, HEAD))


def _out_proj_kernel(y_ref, w_ref, x_ref, gate_ref, o_ref):
    o_ref[...] = x_ref[...] + gate_ref[...] * _dot(y_ref[...].astype(BF16), w_ref[...])


def _out_proj_norm_kernel(y_ref, w_ref, x_ref, gate_ref, g_ref, o_ref):
    x = x_ref[...] + gate_ref[...] * _dot(y_ref[...].astype(BF16), w_ref[...])
    ms = jnp.mean(x * x, axis=-1, keepdims=True)
    o_ref[...] = x * lax.rsqrt(ms + RMS_EPS) * g_ref[...]


def _out_proj(y, w, layer, x, gate, final_gain=None):
    bsz, seq, k = y.shape
    d = w.shape[2]
    fused_norm = final_gain is not None
    tm = _pick(seq, (512, 256, 128)) if fused_norm else _pick(seq, (1024, 512, 256, 128))
    tn = d if fused_norm else _pick(d, (512, 256, 128))
    in_specs = [
        pl.BlockSpec((None, tm, k), lambda b, i, j: (b, i, 0)),
        pl.BlockSpec((None, k, tn), lambda b, i, j: (layer, 0, j)),
        pl.BlockSpec((None, tm, tn), lambda b, i, j: (b, i, j)),
        pl.BlockSpec((None, 1, tn), lambda b, i, j: (b, 0, j)),
    ]
    args = [y, w, x, gate.reshape(bsz, 1, d)]
    if fused_norm:
        in_specs.append(pl.BlockSpec((1, d), lambda b, i, j: (0, 0)))
        args.append(final_gain.reshape(1, d))
    return pl.pallas_call(
        _out_proj_norm_kernel if fused_norm else _out_proj_kernel,
        grid=(bsz, seq // tm, d // tn),
        in_specs=in_specs,
        out_specs=pl.BlockSpec((None, tm, tn), lambda b, i, j: (b, i, j)),
        out_shape=jax.ShapeDtypeStruct((bsz, seq, d), F32),
        compiler_params=_cparams(3),
        name="out_proj_norm" if fused_norm else "out_proj",
    )(*args)


def _block_slabs(dil, res, nb):
    blocks_per_res = ATT_TILE // (BAND * dil)
    slab = PLANE_ROWS // blocks_per_res
    return [((res + dil * a) * PLANE_ROWS + slab * nb, slab) for a in range(PLANES // dil)]


def _block_positions(dil):
    n_slabs = PLANES // dil
    slab = BAND // n_slabs
    return [n_slabs * i + a for a in range(n_slabs) for i in range(slab)]


def _attn_blocks():
    blocks = []
    for gi, (_, dil) in enumerate(DILATION_GROUPS):
        blocks_per_res = ATT_TILE // (BAND * dil)
        for res in range(dil):
            for nb in range(blocks_per_res):
                prev_nb = (nb - 1) % blocks_per_res
                blocks.append((gi, _block_slabs(dil, res, nb), _block_slabs(dil, res, prev_nb), nb == 0))
    return blocks


def _gather_rows(ref, slabs):
    parts = [ref[start:start + size, :] for start, size in slabs]
    return parts[0] if len(parts) == 1 else jnp.concatenate(parts, axis=0)


def _scatter_rows(ref, slabs, val):
    off = 0
    for start, size in slabs:
        ref[start:start + size, :] = val[off:off + size, :].astype(ref.dtype)
        off += size


def _attn_kernel(q0_ref, q1_ref, q2_ref, kp_ref, ko_ref, vp_ref, vo_ref, z_ref, bias_ref, o_ref,
                 m_ref, l_ref, acc_ref):
    first_tile = pl.program_id(1) == 0
    col = lax.broadcasted_iota(jnp.int32, (1, 2 * BAND), 1)
    head_penalty = jnp.where((col < BAND) & first_tile, NEG, 0.0)
    scale = HEAD ** -0.5 * LOG2_E
    q_refs = (q0_ref, q1_ref, q2_ref)
    last_group = len(DILATION_GROUPS) - 1

    def score_stage(blk):
        gi, q_slabs, prev_slabs, prev_tile = blk
        q16 = _gather_rows(q_refs[gi], q_slabs).astype(BF16)
        k_prev = _gather_rows(kp_ref if prev_tile else ko_ref, prev_slabs)
        k16 = jnp.concatenate([k_prev, _gather_rows(ko_ref, q_slabs)], axis=0).astype(BF16)
        return _dot_nt(q16, k16)

    def softmax_stage(blk, s):
        gi, _, _, prev_tile = blk
        bias = bias_ref[gi]
        if prev_tile:
            bias = bias + head_penalty
        s = s * scale + bias
        m = jnp.max(s, axis=-1, keepdims=True)
        p = jnp.exp2(s - m)
        return m, jnp.sum(p, axis=-1, keepdims=True), p.astype(BF16)

    def value_stage(blk, p16):
        _, q_slabs, prev_slabs, prev_tile = blk
        v_prev = _gather_rows(vp_ref if prev_tile else vo_ref, prev_slabs)
        v16 = jnp.concatenate([v_prev, _gather_rows(vo_ref, q_slabs)], axis=0).astype(BF16)
        return _dot(p16, v16)

    def merge_stage(blk, m, l, pv):
        gi, q_slabs, _, _ = blk
        if gi == 0:
            m_new = jnp.broadcast_to(m, (BAND, HEAD))
            l_new = jnp.broadcast_to(l, (BAND, HEAD))
            acc_new = pv
        else:
            m_old = _gather_rows(m_ref, q_slabs)
            m_new = jnp.maximum(m_old, m)
            w_old = jnp.exp2(m_old - m_new)
            w_blk = jnp.exp2(m - m_new)
            l_new = w_old * _gather_rows(l_ref, q_slabs) + w_blk * l
            acc_new = w_old * _gather_rows(acc_ref, q_slabs) + w_blk * pv
        if gi == last_group:
            _scatter_rows(o_ref, q_slabs, acc_new / l_new * _silu(_gather_rows(z_ref, q_slabs)))
        else:
            _scatter_rows(m_ref, q_slabs, m_new)
            _scatter_rows(l_ref, q_slabs, l_new)
            _scatter_rows(acc_ref, q_slabs, acc_new)

    blocks = _attn_blocks()
    batches = [blocks[i:i + ATT_BATCH] for i in range(0, len(blocks), ATT_BATCH)]
    scores_next = [score_stage(blk) for blk in batches[0]]
    for bi, batch in enumerate(batches):
        scores = scores_next
        if bi + 1 < len(batches):
            scores_next = [score_stage(blk) for blk in batches[bi + 1]]
        stats = [softmax_stage(blk, s) for blk, s in zip(batch, scores)]
        pvs = [value_stage(blk, p16) for blk, (_, _, p16) in zip(batch, stats)]
        for blk, (m, l, _), pv in zip(batch, stats, pvs):
            merge_stage(blk, m, l, pv)


def _dilated_attention(qz, kv, bias):
    bsz, seq, _ = qz.shape
    width = kv.shape[-1] // 2
    heads = width // HEAD
    t = ATT_TILE
    row_tile = lambda b, i, h: (b, i, h)

    def q_spec(gi):
        return pl.BlockSpec((None, t, HEAD), lambda b, i, h: (b, i, gi * heads + h))

    return pl.pallas_call(
        _attn_kernel,
        grid=(bsz, seq // t, heads),
        in_specs=[
            q_spec(0), q_spec(1), q_spec(2),
            pl.BlockSpec((None, t, HEAD), lambda b, i, h: (b, jnp.maximum(i - 1, 0), h)),
            pl.BlockSpec((None, t, HEAD), row_tile),
            pl.BlockSpec((None, t, HEAD), lambda b, i, h: (b, jnp.maximum(i - 1, 0), heads + h)),
            pl.BlockSpec((None, t, HEAD), lambda b, i, h: (b, i, heads + h)),
            pl.BlockSpec((None, t, HEAD), lambda b, i, h: (b, i, 3 * heads + h)),
            pl.BlockSpec((3, None, BAND, 2 * BAND), lambda b, i, h: (0, h, 0, 0)),
        ],
        out_specs=pl.BlockSpec((None, t, HEAD), row_tile),
        out_shape=jax.ShapeDtypeStruct((bsz, seq, width), BF16),
        scratch_shapes=[pltpu.VMEM((t, HEAD), F32)] * 3,
        compiler_params=_cparams(3),
        name="dilated_attention",
    )(qz, qz, qz, kv, kv, kv, kv, qz, bias)


def _t5_bucket(dist):
    max_exact = N_BUCKETS // 2
    n = jnp.maximum(dist, 0)
    large = max_exact + (jnp.log(jnp.maximum(n, 1).astype(F32) / max_exact)
                         / math.log(MAX_DISTANCE / max_exact)
                         * (N_BUCKETS - max_exact)).astype(jnp.int32)
    large = jnp.minimum(large, N_BUCKETS - 1)
    return jnp.where(n < max_exact, n, large)


def _band_bias(rel_bias, heads):
    out = []
    for gi, (_, dil) in enumerate(DILATION_GROUPS):
        pos = jnp.asarray(_block_positions(dil), jnp.int32)
        key_pos = jnp.concatenate([pos, pos + BAND])
        rel = pos[:, None] + BAND - key_pos[None, :]
        valid = (rel >= 0) & (rel <= BAND)
        table = rel_bias[:, gi * heads:(gi + 1) * heads].astype(F32)
        onehot = (_t5_bucket(rel * dil)[None] == jnp.arange(N_BUCKETS)[:, None, None]).astype(F32)
        b = jnp.einsum("nh,nqk->hqk", table, onehot, precision=lax.Precision.HIGHEST)
        out.append(jnp.where(valid[None], b * LOG2_E, NEG))
    return jnp.stack(out, axis=0)


def _permute_kernel(x_ref, o_ref):
    for r in range(PLANES):
        o_ref[r * PLANE_ROWS:(r + 1) * PLANE_ROWS, :] = x_ref[pl.ds(r, PLANE_ROWS, stride=PLANES), :]


def _unpermute_kernel(x_ref, o_ref):
    for r in range(PLANES):
        o_ref[pl.ds(r, PLANE_ROWS, stride=PLANES), :] = x_ref[r * PLANE_ROWS:(r + 1) * PLANE_ROWS, :]


def _move_rows(x, body, name):
    bsz, seq, d = x.shape
    spec = pl.BlockSpec((None, ATT_TILE, HEAD), lambda b, t, j: (b, t, j))
    return pl.pallas_call(
        body,
        grid=(bsz, seq // ATT_TILE, d // HEAD),
        in_specs=[spec],
        out_specs=spec,
        out_shape=jax.ShapeDtypeStruct((bsz, seq, d), x.dtype),
        compiler_params=_cparams(3),
        name=name,
    )(x)


def kernel(x, c, norm_gain, w_mod, b_mod, w_in_a, conv_w_a, a_log, dt_bias, o_norm_a, w_out_a,
           kv_gain, w_kv_mod, b_kv_mod, w_kv, w_in_b, w_out_b, rel_bias, final_gain):
    bsz, seq, d = x.shape
    depth = norm_gain.shape[0]
    n_a = w_in_a.shape[0]
    qk_width = d
    v_width = w_out_a.shape[1]
    n_vheads = v_width // HEAD
    conv_ch = 2 * qk_width + v_width
    dil_width = w_out_b.shape[1]
    dil_heads = dil_width // HEAD
    assert seq % ATT_TILE == 0 and seq % CHUNK == 0 and 2 * n_vheads <= HEAD

    mods = _mod_proj(c, w_mod, b_mod)
    kv_mod = _mod_proj(c, w_kv_mod[None], b_kv_mod[None])[0]
    bias = _band_bias(rel_bias, dil_heads)

    w_in_a16, w_out_a16 = w_in_a.astype(BF16), w_out_a.astype(BF16)
    w_in_b16, w_out_b16 = w_in_b.astype(BF16), w_out_b.astype(BF16)
    w_ba16 = jnp.pad(w_in_a[:, :, conv_ch + v_width:].astype(BF16), ((0, 0), (0, 0), (0, HEAD - 2 * n_vheads)))

    kv = None
    for layer in range(depth):
        shift, scale, gate = mods[layer, :, :d], mods[layer, :, d:2 * d], mods[layer, :, 2 * d:]
        if layer < n_a:
            qkv = _norm_matmul_conv(x, norm_gain[layer], shift, scale, w_in_a16, layer, conv_w_a[layer], qk_width)
            z, ba = _norm_matmul(x, norm_gain[layer], shift, scale, w_in_a16, layer, col_start=conv_ch,
                                 n_cols=v_width, w_side=w_ba16)
            gate_par = jnp.zeros((2, HEAD), F32)
            gate_par = gate_par.at[0, n_vheads:2 * n_vheads].set(a_log[layer])
            gate_par = gate_par.at[1, n_vheads:2 * n_vheads].set(dt_bias[layer])
            y = _delta_rule(qkv, z, ba, gate_par, o_norm_a[layer], qk_width)
            x = _out_proj(y, w_out_a16, layer, x, gate)
        else:
            if kv is None:
                x = _move_rows(x, _permute_kernel, "permute_rows")
                kv = _norm_matmul(x, kv_gain, kv_mod[:, :d], kv_mod[:, d:], w_kv.astype(BF16))
            j = layer - n_a
            qz = _norm_matmul(x, norm_gain[layer], shift, scale, w_in_b16, j)
            y = _dilated_attention(qz, kv, bias)
            x = _out_proj(y, w_out_b16, j, x, gate, final_gain if layer == depth - 1 else None)
    assert kv is not None and depth > n_a
    return _move_rows(x, _unpermute_kernel, "unpermute_rows")
```

```python
import functools
import math

import jax
import jax.numpy as jnp
from jax import lax
from jax.experimental import pallas as pl
from jax.experimental.pallas import tpu as pltpu

F32 = jnp.float32
BF16 = jnp.bfloat16

RMS_EPS = 1e-6
L2_EPS = 1e-6
HEAD = 128
BF16_SUBLANES = 16
CHUNK = 64
DELTA_PACK = 4
INV_BASE = 8
CONV_WIDTH = 4
HALO = 8
DILATION_GROUPS =((128, 1), (512, 4), (2048, 16))
BAND = 128
ATT_TILE = 2048
PLANES = 16
PLANE_ROWS = ATT_TILE // PLANES
ATT_BATCH = 4
N_BUCKETS = 32
MAX_DISTANCE = 2048
NEG = -1e30
LOG2_E = math.log2(math.e)
VMEM_LIMIT = 56 * 1024 * 1024


def _cparams(n_axes):
    return pltpu.CompilerParams(
        dimension_semantics=("arbitrary",) * n_axes, vmem_limit_bytes=VMEM_LIMIT)


def _dot(a, b):
    return jnp.dot(a, b, preferred_element_type=F32)


def _dot_nt(a, b):
    return lax.dot_general(a, b, (((1,), (1,)), ((), ())), preferred_element_type=F32)


def _dot_tn(a, b):
    return lax.dot_general(a, b, (((0,), (0,)), ((), ())), preferred_element_type=F32)


def _sigmoid(x):
    return 1.0 / (1.0 + jnp.exp(-x))


def _silu(x):
    return x * _sigmoid(x)


def _softplus(x):
    return jnp.maximum(x, 0.0) + jnp.log(1.0 + jnp.exp(-jnp.abs(x)))


def _split3(x):
    hi = x.astype(BF16)
    r1 = x - hi.astype(F32)
    mid = r1.astype(BF16)
    lo = (r1 - mid.astype(F32)).astype(BF16)
    return hi, mid, lo


def _pick(n, prefs):
    for p in prefs:
        if n % p == 0:
            return p
    return n


def _mod_kernel(c_ref, w_ref, b_ref, o_ref):
    ca = _silu(c_ref[...])
    w = w_ref[...]
    c_hi, c_mid, c_lo = _split3(ca)
    w_hi = w.astype(BF16)
    w_lo = (w - w_hi.astype(F32)).astype(BF16)
    acc = _dot(c_hi, w_hi) + _dot(c_hi, w_lo) + _dot(c_mid, w_hi) + _dot(c_lo, w_hi)
    o_ref[...] = acc + b_ref[...]


def _mod_proj(c, w, b):
    nl, d, n = w.shape
    n_rows = c.shape[0]
    bsz = -(-n_rows // BF16_SUBLANES) * BF16_SUBLANES
    c = jnp.pad(c, ((0, bsz - n_rows), (0, 0)))
    tn = _pick(n, (512, 256, 128))
    out = pl.pallas_call(
        _mod_kernel,
        grid=(nl, n // tn),
        in_specs=[
            pl.BlockSpec((bsz, d), lambda l, j: (0, 0)),
            pl.BlockSpec((None, d, tn), lambda l, j: (l, 0, j)),
            pl.BlockSpec((None, 1, tn), lambda l, j: (l, 0, j)),
        ],
        out_specs=pl.BlockSpec((None, bsz, tn), lambda l, j: (l, 0, j)),
        out_shape=jax.ShapeDtypeStruct((nl, bsz, n), F32),
        compiler_params=_cparams(2),
        name="mod_proj",
    )(c, w, b.reshape(nl, 1, n))
    return out[:, :n_rows]


def _modnorm(x, gain, shift, scale):
    ms = jnp.mean(x * x, axis=-1, keepdims=True)
    return x * lax.rsqrt(ms + RMS_EPS) * (gain * (1.0 + scale)) + shift


def _nmm_plain_kernel(x_ref, g_ref, sh_ref, sc_ref, w_ref, o_ref, h_ref):
    @pl.when(pl.program_id(2) == 0)
    def _():
        h_ref[...] = _modnorm(x_ref[...], g_ref[...], sh_ref[...], sc_ref[...]).astype(BF16)

    o_ref[...] = _dot(h_ref[...], w_ref[...]).astype(o_ref.dtype)


def _nmm_side_kernel(x_ref, g_ref, sh_ref, sc_ref, w_ref, ws_ref, o_ref, os_ref, h_ref):
    @pl.when(pl.program_id(2) == 0)
    def _():
        h_ref[...] = _modnorm(x_ref[...], g_ref[...], sh_ref[...], sc_ref[...]).astype(BF16)
        os_ref[...] = _dot(h_ref[...], ws_ref[...])

    o_ref[...] = _dot(h_ref[...], w_ref[...]).astype(o_ref.dtype)


def _norm_matmul(x, gain, shift, scale, w, layer=0, col_start=0, n_cols=None, out_dtype=F32, w_side=None):
    bsz, seq, d = x.shape
    w = w[None] if w.ndim == 2 else w
    n = w.shape[2] - col_start if n_cols is None else n_cols
    tm = _pick(seq, (1024, 512, 256, 128))
    tn = _pick(math.gcd(n, col_start) if col_start else n, (512, 256, 128))
    first = col_start // tn
    in_specs = [
        pl.BlockSpec((None, tm, d), lambda b, i, j: (b, i, 0)),
        pl.BlockSpec((1, d), lambda b, i, j: (0, 0)),
        pl.BlockSpec((None, 1, d), lambda b, i, j: (b, 0, 0)),
        pl.BlockSpec((None, 1, d), lambda b, i, j: (b, 0, 0)),
        pl.BlockSpec((None, d, tn), lambda b, i, j: (layer, 0, first + j)),
    ]
    args = [x, gain.reshape(1, d), shift.reshape(bsz, 1, d), scale.reshape(bsz, 1, d), w]
    out_specs = pl.BlockSpec((None, tm, tn), lambda b, i, j: (b, i, j))
    out_shape = jax.ShapeDtypeStruct((bsz, seq, n), out_dtype)
    if w_side is not None:
        in_specs.append(pl.BlockSpec((None, d, HEAD), lambda b, i, j: (layer, 0, 0)))
        args.append(w_side)
        out_specs = [out_specs, pl.BlockSpec((None, tm, HEAD), lambda b, i, j: (b, i, 0))]
        out_shape = [out_shape, jax.ShapeDtypeStruct((bsz, seq, HEAD), F32)]
    return pl.pallas_call(
        _nmm_plain_kernel if w_side is None else _nmm_side_kernel,
        grid=(bsz, seq // tm, n // tn),
        in_specs=in_specs,
        out_specs=out_specs,
        out_shape=out_shape,
        scratch_shapes=[pltpu.VMEM((tm, d), BF16)],
        compiler_params=_cparams(3),
        name="norm_matmul",
    )(*args)


def _nmm_conv_kernel(x_ref, g_ref, sh_ref, sc_ref, w_ref, cw_ref, o_ref, h_ref, halo_ref, acc0_ref, acc1_ref,
                     *, n_q_tiles, n_qk_tiles, n_col_tiles, n_steps):
    s = pl.program_id(1)
    tm, tn = o_ref.shape
    jt = jnp.maximum(s - 1, 0) % n_col_tiles

    @pl.when((s % n_col_tiles == 0) & (s < n_steps - 1))
    def _():
        h_ref[...] = _modnorm(x_ref[...], g_ref[...], sh_ref[...], sc_ref[...]).astype(BF16)

    @pl.when(s == 0)
    def _():
        acc1_ref[...] = jnp.zeros(acc1_ref.shape, F32)
        halo_ref[...] = jnp.zeros(halo_ref.shape, F32)

    def step(done_ref, next_ref):
        prev_rows = halo_ref[jt]
        done_ref[0:HALO, :] = prev_rows
        acc = done_ref[HALO:, :]
        halo_ref[jt] = jnp.where(s > 0, acc[tm - HALO:, :], prev_rows)
        cw = cw_ref[...]
        y = cw[3:4, :] * acc
        for back in (1, 2, 3):
            y = y + cw[3 - back:4 - back, :] * done_ref[pl.ds(HALO - back, tm), :]
        y = _silu(y)
        is_qk = jt < n_qk_tiles
        qscale = jnp.where(jt < n_q_tiles, HEAD ** -0.5, 1.0).astype(F32)
        for hh in range(tn // HEAD):
            yh = y[:, hh * HEAD:(hh + 1) * HEAD]
            ss = jnp.sum(yh * yh, axis=-1, keepdims=True)
            o_ref[:, hh * HEAD:(hh + 1) * HEAD] = yh * jnp.where(is_qk, lax.rsqrt(ss + L2_EPS) * qscale, 1.0)
        next_ref[HALO:, :] = _dot(h_ref[...], w_ref[...])

    @pl.when(s % 2 == 0)
    def _():
        step(acc1_ref, acc0_ref)

    @pl.when(s % 2 == 1)
    def _():
        step(acc0_ref, acc1_ref)


def _norm_matmul_conv(x, gain, shift, scale, w, layer, conv_w, qk_width):
    bsz, seq, d = x.shape
    n = conv_w.shape[1]
    tm = _pick(seq, (1024, 512, 256, 128))
    tn = _pick(qk_width, (512, 256, 128))
    n_tiles = n // tn
    n_steps = (seq // tm) * n_tiles + 1
    kern = functools.partial(_nmm_conv_kernel, n_q_tiles=qk_width // tn, n_qk_tiles=2 * qk_width // tn,
                             n_col_tiles=n_tiles, n_steps=n_steps)
    cur = lambda s: jnp.minimum(s, n_steps - 2)
    old = lambda s: jnp.maximum(s - 1, 0)
    return pl.pallas_call(
        kern,
        grid=(bsz, n_steps),
        in_specs=[
            pl.BlockSpec((None, tm, d), lambda b, s: (b, cur(s) // n_tiles, 0)),
            pl.BlockSpec((1, d), lambda b, s: (0, 0)),
            pl.BlockSpec((None, 1, d), lambda b, s: (b, 0, 0)),
            pl.BlockSpec((None, 1, d), lambda b, s: (b, 0, 0)),
            pl.BlockSpec((None, d, tn), lambda b, s: (layer, 0, cur(s) % n_tiles)),
            pl.BlockSpec((CONV_WIDTH, tn), lambda b, s: (0, old(s) % n_tiles)),
        ],
        out_specs=pl.BlockSpec((None, tm, tn), lambda b, s: (b, old(s) // n_tiles, old(s) % n_tiles)),
        out_shape=jax.ShapeDtypeStruct((bsz, seq, n), F32),
        scratch_shapes=[pltpu.VMEM((tm, d), BF16), pltpu.VMEM((n_tiles, HALO, tn), F32),
                        pltpu.VMEM((HALO + tm, tn), F32), pltpu.VMEM((HALO + tm, tn), F32)],
        compiler_params=_cparams(2),
        name="norm_matmul_conv",
    )(x, gain.reshape(1, d), shift.reshape(bsz, 1, d), scale.reshape(bsz, 1, d), w, conv_w)


def _delta_kernel(q_ref, k_ref, v_ref, z_ref, ba_ref, par_ref, og_ref, o_ref,
                  state_ref, u_ref, w_ref, qd_ref, kd_ref, at_ref, a_ref,
                  *, heads_per_step, chunks_per_step, n_vheads, n_blocks, n_steps):
    s = pl.program_id(1)
    hg = jnp.minimum(s, n_steps - 2) // n_blocks
    new_group = jnp.maximum(s - 1, 0) % n_blocks == 0
    g_heads = heads_per_step
    c = CHUNK
    slot = s % 2
    prev = 1 - slot

    @pl.when(s == 0)
    def _():
        state_ref[...] = jnp.zeros(state_ref.shape, F32)
        u_ref[1] = jnp.zeros(u_ref.shape[1:], F32)
        w_ref[1] = jnp.zeros(w_ref.shape[1:], BF16)
        qd_ref[1] = jnp.zeros(qd_ref.shape[1:], BF16)
        kd_ref[1] = jnp.zeros(kd_ref.shape[1:], BF16)
        at_ref[1] = jnp.zeros(at_ref.shape[1:], BF16)
        a_ref[1] = jnp.zeros(a_ref.shape[1:], F32)

    row = lax.broadcasted_iota(jnp.int32, (c, c), 0)
    col = lax.broadcasted_iota(jnp.int32, (c, c), 1)
    lower = row >= col
    strict = row > col
    eye = row == col
    lane = lax.broadcasted_iota(jnp.int32, (c, HEAD), 1)
    tril = jnp.where(lower, 1.0, 0.0).astype(BF16)

    items = [(cb, h) for cb in range(chunks_per_step) for h in range(g_heads)]
    rows_of = lambda cb: slice(cb * c, (cb + 1) * c)
    cols_of = lambda i: slice(i * HEAD, (i + 1) * HEAD)

    pcols = DELTA_PACK * c
    prow = lax.broadcasted_iota(jnp.int32, (c, pcols), 0)
    plane = lax.broadcasted_iota(jnp.int32, (c, pcols), 1)
    pcol = jnp.bitwise_and(plane, c - 1)
    pblk = jnp.right_shift(plane, c.bit_length() - 1)
    p_lower = prow >= pcol
    p_strict = prow > pcol
    p_eye = prow == pcol
    in_block = [pblk == i for i in range(DELTA_PACK)]
    packs = [(cb, qd) for cb in range(chunks_per_step) for qd in range(g_heads // DELTA_PACK)]
    heads_of = lambda qd: [DELTA_PACK * qd + i for i in range(DELTA_PACK)]

    def pack_columns(cols):
        out = cols[-1]
        for i in reversed(range(DELTA_PACK - 1)):
            out = jnp.where(in_block[i], cols[i], out)
        return out

    block_keep = [jnp.where(in_block[i], 1.0, 0.0).astype(BF16) for i in range(DELTA_PACK)]

    def block_diag(mat):
        mat16 = mat.astype(BF16)
        return jnp.concatenate([mat16 * block_keep[i] for i in range(DELTA_PACK)], axis=0)

    def same_block(size):
        shift = size.bit_length() - 1
        return jnp.right_shift(prow, shift) == jnp.right_shift(pcol, shift)

    base_mask = same_block(INV_BASE)
    join_mask = {}
    size = INV_BASE
    while size < c:
        join_mask[size] = same_block(2 * size) & ~same_block(size)
        size *= 2

    comb, kk, qk = {}, {}, {}
    xm, rhs, fac, lfull = {}, {}, {}, {}

    def gates_stage():
        for cb in range(chunks_per_step):
            ba = ba_ref[rows_of(cb), :]
            beta_full = _sigmoid(ba)
            g_full = -jnp.exp(par_ref[0:1, :]) * _softplus(ba + par_ref[1:2, :])
            g_hi, g_mid, g_lo = _split3(g_full)
            gc_full = _dot(tril, g_hi) + _dot(tril, g_mid) + _dot(tril, g_lo)
            comb[cb] = jnp.where(lane < n_vheads, beta_full, gc_full)

    def gram_stage():
        for cb in range(chunks_per_step):
            for kh_idx in range(g_heads // 2):
                k16 = k_ref[rows_of(cb), cols_of(kh_idx)].astype(BF16)
                q16 = q_ref[rows_of(cb), cols_of(kh_idx)].astype(BF16)
                k_twice = jnp.concatenate([k16, k16], axis=0)
                both = _dot_nt(jnp.concatenate([k16, q16], axis=0), k_twice)
                kk[cb, kh_idx] = both[:c, :]
                qk[cb, kh_idx] = both[c:, :]

    def decay_stage():
        for cb, qd in packs:
            betas, gcs = [], []
            for h in heads_of(qd):
                head = hg * g_heads + h
                kh = k_ref[rows_of(cb), cols_of(h // 2)]
                qh = q_ref[rows_of(cb), cols_of(h // 2)]
                vh = v_ref[rows_of(cb), cols_of(h)]
                beta_c = jnp.sum(jnp.where(lane == head, comb[cb], 0.0), axis=-1, keepdims=True)
                gc_c = jnp.sum(jnp.where(lane == head + n_vheads, comb[cb], 0.0), axis=-1, keepdims=True)
                gl = gc_c[c - 1:c, :]
                eg = jnp.exp(gc_c)
                rhs[cb, h] = jnp.concatenate([vh * beta_c, kh * (beta_c * eg)], axis=1)
                qd_ref[slot, cb, h] = (qh * eg).astype(BF16)
                kd_ref[slot, cb, h] = (kh * jnp.exp(gl - gc_c)).astype(BF16)
                a_ref[slot, cb, h] = jnp.broadcast_to(jnp.exp(gl), (1, HEAD))
                betas.append(beta_c)
                gcs.append(gc_c)
            first_kh = DELTA_PACK * qd // 2
            kk_p = jnp.concatenate([kk[cb, first_kh + i] for i in range(DELTA_PACK // 2)], axis=1)
            qk_p = jnp.concatenate([qk[cb, first_kh + i] for i in range(DELTA_PACK // 2)], axis=1)
            gc_p = pack_columns(gcs)
            gc_r = jnp.sum(jnp.where(p_eye, gc_p, 0.0), axis=0, keepdims=True)
            decay = jnp.exp(jnp.where(p_lower, gc_p - gc_r, NEG))
            lmat = jnp.where(p_strict, kk_p * pack_columns(betas) * decay, 0.0)
            at_ref[slot, cb, qd] = block_diag(qk_p * decay)
            lfull[cb, qd] = lmat
            ldiag = jnp.where(base_mask, lmat, 0.0)
            xm[cb, qd] = -ldiag
            fac[cb, qd] = ldiag

    def base_square_stage():
        for pk in packs:
            fac[pk] = _dot(fac[pk].astype(BF16), block_diag(fac[pk]))

    def base_round_stage():
        for pk in packs:
            both = jnp.concatenate([fac[pk].astype(BF16), xm[pk].astype(BF16)], axis=0)
            out = _dot(both, block_diag(fac[pk]))
            xm[pk] = xm[pk] + fac[pk] + out[c:, :]
            fac[pk] = out[:c, :]

    def base_last_stage():
        for pk in packs:
            xm[pk] = xm[pk] + fac[pk] + _dot(xm[pk].astype(BF16), block_diag(fac[pk]))

    def join_right_stage(size):
        def run():
            for pk in packs:
                joint = jnp.where(join_mask[size], lfull[pk], 0.0)
                fac[pk] = joint + _dot(joint.astype(BF16), block_diag(xm[pk]))
        return run

    def join_left_stage():
        for pk in packs:
            xm[pk] = xm[pk] - fac[pk] - _dot(xm[pk].astype(BF16), block_diag(fac[pk]))

    def solve_stage():
        for cb, qd in packs:
            rhs_rows = jnp.concatenate([rhs[cb, h].astype(BF16) for h in heads_of(qd)], axis=0)
            sol = _dot(block_diag(xm[cb, qd]), rhs_rows)
            for i, h in enumerate(heads_of(qd)):
                uw = rhs[cb, h] + sol[i * c:(i + 1) * c, :]
                u_ref[slot, cb, h] = uw[:, :HEAD]
                w_ref[slot, cb, h] = uw[:, HEAD:].astype(BF16)

    factor_stages = [gates_stage, gram_stage, decay_stage, base_square_stage]
    factor_stages += [base_round_stage] * (INV_BASE.bit_length() - 3)
    factor_stages.append(base_last_stage)
    size = INV_BASE
    while size < c:
        factor_stages += [join_right_stage(size), join_left_stage]
        size *= 2
    factor_stages.append(solve_stage)

    og = og_ref[...]
    states = [jnp.where(new_group, 0.0, state_ref[h]) for h in range(g_heads)]
    wqs, v16 = {}, {}

    def read_stage(cb):
        def run():
            for h in range(g_heads):
                wq = jnp.concatenate([w_ref[prev, cb, h], qd_ref[prev, cb, h]], axis=0)
                wqs[h] = _dot(wq, states[h].astype(BF16))
        return run

    def update_stage(cb):
        def run():
            for h in range(g_heads):
                v16[h] = (u_ref[prev, cb, h] - wqs[h][:c, :]).astype(BF16)
            intra = {}
            for qd in range(g_heads // DELTA_PACK):
                v_rows = jnp.concatenate([v16[h] for h in heads_of(qd)], axis=0)
                av = _dot(at_ref[prev, cb, qd], v_rows)
                for i, h in enumerate(heads_of(qd)):
                    intra[h] = av[i * c:(i + 1) * c, :]
            for h in range(g_heads):
                o = wqs[h][c:, :] + intra[h]
                states[h] = states[h] * a_ref[prev, cb, h] + _dot_tn(kd_ref[prev, cb, h], v16[h])
                ms = jnp.mean(o * o, axis=-1, keepdims=True)
                zh = z_ref[rows_of(cb), cols_of(h)]
                o_ref[rows_of(cb), cols_of(h)] = (o * lax.rsqrt(ms + RMS_EPS) * og * _silu(zh)).astype(o_ref.dtype)
        return run

    state_stages = []
    for cb in range(chunks_per_step):
        state_stages += [read_stage(cb), update_stage(cb)]

    stride = max(1, len(factor_stages) // (len(state_stages) + 1))
    for idx, stage in enumerate(factor_stages):
        stage()
        if idx % stride == stride - 1 and state_stages:
            state_stages.pop(0)()
    for stage in state_stages:
        stage()
    for h in range(g_heads):
        state_ref[h] = states[h]


def _delta_rule(qkv, z, ba, gate_par, o_gain, qk_width, heads_per_step=8, chunks_per_step=4):
    bsz, seq, _ = qkv.shape
    v_width = z.shape[-1]
    n_vheads = v_width // HEAD
    g = min(heads_per_step, n_vheads)
    assert g % DELTA_PACK == 0 and n_vheads % g == 0
    gk = g // 2
    cbs = chunks_per_step
    rows = cbs * CHUNK
    n_blocks = seq // rows
    n_kblocks = qk_width // (gk * HEAD)
    n_steps = (n_vheads // g) * n_blocks + 1
    kern = functools.partial(_delta_kernel, heads_per_step=g, chunks_per_step=cbs, n_vheads=n_vheads,
                             n_blocks=n_blocks, n_steps=n_steps)
    cur = lambda s: jnp.minimum(s, n_steps - 2)
    old = lambda s: jnp.maximum(s - 1, 0)
    blk = lambda t: t % n_blocks
    grp = lambda t: t // n_blocks
    v_first = 2 * qk_width // (g * HEAD)
    return pl.pallas_call(
        kern,
        grid=(bsz, n_steps),
        in_specs=[
            pl.BlockSpec((None, rows, gk * HEAD), lambda b, s: (b, blk(cur(s)), grp(cur(s)))),
            pl.BlockSpec((None, rows, gk * HEAD), lambda b, s: (b, blk(cur(s)), n_kblocks + grp(cur(s)))),
            pl.BlockSpec((None, rows, g * HEAD), lambda b, s: (b, blk(cur(s)), v_first + grp(cur(s)))),
            pl.BlockSpec((None, rows, g * HEAD), lambda b, s: (b, blk(old(s)), grp(old(s)))),
            pl.BlockSpec((None, rows, HEAD), lambda b, s: (b, blk(cur(s)), 0)),
            pl.BlockSpec((2, HEAD), lambda b, s: (0, 0)),
            pl.BlockSpec((1, HEAD), lambda b, s: (0, 0)),
        ],
        out_specs=pl.BlockSpec((None, rows, g * HEAD), lambda b, s: (b, blk(old(s)), grp(old(s)))),
        out_shape=jax.ShapeDtypeStruct((bsz, seq, v_width), BF16),
        scratch_shapes=[
            pltpu.VMEM((g, HEAD, HEAD), F32),
            pltpu.VMEM((2, cbs, g, CHUNK, HEAD), F32),
            pltpu.VMEM((2, cbs, g, CHUNK, HEAD), BF16),
            pltpu.VMEM((2, cbs, g, CHUNK, HEAD), BF16),
            pltpu.VMEM((2, cbs, g, CHUNK, HEAD), BF16),
            pltpu.VMEM((2, cbs, g // DELTA_PACK, DELTA_PACK * CHUNK, DELTA_PACK * CHUNK), BF16),
            pltpu.VMEM((2, cbs, g, 1, HEAD), F32),
        ],
        compiler_params=_cparams(2),
        name="delta_rule",
    )(qkv, qkv, qkv, z, ba, gate_par, o_gain.reshape(1, HEAD))


def _out_proj_kernel(y_ref, w_ref, x_ref, gate_ref, o_ref):
    o_ref[...] = x_ref[...] + gate_ref[...] * _dot(y_ref[...].astype(BF16), w_ref[...])


def _out_proj_norm_kernel(y_ref, w_ref, x_ref, gate_ref, g_ref, o_ref):
    x = x_ref[...] + gate_ref[...] * _dot(y_ref[...].astype(BF16), w_ref[...])
    ms = jnp.mean(x * x, axis=-1, keepdims=True)
    o_ref[...] = x * lax.rsqrt(ms + RMS_EPS) * g_ref[...]


def _out_proj(y, w, layer, x, gate, final_gain=None):
    bsz, seq, k = y.shape
    d = w.shape[2]
    fused_norm = final_gain is not None
    tm = _pick(seq, (512, 256, 128)) if fused_norm else _pick(seq, (1024, 512, 256, 128))
    tn = d if fused_norm else _pick(d, (512, 256, 128))
    in_specs = [
        pl.BlockSpec((None, tm, k), lambda b, i, j: (b, i, 0)),
        pl.BlockSpec((None, k, tn), lambda b, i, j: (layer, 0, j)),
        pl.BlockSpec((None, tm, tn), lambda b, i, j: (b, i, j)),
        pl.BlockSpec((None, 1, tn), lambda b, i, j: (b, 0, j)),
    ]
    args = [y, w, x, gate.reshape(bsz, 1, d)]
    if fused_norm:
        in_specs.append(pl.BlockSpec((1, d), lambda b, i, j: (0, 0)))
        args.append(final_gain.reshape(1, d))
    return pl.pallas_call(
        _out_proj_norm_kernel if fused_norm else _out_proj_kernel,
        grid=(bsz, seq // tm, d // tn),
        in_specs=in_specs,
        out_specs=pl.BlockSpec((None, tm, tn), lambda b, i, j: (b, i, j)),
        out_shape=jax.ShapeDtypeStruct((bsz, seq, d), F32),
        compiler_params=_cparams(3),
        name="out_proj_norm" if fused_norm else "out_proj",
    )(*args)


def _block_slabs(dil, res, nb):
    blocks_per_res = ATT_TILE // (BAND * dil)
    slab = PLANE_ROWS // blocks_per_res
    return [((res + dil * a) * PLANE_ROWS + slab * nb, slab) for a in range(PLANES // dil)]


def _block_positions(dil):
    n_slabs = PLANES // dil
    slab = BAND // n_slabs
    return [n_slabs * i + a for a in range(n_slabs) for i in range(slab)]


def _attn_blocks():
    blocks = []
    for gi, (_, dil) in enumerate(DILATION_GROUPS):
        blocks_per_res = ATT_TILE // (BAND * dil)
        for res in range(dil):
            for nb in range(blocks_per_res):
                prev_nb = (nb - 1) % blocks_per_res
                blocks.append((gi, _block_slabs(dil, res, nb), _block_slabs(dil, res, prev_nb), nb == 0))
    return blocks


def _gather_rows(ref, slabs):
    parts = [ref[start:start + size, :] for start, size in slabs]
    return parts[0] if len(parts) == 1 else jnp.concatenate(parts, axis=0)


def _scatter_rows(ref, slabs, val):
    off = 0
    for start, size in slabs:
        ref[start:start + size, :] = val[off:off + size, :].astype(ref.dtype)
        off += size


def _attn_kernel(q0_ref, q1_ref, q2_ref, kp_ref, ko_ref, vp_ref, vo_ref, z_ref, bias_ref, o_ref,
                 m_ref, l_ref, acc_ref):
    first_tile = pl.program_id(1) == 0
    col = lax.broadcasted_iota(jnp.int32, (1, 2 * BAND), 1)
    head_penalty = jnp.where((col < BAND) & first_tile, NEG, 0.0)
    scale = HEAD ** -0.5 * LOG2_E
    q_refs = (q0_ref, q1_ref, q2_ref)
    last_group = len(DILATION_GROUPS) - 1

    def score_stage(blk):
        gi, q_slabs, prev_slabs, prev_tile = blk
        q16 = _gather_rows(q_refs[gi], q_slabs).astype(BF16)
        k_prev = _gather_rows(kp_ref if prev_tile else ko_ref, prev_slabs)
        k16 = jnp.concatenate([k_prev, _gather_rows(ko_ref, q_slabs)], axis=0).astype(BF16)
        return _dot_nt(q16, k16)

    def softmax_stage(blk, s):
        gi, _, _, prev_tile = blk
        bias = bias_ref[gi]
        if prev_tile:
            bias = bias + head_penalty
        s = s * scale + bias
        m = jnp.max(s, axis=-1, keepdims=True)
        p = jnp.exp2(s - m)
        return m, jnp.sum(p, axis=-1, keepdims=True), p.astype(BF16)

    def value_stage(blk, p16):
        _, q_slabs, prev_slabs, prev_tile = blk
        v_prev = _gather_rows(vp_ref if prev_tile else vo_ref, prev_slabs)
        v16 = jnp.concatenate([v_prev, _gather_rows(vo_ref, q_slabs)], axis=0).astype(BF16)
        return _dot(p16, v16)

    def merge_stage(blk, m, l, pv):
        gi, q_slabs, _, _ = blk
        if gi == 0:
            m_new = jnp.broadcast_to(m, (BAND, HEAD))
            l_new = jnp.broadcast_to(l, (BAND, HEAD))
            acc_new = pv
        else:
            m_old = _gather_rows(m_ref, q_slabs)
            m_new = jnp.maximum(m_old, m)
            w_old = jnp.exp2(m_old - m_new)
            w_blk = jnp.exp2(m - m_new)
            l_new = w_old * _gather_rows(l_ref, q_slabs) + w_blk * l
            acc_new = w_old * _gather_rows(acc_ref, q_slabs) + w_blk * pv
        if gi == last_group:
            _scatter_rows(o_ref, q_slabs, acc_new / l_new * _silu(_gather_rows(z_ref, q_slabs)))
        else:
            _scatter_rows(m_ref, q_slabs, m_new)
            _scatter_rows(l_ref, q_slabs, l_new)
            _scatter_rows(acc_ref, q_slabs, acc_new)

    blocks = _attn_blocks()
    batches = [blocks[i:i + ATT_BATCH] for i in range(0, len(blocks), ATT_BATCH)]
    scores_next = [score_stage(blk) for blk in batches[0]]
    for bi, batch in enumerate(batches):
        scores = scores_next
        if bi + 1 < len(batches):
            scores_next = [score_stage(blk) for blk in batches[bi + 1]]
        stats = [softmax_stage(blk, s) for blk, s in zip(batch, scores)]
        pvs = [value_stage(blk, p16) for blk, (_, _, p16) in zip(batch, stats)]
        for blk, (m, l, _), pv in zip(batch, stats, pvs):
            merge_stage(blk, m, l, pv)


def _dilated_attention(qz, kv, bias):
    bsz, seq, _ = qz.shape
    width = kv.shape[-1] // 2
    heads = width // HEAD
    t = ATT_TILE
    row_tile = lambda b, i, h: (b, i, h)

    def q_spec(gi):
        return pl.BlockSpec((None, t, HEAD), lambda b, i, h: (b, i, gi * heads + h))

    return pl.pallas_call(
        _attn_kernel,
        grid=(bsz, seq // t, heads),
        in_specs=[
            q_spec(0), q_spec(1), q_spec(2),
            pl.BlockSpec((None, t, HEAD), lambda b, i, h: (b, jnp.maximum(i - 1, 0), h)),
            pl.BlockSpec((None, t, HEAD), row_tile),
            pl.BlockSpec((None, t, HEAD), lambda b, i, h: (b, jnp.maximum(i - 1, 0), heads + h)),
            pl.BlockSpec((None, t, HEAD), lambda b, i, h: (b, i, heads + h)),
            pl.BlockSpec((None, t, HEAD), lambda b, i, h: (b, i, 3 * heads + h)),
            pl.BlockSpec((3, None, BAND, 2 * BAND), lambda b, i, h: (0, h, 0, 0)),
        ],
        out_specs=pl.BlockSpec((None, t, HEAD), row_tile),
        out_shape=jax.ShapeDtypeStruct((bsz, seq, width), BF16),
        scratch_shapes=[pltpu.VMEM((t, HEAD), F32)] * 3,
        compiler_params=_cparams(3),
        name="dilated_attention",
    )(qz, qz, qz, kv, kv, kv, kv, qz, bias)


def _t5_bucket(dist):
    max_exact = N_BUCKETS // 2
    n = jnp.maximum(dist, 0)
    large = max_exact + (jnp.log(jnp.maximum(n, 1).astype(F32) / max_exact)
                         / math.log(MAX_DISTANCE / max_exact)
                         * (N_BUCKETS - max_exact)).astype(jnp.int32)
    large = jnp.minimum(large, N_BUCKETS - 1)
    return jnp.where(n < max_exact, n, large)


def _band_bias(rel_bias, heads):
    out = []
    for gi, (_, dil) in enumerate(DILATION_GROUPS):
        pos = jnp.asarray(_block_positions(dil), jnp.int32)
        key_pos = jnp.concatenate([pos, pos + BAND])
        rel = pos[:, None] + BAND - key_pos[None, :]
        valid = (rel >= 0) & (rel <= BAND)
        table = rel_bias[:, gi * heads:(gi + 1) * heads].astype(F32)
        onehot = (_t5_bucket(rel * dil)[None] == jnp.arange(N_BUCKETS)[:, None, None]).astype(F32)
        b = jnp.einsum("nh,nqk->hqk", table, onehot, precision=lax.Precision.HIGHEST)
        out.append(jnp.where(valid[None], b * LOG2_E, NEG))
    return jnp.stack(out, axis=0)


def _permute_kernel(x_ref, o_ref):
    for r in range(PLANES):
        o_ref[r * PLANE_ROWS:(r + 1) * PLANE_ROWS, :] = x_ref[pl.ds(r, PLANE_ROWS, stride=PLANES), :]


def _unpermute_kernel(x_ref, o_ref):
    for r in range(PLANES):
        o_ref[pl.ds(r, PLANE_ROWS, stride=PLANES), :] = x_ref[r * PLANE_ROWS:(r + 1) * PLANE_ROWS, :]


def _move_rows(x, body, name):
    bsz, seq, d = x.shape
    spec = pl.BlockSpec((None, ATT_TILE, HEAD), lambda b, t, j: (b, t, j))
    return pl.pallas_call(
        body,
        grid=(bsz, seq // ATT_TILE, d // HEAD),
        in_specs=[spec],
        out_specs=spec,
        out_shape=jax.ShapeDtypeStruct((bsz, seq, d), x.dtype),
        compiler_params=_cparams(3),
        name=name,
    )(x)


def kernel(x, c, norm_gain, w_mod, b_mod, w_in_a, conv_w_a, a_log, dt_bias, o_norm_a, w_out_a,
           kv_gain, w_kv_mod, b_kv_mod, w_kv, w_in_b, w_out_b, rel_bias, final_gain):
    bsz, seq, d = x.shape
    depth = norm_gain.shape[0]
    n_a = w_in_a.shape[0]
    qk_width = d
    v_width = w_out_a.shape[1]
    n_vheads = v_width // HEAD
    conv_ch = 2 * qk_width + v_width
    dil_width = w_out_b.shape[1]
    dil_heads = dil_width // HEAD
    assert seq % ATT_TILE == 0 and seq % CHUNK == 0 and 2 * n_vheads <= HEAD

    mods = _mod_proj(c, w_mod, b_mod)
    kv_mod = _mod_proj(c, w_kv_mod[None], b_kv_mod[None])[0]
    bias = _band_bias(rel_bias, dil_heads)

    w_in_a16, w_out_a16 = w_in_a.astype(BF16), w_out_a.astype(BF16)
    w_in_b16, w_out_b16 = w_in_b.astype(BF16), w_out_b.astype(BF16)
    w_ba16 = jnp.pad(w_in_a[:, :, conv_ch + v_width:].astype(BF16), ((0, 0), (0, 0), (0, HEAD - 2 * n_vheads)))

    kv = None
    for layer in range(depth):
        shift, scale, gate = mods[layer, :, :d], mods[layer, :, d:2 * d], mods[layer, :, 2 * d:]
        if layer < n_a:
            qkv = _norm_matmul_conv(x, norm_gain[layer], shift, scale, w_in_a16, layer, conv_w_a[layer], qk_width)
            z, ba = _norm_matmul(x, norm_gain[layer], shift, scale, w_in_a16, layer, col_start=conv_ch,
                                 n_cols=v_width, w_side=w_ba16)
            gate_par = jnp.zeros((2, HEAD), F32)
            gate_par = gate_par.at[0, n_vheads:2 * n_vheads].set(a_log[layer])
            gate_par = gate_par.at[1, n_vheads:2 * n_vheads].set(dt_bias[layer])
            y = _delta_rule(qkv, z, ba, gate_par, o_norm_a[layer], qk_width)
            x = _out_proj(y, w_out_a16, layer, x, gate)
        else:
            if kv is None:
                x = _move_rows(x, _permute_kernel, "permute_rows")
                kv = _norm_matmul(x, kv_gain, kv_mod[:, :d], kv_mod[:, d:], w_kv.astype(BF16))
            j = layer - n_a
            qz = _norm_matmul(x, norm_gain[layer], shift, scale, w_in_b16, j)
            y = _dilated_attention(qz, kv, bias)
            x = _out_proj(y, w_out_b16, j, x, gate, final_gain if layer == depth - 1 else None)
    assert kv is not None and depth > n_a
    return _move_rows(x, _unpermute_kernel, "unpermute_rows")
```

```python
import functools
import math

import jax
import jax.numpy as jnp
from jax import lax
from jax.experimental import pallas as pl
from jax.experimental.pallas import tpu as pltpu

F32 = jnp.float32
BF16 = jnp.bfloat16

RMS_EPS = 1e-6
L2_EPS = 1e-6
HEAD = 128
BF16_SUBLANES = 16
CHUNK = 64
DELTA_PACK = 4
INV_BASE = 8
CONV_WIDTH = 4
HALO = 8
DILATION_GROUPS =((128, 1), (512, 4), (2048, 16))
BAND = 128
ATT_TILE = 2048
PLANES = 16
PLANE_ROWS = ATT_TILE // PLANES
ATT_BATCH = 4
N_BUCKETS = 32
MAX_DISTANCE = 2048
NEG = -1e30
LOG2_E = math.log2(math.e)
VMEM_LIMIT = 56 * 1024 * 1024


def _cparams(n_axes):
    return pltpu.CompilerParams(
        dimension_semantics=("arbitrary",) * n_axes, vmem_limit_bytes=VMEM_LIMIT)


def _dot(a, b):
    return jnp.dot(a, b, preferred_element_type=F32)


def _dot_nt(a, b):
    return lax.dot_general(a, b, (((1,), (1,)), ((), ())), preferred_element_type=F32)


def _dot_tn(a, b):
    return lax.dot_general(a, b, (((0,), (0,)), ((), ())), preferred_element_type=F32)


def _sigmoid(x):
    return 1.0 / (1.0 + jnp.exp(-x))


def _silu(x):
    return x * _sigmoid(x)


def _softplus(x):
    return jnp.maximum(x, 0.0) + jnp.log(1.0 + jnp.exp(-jnp.abs(x)))


def _split3(x):
    hi = x.astype(BF16)
    r1 = x - hi.astype(F32)
    mid = r1.astype(BF16)
    lo = (r1 - mid.astype(F32)).astype(BF16)
    return hi, mid, lo


def _pick(n, prefs):
    for p in prefs:
        if n % p == 0:
            return p
    return n


def _mod_kernel(c_ref, w_ref, b_ref, o_ref):
    ca = _silu(c_ref[...])
    w = w_ref[...]
    c_hi, c_mid, c_lo = _split3(ca)
    w_hi = w.astype(BF16)
    w_lo = (w - w_hi.astype(F32)).astype(BF16)
    acc = _dot(c_hi, w_hi) + _dot(c_hi, w_lo) + _dot(c_mid, w_hi) + _dot(c_lo, w_hi)
    o_ref[...] = acc + b_ref[...]


def _mod_proj(c, w, b):
    nl, d, n = w.shape
    n_rows = c.shape[0]
    bsz = -(-n_rows // BF16_SUBLANES) * BF16_SUBLANES
    c = jnp.pad(c, ((0, bsz - n_rows), (0, 0)))
    tn = _pick(n, (512, 256, 128))
    out = pl.pallas_call(
        _mod_kernel,
        grid=(nl, n // tn),
        in_specs=[
            pl.BlockSpec((bsz, d), lambda l, j: (0, 0)),
            pl.BlockSpec((None, d, tn), lambda l, j: (l, 0, j)),
            pl.BlockSpec((None, 1, tn), lambda l, j: (l, 0, j)),
        ],
        out_specs=pl.BlockSpec((None, bsz, tn), lambda l, j: (l, 0, j)),
        out_shape=jax.ShapeDtypeStruct((nl, bsz, n), F32),
        compiler_params=_cparams(2),
        name="mod_proj",
    )(c, w, b.reshape(nl, 1, n))
    return out[:, :n_rows]


def _modnorm(x, gain, shift, scale):
    ms = jnp.mean(x * x, axis=-1, keepdims=True)
    return x * lax.rsqrt(ms + RMS_EPS) * (gain * (1.0 + scale)) + shift


def _nmm_plain_kernel(x_ref, g_ref, sh_ref, sc_ref, w_ref, o_ref, h_ref):
    @pl.when(pl.program_id(2) == 0)
    def _():
        h_ref[...] = _modnorm(x_ref[...], g_ref[...], sh_ref[...], sc_ref[...]).astype(BF16)

    o_ref[...] = _dot(h_ref[...], w_ref[...]).astype(o_ref.dtype)


def _nmm_side_kernel(x_ref, g_ref, sh_ref, sc_ref, w_ref, ws_ref, o_ref, os_ref, h_ref):
    @pl.when(pl.program_id(2) == 0)
    def _():
        h_ref[...] = _modnorm(x_ref[...], g_ref[...], sh_ref[...], sc_ref[...]).astype(BF16)
        os_ref[...] = _dot(h_ref[...], ws_ref[...])

    o_ref[...] = _dot(h_ref[...], w_ref[...]).astype(o_ref.dtype)


def _norm_matmul(x, gain, shift, scale, w, layer=0, col_start=0, n_cols=None, out_dtype=F32, w_side=None):
    bsz, seq, d = x.shape
    w = w[None] if w.ndim == 2 else w
    n = w.shape[2] - col_start if n_cols is None else n_cols
    tm = _pick(seq, (1024, 512, 256, 128))
    tn = _pick(math.gcd(n, col_start) if col_start else n, (1024, 512, 256, 128))
    first = col_start // tn
    in_specs = [
        pl.BlockSpec((None, tm, d), lambda b, i, j: (b, i, 0)),
        pl.BlockSpec((1, d), lambda b, i, j: (0, 0)),
        pl.BlockSpec((None, 1, d), lambda b, i, j: (b, 0, 0)),
        pl.BlockSpec((None, 1, d), lambda b, i, j: (b, 0, 0)),
        pl.BlockSpec((None, d, tn), lambda b, i, j: (layer, 0, first + j)),
    ]
    args = [x, gain.reshape(1, d), shift.reshape(bsz, 1, d), scale.reshape(bsz, 1, d), w]
    out_specs = pl.BlockSpec((None, tm, tn), lambda b, i, j: (b, i, j))
    out_shape = jax.ShapeDtypeStruct((bsz, seq, n), out_dtype)
    if w_side is not None:
        in_specs.append(pl.BlockSpec((None, d, HEAD), lambda b, i, j: (layer, 0, 0)))
        args.append(w_side)
        out_specs = [out_specs, pl.BlockSpec((None, tm, HEAD), lambda b, i, j: (b, i, 0))]
        out_shape = [out_shape, jax.ShapeDtypeStruct((bsz, seq, HEAD), F32)]
    return pl.pallas_call(
        _nmm_plain_kernel if w_side is None else _nmm_side_kernel,
        grid=(bsz, seq // tm, n // tn),
        in_specs=in_specs,
        out_specs=out_specs,
        out_shape=out_shape,
        scratch_shapes=[pltpu.VMEM((tm, d), BF16)],
        compiler_params=_cparams(3),
        name="norm_matmul",
    )(*args)


def _nmm_conv_kernel(x_ref, g_ref, sh_ref, sc_ref, w_ref, cw_ref, o_ref, h_ref, halo_ref, acc0_ref, acc1_ref,
                     *, n_q_tiles, n_qk_tiles, n_col_tiles, n_steps):
    s = pl.program_id(1)
    tm, tn = o_ref.shape
    jt = jnp.maximum(s - 1, 0) % n_col_tiles

    @pl.when((s % n_col_tiles == 0) & (s < n_steps - 1))
    def _():
        h_ref[...] = _modnorm(x_ref[...], g_ref[...], sh_ref[...], sc_ref[...]).astype(BF16)

    @pl.when(s == 0)
    def _():
        acc1_ref[...] = jnp.zeros(acc1_ref.shape, F32)
        halo_ref[...] = jnp.zeros(halo_ref.shape, F32)

    def step(done_ref, next_ref):
        prev_rows = halo_ref[jt]
        done_ref[0:HALO, :] = prev_rows
        acc = done_ref[HALO:, :]
        halo_ref[jt] = jnp.where(s > 0, acc[tm - HALO:, :], prev_rows)
        cw = cw_ref[...]
        y = cw[3:4, :] * acc
        for back in (1, 2, 3):
            y = y + cw[3 - back:4 - back, :] * done_ref[pl.ds(HALO - back, tm), :]
        y = _silu(y)
        is_qk = jt < n_qk_tiles
        qscale = jnp.where(jt < n_q_tiles, HEAD ** -0.5, 1.0).astype(F32)
        for hh in range(tn // HEAD):
            yh = y[:, hh * HEAD:(hh + 1) * HEAD]
            ss = jnp.sum(yh * yh, axis=-1, keepdims=True)
            o_ref[:, hh * HEAD:(hh + 1) * HEAD] = yh * jnp.where(is_qk, lax.rsqrt(ss + L2_EPS) * qscale, 1.0)
        next_ref[HALO:, :] = _dot(h_ref[...], w_ref[...])

    @pl.when(s % 2 == 0)
    def _():
        step(acc1_ref, acc0_ref)

    @pl.when(s % 2 == 1)
    def _():
        step(acc0_ref, acc1_ref)


def _norm_matmul_conv(x, gain, shift, scale, w, layer, conv_w, qk_width):
    bsz, seq, d = x.shape
    n = conv_w.shape[1]
    tm = _pick(seq, (1024, 512, 256, 128))
    tn = _pick(qk_width, (512, 256, 128))
    n_tiles = n // tn
    n_steps = (seq // tm) * n_tiles + 1
    kern = functools.partial(_nmm_conv_kernel, n_q_tiles=qk_width // tn, n_qk_tiles=2 * qk_width // tn,
                             n_col_tiles=n_tiles, n_steps=n_steps)
    cur = lambda s: jnp.minimum(s, n_steps - 2)
    old = lambda s: jnp.maximum(s - 1, 0)
    return pl.pallas_call(
        kern,
        grid=(bsz, n_steps),
        in_specs=[
            pl.BlockSpec((None, tm, d), lambda b, s: (b, cur(s) // n_tiles, 0)),
            pl.BlockSpec((1, d), lambda b, s: (0, 0)),
            pl.BlockSpec((None, 1, d), lambda b, s: (b, 0, 0)),
            pl.BlockSpec((None, 1, d), lambda b, s: (b, 0, 0)),
            pl.BlockSpec((None, d, tn), lambda b, s: (layer, 0, cur(s) % n_tiles)),
            pl.BlockSpec((CONV_WIDTH, tn), lambda b, s: (0, old(s) % n_tiles)),
        ],
        out_specs=pl.BlockSpec((None, tm, tn), lambda b, s: (b, old(s) // n_tiles, old(s) % n_tiles)),
        out_shape=jax.ShapeDtypeStruct((bsz, seq, n), F32),
        scratch_shapes=[pltpu.VMEM((tm, d), BF16), pltpu.VMEM((n_tiles, HALO, tn), F32),
                        pltpu.VMEM((HALO + tm, tn), F32), pltpu.VMEM((HALO + tm, tn), F32)],
        compiler_params=_cparams(2),
        name="norm_matmul_conv",
    )(x, gain.reshape(1, d), shift.reshape(bsz, 1, d), scale.reshape(bsz, 1, d), w, conv_w)


def _delta_kernel(q_ref, k_ref, v_ref, z_ref, ba_ref, par_ref, og_ref, o_ref,
                  state_ref, u_ref, w_ref, qd_ref, kd_ref, at_ref, a_ref,
                  *, heads_per_step, chunks_per_step, n_vheads, n_blocks, n_steps):
    s = pl.program_id(1)
    hg = jnp.minimum(s, n_steps - 2) // n_blocks
    new_group = jnp.maximum(s - 1, 0) % n_blocks == 0
    g_heads = heads_per_step
    c = CHUNK
    slot = s % 2
    prev = 1 - slot

    @pl.when(s == 0)
    def _():
        state_ref[...] = jnp.zeros(state_ref.shape, F32)
        u_ref[1] = jnp.zeros(u_ref.shape[1:], F32)
        w_ref[1] = jnp.zeros(w_ref.shape[1:], BF16)
        qd_ref[1] = jnp.zeros(qd_ref.shape[1:], BF16)
        kd_ref[1] = jnp.zeros(kd_ref.shape[1:], BF16)
        at_ref[1] = jnp.zeros(at_ref.shape[1:], BF16)
        a_ref[1] = jnp.zeros(a_ref.shape[1:], F32)

    row = lax.broadcasted_iota(jnp.int32, (c, c), 0)
    col = lax.broadcasted_iota(jnp.int32, (c, c), 1)
    lower = row >= col
    strict = row > col
    eye = row == col
    lane = lax.broadcasted_iota(jnp.int32, (c, HEAD), 1)
    tril = jnp.where(lower, 1.0, 0.0).astype(BF16)

    items = [(cb, h) for cb in range(chunks_per_step) for h in range(g_heads)]
    rows_of = lambda cb: slice(cb * c, (cb + 1) * c)
    cols_of = lambda i: slice(i * HEAD, (i + 1) * HEAD)

    pcols = DELTA_PACK * c
    prow = lax.broadcasted_iota(jnp.int32, (c, pcols), 0)
    plane = lax.broadcasted_iota(jnp.int32, (c, pcols), 1)
    pcol = jnp.bitwise_and(plane, c - 1)
    pblk = jnp.right_shift(plane, c.bit_length() - 1)
    p_lower = prow >= pcol
    p_strict = prow > pcol
    p_eye = prow == pcol
    in_block = [pblk == i for i in range(DELTA_PACK)]
    packs = [(cb, qd) for cb in range(chunks_per_step) for qd in range(g_heads // DELTA_PACK)]
    heads_of = lambda qd: [DELTA_PACK * qd + i for i in range(DELTA_PACK)]

    def pack_columns(cols):
        out = cols[-1]
        for i in reversed(range(DELTA_PACK - 1)):
            out = jnp.where(in_block[i], cols[i], out)
        return out

    block_keep = [jnp.where(in_block[i], 1.0, 0.0).astype(BF16) for i in range(DELTA_PACK)]

    def block_diag(mat):
        mat16 = mat.astype(BF16)
        return jnp.concatenate([mat16 * block_keep[i] for i in range(DELTA_PACK)], axis=0)

    def same_block(size):
        shift = size.bit_length() - 1
        return jnp.right_shift(prow, shift) == jnp.right_shift(pcol, shift)

    base_mask = same_block(INV_BASE)
    join_mask = {}
    size = INV_BASE
    while size < c:
        join_mask[size] = same_block(2 * size) & ~same_block(size)
        size *= 2

    comb, kk, qk = {}, {}, {}
    xm, rhs, fac, lfull = {}, {}, {}, {}

    def gates_stage():
        for cb in range(chunks_per_step):
            ba = ba_ref[rows_of(cb), :]
            beta_full = _sigmoid(ba)
            g_full = -jnp.exp(par_ref[0:1, :]) * _softplus(ba + par_ref[1:2, :])
            g_hi, g_mid, g_lo = _split3(g_full)
            gc_full = _dot(tril, g_hi) + _dot(tril, g_mid) + _dot(tril, g_lo)
            comb[cb] = jnp.where(lane < n_vheads, beta_full, gc_full)

    def gram_stage():
        for cb in range(chunks_per_step):
            for kh_idx in range(g_heads // 2):
                k16 = k_ref[rows_of(cb), cols_of(kh_idx)].astype(BF16)
                q16 = q_ref[rows_of(cb), cols_of(kh_idx)].astype(BF16)
                k_twice = jnp.concatenate([k16, k16], axis=0)
                both = _dot_nt(jnp.concatenate([k16, q16], axis=0), k_twice)
                kk[cb, kh_idx] = both[:c, :]
                qk[cb, kh_idx] = both[c:, :]

    def decay_stage():
        for cb, qd in packs:
            betas, gcs = [], []
            for h in heads_of(qd):
                head = hg * g_heads + h
                kh = k_ref[rows_of(cb), cols_of(h // 2)]
                qh = q_ref[rows_of(cb), cols_of(h // 2)]
                vh = v_ref[rows_of(cb), cols_of(h)]
                beta_c = jnp.sum(jnp.where(lane == head, comb[cb], 0.0), axis=-1, keepdims=True)
                gc_c = jnp.sum(jnp.where(lane == head + n_vheads, comb[cb], 0.0), axis=-1, keepdims=True)
                gl = gc_c[c - 1:c, :]
                eg = jnp.exp(gc_c)
                rhs[cb, h] = jnp.concatenate([vh * beta_c, kh * (beta_c * eg)], axis=1)
                qd_ref[slot, cb, h] = (qh * eg).astype(BF16)
                kd_ref[slot, cb, h] = (kh * jnp.exp(gl - gc_c)).astype(BF16)
                a_ref[slot, cb, h] = jnp.broadcast_to(jnp.exp(gl), (1, HEAD))
                betas.append(beta_c)
                gcs.append(gc_c)
            first_kh = DELTA_PACK * qd // 2
            kk_p = jnp.concatenate([kk[cb, first_kh + i] for i in range(DELTA_PACK // 2)], axis=1)
            qk_p = jnp.concatenate([qk[cb, first_kh + i] for i in range(DELTA_PACK // 2)], axis=1)
            gc_p = pack_columns(gcs)
            gc_r = jnp.sum(jnp.where(p_eye, gc_p, 0.0), axis=0, keepdims=True)
            decay = jnp.exp(jnp.where(p_lower, gc_p - gc_r, NEG))
            lmat = jnp.where(p_strict, kk_p * pack_columns(betas) * decay, 0.0)
            at_ref[slot, cb, qd] = block_diag(qk_p * decay)
            lfull[cb, qd] = lmat
            ldiag = jnp.where(base_mask, lmat, 0.0)
            xm[cb, qd] = -ldiag
            fac[cb, qd] = ldiag

    def base_square_stage():
        for pk in packs:
            fac[pk] = _dot(fac[pk].astype(BF16), block_diag(fac[pk]))

    def base_round_stage():
        for pk in packs:
            both = jnp.concatenate([fac[pk].astype(BF16), xm[pk].astype(BF16)], axis=0)
            out = _dot(both, block_diag(fac[pk]))
            xm[pk] = xm[pk] + fac[pk] + out[c:, :]
            fac[pk] = out[:c, :]

    def base_last_stage():
        for pk in packs:
            xm[pk] = xm[pk] + fac[pk] + _dot(xm[pk].astype(BF16), block_diag(fac[pk]))

    def join_right_stage(size):
        def run():
            for pk in packs:
                joint = jnp.where(join_mask[size], lfull[pk], 0.0)
                fac[pk] = joint + _dot(joint.astype(BF16), block_diag(xm[pk]))
        return run

    def join_left_stage():
        for pk in packs:
            xm[pk] = xm[pk] - fac[pk] - _dot(xm[pk].astype(BF16), block_diag(fac[pk]))

    def solve_stage():
        for cb, qd in packs:
            rhs_rows = jnp.concatenate([rhs[cb, h].astype(BF16) for h in heads_of(qd)], axis=0)
            sol = _dot(block_diag(xm[cb, qd]), rhs_rows)
            for i, h in enumerate(heads_of(qd)):
                uw = rhs[cb, h] + sol[i * c:(i + 1) * c, :]
                u_ref[slot, cb, h] = uw[:, :HEAD]
                w_ref[slot, cb, h] = uw[:, HEAD:].astype(BF16)

    factor_stages = [gates_stage, gram_stage, decay_stage, base_square_stage]
    factor_stages += [base_round_stage] * (INV_BASE.bit_length() - 3)
    factor_stages.append(base_last_stage)
    size = INV_BASE
    while size < c:
        factor_stages += [join_right_stage(size), join_left_stage]
        size *= 2
    factor_stages.append(solve_stage)

    og = og_ref[...]
    states = [jnp.where(new_group, 0.0, state_ref[h]) for h in range(g_heads)]
    wqs, v16 = {}, {}

    def read_stage(cb):
        def run():
            for h in range(g_heads):
                wq = jnp.concatenate([w_ref[prev, cb, h], qd_ref[prev, cb, h]], axis=0)
                wqs[h] = _dot(wq, states[h].astype(BF16))
        return run

    def update_stage(cb):
        def run():
            for h in range(g_heads):
                v16[h] = (u_ref[prev, cb, h] - wqs[h][:c, :]).astype(BF16)
            intra = {}
            for qd in range(g_heads // DELTA_PACK):
                v_rows = jnp.concatenate([v16[h] for h in heads_of(qd)], axis=0)
                av = _dot(at_ref[prev, cb, qd], v_rows)
                for i, h in enumerate(heads_of(qd)):
                    intra[h] = av[i * c:(i + 1) * c, :]
            for h in range(g_heads):
                o = wqs[h][c:, :] + intra[h]
                states[h] = states[h] * a_ref[prev, cb, h] + _dot_tn(kd_ref[prev, cb, h], v16[h])
                ms = jnp.mean(o * o, axis=-1, keepdims=True)
                zh = z_ref[rows_of(cb), cols_of(h)]
                o_ref[rows_of(cb), cols_of(h)] = (o * lax.rsqrt(ms + RMS_EPS) * og * _silu(zh)).astype(o_ref.dtype)
        return run

    state_stages = []
    for cb in range(chunks_per_step):
        state_stages += [read_stage(cb), update_stage(cb)]

    stride = max(1, len(factor_stages) // (len(state_stages) + 1))
    for idx, stage in enumerate(factor_stages):
        stage()
        if idx % stride == stride - 1 and state_stages:
            state_stages.pop(0)()
    for stage in state_stages:
        stage()
    for h in range(g_heads):
        state_ref[h] = states[h]


def _delta_rule(qkv, z, ba, gate_par, o_gain, qk_width, heads_per_step=8, chunks_per_step=4):
    bsz, seq, _ = qkv.shape
    v_width = z.shape[-1]
    n_vheads = v_width // HEAD
    g = min(heads_per_step, n_vheads)
    assert g % DELTA_PACK == 0 and n_vheads % g == 0
    gk = g // 2
    cbs = chunks_per_step
    rows = cbs * CHUNK
    n_blocks = seq // rows
    n_kblocks = qk_width // (gk * HEAD)
    n_steps = (n_vheads // g) * n_blocks + 1
    kern = functools.partial(_delta_kernel, heads_per_step=g, chunks_per_step=cbs, n_vheads=n_vheads,
                             n_blocks=n_blocks, n_steps=n_steps)
    cur = lambda s: jnp.minimum(s, n_steps - 2)
    old = lambda s: jnp.maximum(s - 1, 0)
    blk = lambda t: t % n_blocks
    grp = lambda t: t // n_blocks
    v_first = 2 * qk_width // (g * HEAD)
    return pl.pallas_call(
        kern,
        grid=(bsz, n_steps),
        in_specs=[
            pl.BlockSpec((None, rows, gk * HEAD), lambda b, s: (b, blk(cur(s)), grp(cur(s)))),
            pl.BlockSpec((None, rows, gk * HEAD), lambda b, s: (b, blk(cur(s)), n_kblocks + grp(cur(s)))),
            pl.BlockSpec((None, rows, g * HEAD), lambda b, s: (b, blk(cur(s)), v_first + grp(cur(s)))),
            pl.BlockSpec((None, rows, g * HEAD), lambda b, s: (b, blk(old(s)), grp(old(s)))),
            pl.BlockSpec((None, rows, HEAD), lambda b, s: (b, blk(cur(s)), 0)),
            pl.BlockSpec((2, HEAD), lambda b, s: (0, 0)),
            pl.BlockSpec((1, HEAD), lambda b, s: (0, 0)),
        ],
        out_specs=pl.BlockSpec((None, rows, g * HEAD), lambda b, s: (b, blk(old(s)), grp(old(s)))),
        out_shape=jax.ShapeDtypeStruct((bsz, seq, v_width), BF16),
        scratch_shapes=[
            pltpu.VMEM((g, HEAD, HEAD), F32),
            pltpu.VMEM((2, cbs, g, CHUNK, HEAD), F32),
            pltpu.VMEM((2, cbs, g, CHUNK, HEAD), BF16),
            pltpu.VMEM((2, cbs, g, CHUNK, HEAD), BF16),
            pltpu.VMEM((2, cbs, g, CHUNK, HEAD), BF16),
            pltpu.VMEM((2, cbs, g // DELTA_PACK, DELTA_PACK * CHUNK, DELTA_PACK * CHUNK), BF16),
            pltpu.VMEM((2, cbs, g, 1, HEAD), F32),
        ],
        compiler_params=_cparams(2),
        name="delta_rule",
    )(qkv, qkv, qkv, z, ba, gate_par, o_gain.reshape(1, HEAD))


def _out_proj_kernel(y_ref, w_ref, x_ref, gate_ref, o_ref):
    o_ref[...] = x_ref[...] + gate_ref[...] * _dot(y_ref[...].astype(BF16), w_ref[...])


def _out_proj_norm_kernel(y_ref, w_ref, x_ref, gate_ref, g_ref, o_ref):
    x = x_ref[...] + gate_ref[...] * _dot(y_ref[...].astype(BF16), w_ref[...])
    ms = jnp.mean(x * x, axis=-1, keepdims=True)
    o_ref[...] = x * lax.rsqrt(ms + RMS_EPS) * g_ref[...]


def _out_proj(y, w, layer, x, gate, final_gain=None):
    bsz, seq, k = y.shape
    d = w.shape[2]
    fused_norm = final_gain is not None
    tm = _pick(seq, (512, 256, 128)) if fused_norm else _pick(seq, (1024, 512, 256, 128))
    tn = d if fused_norm else _pick(d, (512, 256, 128))
    in_specs = [
        pl.BlockSpec((None, tm, k), lambda b, i, j: (b, i, 0)),
        pl.BlockSpec((None, k, tn), lambda b, i, j: (layer, 0, j)),
        pl.BlockSpec((None, tm, tn), lambda b, i, j: (b, i, j)),
        pl.BlockSpec((None, 1, tn), lambda b, i, j: (b, 0, j)),
    ]
    args = [y, w, x, gate.reshape(bsz, 1, d)]
    if fused_norm:
        in_specs.append(pl.BlockSpec((1, d), lambda b, i, j: (0, 0)))
        args.append(final_gain.reshape(1, d))
    return pl.pallas_call(
        _out_proj_norm_kernel if fused_norm else _out_proj_kernel,
        grid=(bsz, seq // tm, d // tn),
        in_specs=in_specs,
        out_specs=pl.BlockSpec((None, tm, tn), lambda b, i, j: (b, i, j)),
        out_shape=jax.ShapeDtypeStruct((bsz, seq, d), F32),
        compiler_params=_cparams(3),
        name="out_proj_norm" if fused_norm else "out_proj",
    )(*args)


def _block_slabs(dil, res, nb):
    blocks_per_res = ATT_TILE // (BAND * dil)
    slab = PLANE_ROWS // blocks_per_res
    return [((res + dil * a) * PLANE_ROWS + slab * nb, slab) for a in range(PLANES // dil)]


def _block_positions(dil):
    n_slabs = PLANES // dil
    slab = BAND // n_slabs
    return [n_slabs * i + a for a in range(n_slabs) for i in range(slab)]


def _attn_blocks():
    blocks = []
    for gi, (_, dil) in enumerate(DILATION_GROUPS):
        blocks_per_res = ATT_TILE // (BAND * dil)
        for res in range(dil):
            for nb in range(blocks_per_res):
                prev_nb = (nb - 1) % blocks_per_res
                blocks.append((gi, _block_slabs(dil, res, nb), _block_slabs(dil, res, prev_nb), nb == 0))
    return blocks


def _gather_rows(ref, slabs):
    parts = [ref[start:start + size, :] for start, size in slabs]
    return parts[0] if len(parts) == 1 else jnp.concatenate(parts, axis=0)


def _scatter_rows(ref, slabs, val):
    off = 0
    for start, size in slabs:
        ref[start:start + size, :] = val[off:off + size, :].astype(ref.dtype)
        off += size


def _attn_kernel(q0_ref, q1_ref, q2_ref, kp_ref, ko_ref, vp_ref, vo_ref, z_ref, bias_ref, o_ref,
                 m_ref, l_ref, acc_ref):
    first_tile = pl.program_id(1) == 0
    col = lax.broadcasted_iota(jnp.int32, (1, 2 * BAND), 1)
    head_penalty = jnp.where((col < BAND) & first_tile, NEG, 0.0)
    scale = HEAD ** -0.5 * LOG2_E
    q_refs = (q0_ref, q1_ref, q2_ref)
    last_group = len(DILATION_GROUPS) - 1

    def score_stage(blk):
        gi, q_slabs, prev_slabs, prev_tile = blk
        q16 = _gather_rows(q_refs[gi], q_slabs).astype(BF16)
        k_prev = _gather_rows(kp_ref if prev_tile else ko_ref, prev_slabs)
        k16 = jnp.concatenate([k_prev, _gather_rows(ko_ref, q_slabs)], axis=0).astype(BF16)
        return _dot_nt(q16, k16)

    def softmax_stage(blk, s):
        gi, _, _, prev_tile = blk
        bias = bias_ref[gi]
        if prev_tile:
            bias = bias + head_penalty
        s = s * scale + bias
        m = jnp.max(s, axis=-1, keepdims=True)
        p = jnp.exp2(s - m)
        return m, jnp.sum(p, axis=-1, keepdims=True), p.astype(BF16)

    def value_stage(blk, p16):
        _, q_slabs, prev_slabs, prev_tile = blk
        v_prev = _gather_rows(vp_ref if prev_tile else vo_ref, prev_slabs)
        v16 = jnp.concatenate([v_prev, _gather_rows(vo_ref, q_slabs)], axis=0).astype(BF16)
        return _dot(p16, v16)

    def merge_stage(blk, m, l, pv):
        gi, q_slabs, _, _ = blk
        if gi == 0:
            m_new = jnp.broadcast_to(m, (BAND, HEAD))
            l_new = jnp.broadcast_to(l, (BAND, HEAD))
            acc_new = pv
        else:
            m_old = _gather_rows(m_ref, q_slabs)
            m_new = jnp.maximum(m_old, m)
            w_old = jnp.exp2(m_old - m_new)
            w_blk = jnp.exp2(m - m_new)
            l_new = w_old * _gather_rows(l_ref, q_slabs) + w_blk * l
            acc_new = w_old * _gather_rows(acc_ref, q_slabs) + w_blk * pv
        if gi == last_group:
            _scatter_rows(o_ref, q_slabs, acc_new / l_new * _silu(_gather_rows(z_ref, q_slabs)))
        else:
            _scatter_rows(m_ref, q_slabs, m_new)
            _scatter_rows(l_ref, q_slabs, l_new)
            _scatter_rows(acc_ref, q_slabs, acc_new)

    blocks = _attn_blocks()
    batches = [blocks[i:i + ATT_BATCH] for i in range(0, len(blocks), ATT_BATCH)]
    scores_next = [score_stage(blk) for blk in batches[0]]
    for bi, batch in enumerate(batches):
        scores = scores_next
        if bi + 1 < len(batches):
            scores_next = [score_stage(blk) for blk in batches[bi + 1]]
        stats = [softmax_stage(blk, s) for blk, s in zip(batch, scores)]
        pvs = [value_stage(blk, p16) for blk, (_, _, p16) in zip(batch, stats)]
        for blk, (m, l, _), pv in zip(batch, stats, pvs):
            merge_stage(blk, m, l, pv)


def _dilated_attention(qz, kv, bias):
    bsz, seq, _ = qz.shape
    width = kv.shape[-1] // 2
    heads = width // HEAD
    t = ATT_TILE
    row_tile = lambda b, i, h: (b, i, h)

    def q_spec(gi):
        return pl.BlockSpec((None, t, HEAD), lambda b, i, h: (b, i, gi * heads + h))

    return pl.pallas_call(
        _attn_kernel,
        grid=(bsz, seq // t, heads),
        in_specs=[
            q_spec(0), q_spec(1), q_spec(2),
            pl.BlockSpec((None, t, HEAD), lambda b, i, h: (b, jnp.maximum(i - 1, 0), h)),
            pl.BlockSpec((None, t, HEAD), row_tile),
            pl.BlockSpec((None, t, HEAD), lambda b, i, h: (b, jnp.maximum(i - 1, 0), heads + h)),
            pl.BlockSpec((None, t, HEAD), lambda b, i, h: (b, i, heads + h)),
            pl.BlockSpec((None, t, HEAD), lambda b, i, h: (b, i, 3 * heads + h)),
            pl.BlockSpec((3, None, BAND, 2 * BAND), lambda b, i, h: (0, h, 0, 0)),
        ],
        out_specs=pl.BlockSpec((None, t, HEAD), row_tile),
        out_shape=jax.ShapeDtypeStruct((bsz, seq, width), BF16),
        scratch_shapes=[pltpu.VMEM((t, HEAD), F32)] * 3,
        compiler_params=_cparams(3),
        name="dilated_attention",
    )(qz, qz, qz, kv, kv, kv, kv, qz, bias)


def _t5_bucket(dist):
    max_exact = N_BUCKETS // 2
    n = jnp.maximum(dist, 0)
    large = max_exact + (jnp.log(jnp.maximum(n, 1).astype(F32) / max_exact)
                         / math.log(MAX_DISTANCE / max_exact)
                         * (N_BUCKETS - max_exact)).astype(jnp.int32)
    large = jnp.minimum(large, N_BUCKETS - 1)
    return jnp.where(n < max_exact, n, large)


def _band_bias(rel_bias, heads):
    out = []
    for gi, (_, dil) in enumerate(DILATION_GROUPS):
        pos = jnp.asarray(_block_positions(dil), jnp.int32)
        key_pos = jnp.concatenate([pos, pos + BAND])
        rel = pos[:, None] + BAND - key_pos[None, :]
        valid = (rel >= 0) & (rel <= BAND)
        table = rel_bias[:, gi * heads:(gi + 1) * heads].astype(F32)
        onehot = (_t5_bucket(rel * dil)[None] == jnp.arange(N_BUCKETS)[:, None, None]).astype(F32)
        b = jnp.einsum("nh,nqk->hqk", table, onehot, precision=lax.Precision.HIGHEST)
        out.append(jnp.where(valid[None], b * LOG2_E, NEG))
    return jnp.stack(out, axis=0)


def _permute_kernel(x_ref, o_ref):
    for r in range(PLANES):
        o_ref[r * PLANE_ROWS:(r + 1) * PLANE_ROWS, :] = x_ref[pl.ds(r, PLANE_ROWS, stride=PLANES), :]


def _unpermute_kernel(x_ref, o_ref):
    for r in range(PLANES):
        o_ref[pl.ds(r, PLANE_ROWS, stride=PLANES), :] = x_ref[r * PLANE_ROWS:(r + 1) * PLANE_ROWS, :]


def _move_rows(x, body, name):
    bsz, seq, d = x.shape
    spec = pl.BlockSpec((None, ATT_TILE, HEAD), lambda b, t, j: (b, t, j))
    return pl.pallas_call(
        body,
        grid=(bsz, seq // ATT_TILE, d // HEAD),
        in_specs=[spec],
        out_specs=spec,
        out_shape=jax.ShapeDtypeStruct((bsz, seq, d), x.dtype),
        compiler_params=_cparams(3),
        name=name,
    )(x)


def kernel(x, c, norm_gain, w_mod, b_mod, w_in_a, conv_w_a, a_log, dt_bias, o_norm_a, w_out_a,
           kv_gain, w_kv_mod, b_kv_mod, w_kv, w_in_b, w_out_b, rel_bias, final_gain):
    bsz, seq, d = x.shape
    depth = norm_gain.shape[0]
    n_a = w_in_a.shape[0]
    qk_width = d
    v_width = w_out_a.shape[1]
    n_vheads = v_width // HEAD
    conv_ch = 2 * qk_width + v_width
    dil_width = w_out_b.shape[1]
    dil_heads = dil_width // HEAD
    assert seq % ATT_TILE == 0 and seq % CHUNK == 0 and 2 * n_vheads <= HEAD

    mods = _mod_proj(c, w_mod, b_mod)
    kv_mod = _mod_proj(c, w_kv_mod[None], b_kv_mod[None])[0]
    bias = _band_bias(rel_bias, dil_heads)

    w_in_a16, w_out_a16 = w_in_a.astype(BF16), w_out_a.astype(BF16)
    w_in_b16, w_out_b16 = w_in_b.astype(BF16), w_out_b.astype(BF16)
    w_ba16 = jnp.pad(w_in_a[:, :, conv_ch + v_width:].astype(BF16), ((0, 0), (0, 0), (0, HEAD - 2 * n_vheads)))

    kv = None
    for layer in range(depth):
        shift, scale, gate = mods[layer, :, :d], mods[layer, :, d:2 * d], mods[layer, :, 2 * d:]
        if layer < n_a:
            qkv = _norm_matmul_conv(x, norm_gain[layer], shift, scale, w_in_a16, layer, conv_w_a[layer], qk_width)
            z, ba = _norm_matmul(x, norm_gain[layer], shift, scale, w_in_a16, layer, col_start=conv_ch,
                                 n_cols=v_width, w_side=w_ba16)
            gate_par = jnp.zeros((2, HEAD), F32)
            gate_par = gate_par.at[0, n_vheads:2 * n_vheads].set(a_log[layer])
            gate_par = gate_par.at[1, n_vheads:2 * n_vheads].set(dt_bias[layer])
            y = _delta_rule(qkv, z, ba, gate_par, o_norm_a[layer], qk_width)
            x = _out_proj(y, w_out_a16, layer, x, gate)
        else:
            if kv is None:
                x = _move_rows(x, _permute_kernel, "permute_rows")
                kv = _norm_matmul(x, kv_gain, kv_mod[:, :d], kv_mod[:, d:], w_kv.astype(BF16))
            j = layer - n_a
            qz = _norm_matmul(x, norm_gain[layer], shift, scale, w_in_b16, j)
            y = _dilated_attention(qz, kv, bias)
            x = _out_proj(y, w_out_b16, j, x, gate, final_gain if layer == depth - 1 else None)
    assert kv is not None and depth > n_a
    return _move_rows(x, _unpermute_kernel, "unpermute_rows")
```
